```python
import math
import jax
import jax.numpy as jnp
from jax import lax
import numpy as np

D_MODEL = 2048
BATCH = 2
SEQ = 8192
DEPTH = 2

DEEPNORM_ALPHA = (2.0 * DEPTH) ** 0.25
DEEPNORM_BETA = (8.0 * DEPTH) ** -0.25
LN_EPS = 1e-5

SSD_D_INNER = D_MODEL
SSD_HEAD_DIM = 64
SSD_N_HEADS = SSD_D_INNER // SSD_HEAD_DIM
SSD_N_GROUPS = 4
SSD_D_STATE = 128
SSD_CONV = 4
SSD_CHUNK = 64
SSD_GN = SSD_N_GROUPS * SSD_D_STATE
SSD_XBC = SSD_D_INNER + 2 * SSD_GN
SSD_EPS = 1e-5

RWKV_D = D_MODEL
RWKV_HEAD_DIM = 64
RWKV_N_HEADS = RWKV_D // RWKV_HEAD_DIM
RWKV_DECAY_LORA = max(32, int(round(1.8 * RWKV_D ** 0.5 / 32)) * 32)
RWKV_A_LORA = max(32, int(round(1.8 * RWKV_D ** 0.5 / 32)) * 32)
RWKV_GATE_LORA = max(32, int(round(0.6 * RWKV_D ** 0.8 / 32)) * 32)
RWKV_COLS = 3 * RWKV_D + RWKV_DECAY_LORA + RWKV_A_LORA + RWKV_GATE_LORA
RWKV_SPLITS = (RWKV_D, 2 * RWKV_D, 3 * RWKV_D, 3 * RWKV_D + RWKV_DECAY_LORA,
               3 * RWKV_D + RWKV_DECAY_LORA + RWKV_A_LORA)
RWKV_LN_EPS = 64e-5

MLSTM_N_HEADS = 8
MLSTM_DV = D_MODEL
MLSTM_DQK = D_MODEL // 2
MLSTM_HEAD_DV = MLSTM_DV // MLSTM_N_HEADS
MLSTM_HEAD_DQK = MLSTM_DQK // MLSTM_N_HEADS
MLSTM_CONV = 4
MLSTM_CHUNK = 64
MLSTM_EPS = 1e-6

N_BRANCHES = 3
FFN_D = int(round(8 * D_MODEL / 3 / 128)) * 128
FFN_CONV = 3

IN_SIZES = (SSD_D_INNER, SSD_XBC, SSD_N_HEADS,
            RWKV_COLS,
            2 * MLSTM_DQK, MLSTM_DV, MLSTM_DV,
            MLSTM_N_HEADS, MLSTM_N_HEADS,
            N_BRANCHES * D_MODEL)
IN_SPLITS = tuple(sum(IN_SIZES[:i + 1]) for i in range(len(IN_SIZES) - 1))
D_IN = sum(IN_SIZES)

kernel_name = "hybrid_ssd_rwkv7_mlstm_deepnorm"


def layer_norm(x, g, b, eps=LN_EPS):
    xf = x.astype(jnp.float32)
    mu = jnp.mean(xf, axis=-1, keepdims=True)
    var = jnp.mean(jnp.square(xf - mu), axis=-1, keepdims=True)
    return ((xf - mu) * lax.rsqrt(var + eps) * g + b).astype(x.dtype)


def group_rms_norm(y, groups, eps):
    shp = y.shape
    yg = y.reshape(shp[:-1] + (groups, shp[-1] // groups))
    yg = yg * lax.rsqrt(jnp.mean(jnp.square(yg), axis=-1, keepdims=True) + eps)
    return yg.reshape(shp)


def head_layer_norm(y, eps):
    mu = jnp.mean(y, axis=-1, keepdims=True)
    var = jnp.mean(jnp.square(y - mu), axis=-1, keepdims=True)
    return (y - mu) * lax.rsqrt(var + eps)


def causal_dwconv(x, w, b):
    width, ch = w.shape
    y = lax.conv_general_dilated(x, w[:, None, :], window_strides=(1,),
                                 padding=((width - 1, 0),),
                                 dimension_numbers=("NWC", "WIO", "NWC"),
                                 feature_group_count=ch)
    return y + b


def token_shift(p):
    return jnp.pad(p, ((0, 0), (1, 0), (0, 0)))[:, :-1]


def ssd_chunked_scan(xdt, adt, bmat, cmat):
    bsz, seq = xdt.shape[:2]
    L = SSD_CHUNK
    nc = seq // L
    hg = SSD_N_HEADS // SSD_N_GROUPS
    x = xdt.reshape(bsz, nc, L, SSD_N_GROUPS, hg, SSD_HEAD_DIM)
    a_cs = jnp.cumsum(adt.reshape(bsz, nc, L, SSD_N_GROUPS, hg), axis=2)
    bm = bmat.reshape(bsz, nc, L, SSD_N_GROUPS, SSD_D_STATE)
    cm = cmat.reshape(bsz, nc, L, SSD_N_GROUPS, SSD_D_STATE)
    causal = jnp.tril(jnp.ones((L, L), dtype=bool))
    seg = a_cs[:, :, :, None] - a_cs[:, :, None, :]
    decay = jnp.exp(jnp.where(causal[:, :, None, None], seg, -jnp.inf))
    scores = jnp.einsum("bclgn,bcsgn->bclsg", cm, bm)
    y_diag = jnp.einsum("bclsgh,bcsghp->bclghp", scores[..., None] * decay, x)
    decay_to_end = jnp.exp(a_cs[:, :, -1:] - a_cs)
    chunk_states = jnp.einsum("bclgn,bclghp->bcghpn", bm, decay_to_end[..., None] * x)
    chunk_decay = jnp.exp(a_cs[:, :, -1])

    def step(state, inp):
        dec, new = inp
        return dec[..., None, None] * state + new, state

    init = jnp.zeros((bsz, SSD_N_GROUPS, hg, SSD_HEAD_DIM, SSD_D_STATE), xdt.dtype)
    _, prev = lax.scan(step, init, (jnp.moveaxis(chunk_decay, 1, 0), jnp.moveaxis(chunk_states, 1, 0)))
    prev = jnp.moveaxis(prev, 0, 1)
    y_off = jnp.einsum("bclgn,bcghpn->bclghp", cm, prev) * jnp.exp(a_cs)[..., None]
    return (y_diag + y_off).reshape(bsz, seq, SSD_N_HEADS, SSD_HEAD_DIM)


def ssd_mixer(z, xbc, dt_raw, conv_w, conv_b, dt_bias, a_log, d_skip, norm_w):
    bsz, seq, _ = z.shape
    xbc = jax.nn.silu(causal_dwconv(xbc, conv_w, conv_b)).astype(jnp.float32)
    xs, bs, cs = jnp.split(xbc, (SSD_D_INNER, SSD_D_INNER + SSD_GN), axis=-1)
    xh = xs.reshape(bsz, seq, SSD_N_HEADS, SSD_HEAD_DIM)
    dt = jax.nn.softplus(dt_raw.astype(jnp.float32) + dt_bias)
    a = -jnp.exp(a_log.astype(jnp.float32))
    y = ssd_chunked_scan(xh * dt[..., None], dt * a,
                         bs.reshape(bsz, seq, SSD_N_GROUPS, SSD_D_STATE),
                         cs.reshape(bsz, seq, SSD_N_GROUPS, SSD_D_STATE))
    y = (y + xh * d_skip[:, None]).reshape(bsz, seq, SSD_D_INNER)
    y = group_rms_norm(y * jax.nn.silu(z.astype(jnp.float32)), SSD_N_GROUPS, SSD_EPS) * norm_w
    return y.astype(z.dtype)


def rwkv7_recurrence(r, w, k, v, a, b):
    bsz, _, nh, dk = r.shape

    def step(s, inp):
        r_t, w_t, k_t, v_t, a_t, b_t = inp
        sa = jnp.einsum("bhij,bhj->bhi", s, a_t)
        s = s * w_t[:, :, None, :] + v_t[..., :, None] * k_t[..., None, :] + sa[..., :, None] * b_t[..., None, :]
        return s, jnp.einsum("bhij,bhj->bhi", s, r_t)

    xs = tuple(jnp.moveaxis(t, 1, 0) for t in (r, w, k, v, a, b))
    _, y = lax.scan(step, jnp.zeros((bsz, nh, dk, dk), r.dtype), xs)
    return jnp.moveaxis(y, 0, 1)


def rwkv7_mixer(p, mu, w0, w2, a0, a2, g2, k_k, k_a, r_k, ln_w, ln_b):
    out_dtype = p.dtype
    bsz, seq, _ = p.shape
    p = p.astype(jnp.float32)
    p = p + (token_shift(p) - p) * mu
    r, k, v, w_lo, a_lo, g_lo = jnp.split(p, RWKV_SPLITS, axis=-1)
    log_w = -jax.nn.softplus(-(w0 + jnp.tanh(w_lo) @ w2)) - 0.5
    decay = jnp.exp(-jnp.exp(log_w))
    a = jax.nn.sigmoid(a0 + a_lo @ a2)
    g = jax.nn.sigmoid(g_lo) @ g2

    def heads(t):
        return t.reshape(bsz, seq, RWKV_N_HEADS, RWKV_HEAD_DIM)

    kk = heads(k * k_k)
    kk = kk * lax.rsqrt(jnp.maximum(jnp.sum(kk * kk, axis=-1, keepdims=True), 1e-24))
    kh = heads(k * (1.0 + (a - 1.0) * k_a))
    rh, vh, ah = heads(r), heads(v), heads(a)
    y = rwkv7_recurrence(rh, heads(decay), kh, vh, -kk, kk * ah)
    y = head_layer_norm(y, RWKV_LN_EPS).reshape(bsz, seq, RWKV_D) * ln_w + ln_b
    y = y + (jnp.sum(rh * kh * r_k, axis=-1, keepdims=True) * vh).reshape(bsz, seq, RWKV_D)
    return (y * g).astype(out_dtype)


def mlstm_chunked(q, k, v, log_i, log_f):
    bsz, seq, nh, dk = q.shape
    dv = v.shape[-1]
    L = MLSTM_CHUNK
    nc = seq // L

    def chunks(t):
        return jnp.moveaxis(t.reshape((bsz, nc, L, nh) + t.shape[3:]), 1, 0).swapaxes(2, 3)

    causal = jnp.tril(jnp.ones((L, L), dtype=bool))

    def step(carry, inp):
        c_st, n_st, m_st = carry
        qc, kc, vc, lic, lfc = inp
        b_cum = jnp.cumsum(lfc, axis=-1)
        log_d = jnp.where(causal, b_cum[..., :, None] - b_cum[..., None, :] + lic[..., None, :], -jnp.inf)
        m_inter = b_cum + m_st[..., None]
        m_t = jnp.maximum(jnp.max(log_d, axis=-1), m_inter)
        w_intra = jnp.exp(log_d - m_t[..., None]) * jnp.einsum("bhld,bhsd->bhls", qc, kc)
        s_inter = jnp.exp(m_inter - m_t)
        num = jnp.einsum("bhls,bhsv->bhlv", w_intra, vc) + s_inter[..., None] * jnp.einsum("bhvd,bhld->bhlv", c_st, qc)
        den = jnp.sum(w_intra, axis=-1) + s_inter * jnp.einsum("bhd,bhld->bhl", n_st, qc)
        h = num / jnp.maximum(jnp.abs(den), jnp.exp(-m_t))[..., None]
        b_tot = b_cum[..., -1]
        log_w = b_tot[..., None] - b_cum + lic
        m_new = jnp.maximum(b_tot + m_st, jnp.max(log_w, axis=-1))
        w_end = jnp.exp(log_w - m_new[..., None])
        s_old = jnp.exp(b_tot + m_st - m_new)
        c_new = s_old[..., None, None] * c_st + jnp.einsum("bhsv,bhsd->bhvd", w_end[..., None] * vc, kc)
        n_new = s_old[..., None] * n_st + jnp.einsum("bhs,bhsd->bhd", w_end, kc)
        return (c_new, n_new, m_new), h

    init = (jnp.zeros((bsz, nh, dv, dk), q.dtype), jnp.zeros((bsz, nh, dk), q.dtype),
            jnp.zeros((bsz, nh), q.dtype))
    _, hs = lax.scan(step, init, (chunks(q), chunks(k), chunks(v), chunks(log_i), chunks(log_f)))
    return jnp.moveaxis(hs.swapaxes(2, 3), 0, 1).reshape(bsz, seq, nh, dv)


def mlstm_mixer(qk, v, o, i_pre, f_pre, conv_w, conv_b, i_bias, f_bias, norm_w):
    bsz, seq, _ = v.shape
    qk = jax.nn.silu(causal_dwconv(qk, conv_w, conv_b)).astype(jnp.float32)
    q, k = jnp.split(qk, 2, axis=-1)
    q = q.reshape(bsz, seq, MLSTM_N_HEADS, MLSTM_HEAD_DQK) * (MLSTM_HEAD_DQK ** -0.5)
    k = k.reshape(bsz, seq, MLSTM_N_HEADS, MLSTM_HEAD_DQK)
    vh = v.astype(jnp.float32).reshape(bsz, seq, MLSTM_N_HEADS, MLSTM_HEAD_DV)
    log_i = i_pre.astype(jnp.float32) + i_bias
    log_f = jax.nn.log_sigmoid(f_pre.astype(jnp.float32) + f_bias)
    h = mlstm_chunked(q, k, vh, log_i, log_f).reshape(bsz, seq, MLSTM_DV)
    h = group_rms_norm(h, MLSTM_N_HEADS, MLSTM_EPS) * norm_w
    return (jax.nn.sigmoid(o.astype(jnp.float32)) * h).astype(v.dtype)


def hybrid_mixer(h, w_in, ssd_conv_w, ssd_conv_b, ssd_dt_bias, ssd_a_log, ssd_d, ssd_norm_w,
                 rwkv_mu, rwkv_w0, rwkv_w2, rwkv_a0, rwkv_a2, rwkv_g2, rwkv_k_k, rwkv_k_a,
                 rwkv_r_k, rwkv_ln_w, rwkv_ln_b, mlstm_conv_w, mlstm_conv_b, mlstm_i_bias,
                 mlstm_f_bias, mlstm_norm_w, proj_ssd, proj_rwkv, proj_mlstm, w_out):
    proj = h @ w_in
    (ssd_z, ssd_xbc, ssd_dt, rwkv_p, ml_qk, ml_v, ml_o, ml_i, ml_f,
     gate_pre) = jnp.split(proj, IN_SPLITS, axis=-1)
    y_ssd = ssd_mixer(ssd_z, ssd_xbc, ssd_dt, ssd_conv_w, ssd_conv_b, ssd_dt_bias, ssd_a_log,
                      ssd_d, ssd_norm_w)
    y_rwkv = rwkv7_mixer(rwkv_p, rwkv_mu, rwkv_w0, rwkv_w2, rwkv_a0, rwkv_a2, rwkv_g2,
                         rwkv_k_k, rwkv_k_a, rwkv_r_k, rwkv_ln_w, rwkv_ln_b)
    y_mlstm = mlstm_mixer(ml_qk, ml_v, ml_o, ml_i, ml_f, mlstm_conv_w, mlstm_conv_b,
                          mlstm_i_bias, mlstm_f_bias, mlstm_norm_w)
    g_ssd, g_rwkv, g_mlstm = jnp.split(jax.nn.sigmoid(gate_pre), N_BRANCHES, axis=-1)
    merged = (g_ssd * (y_ssd @ proj_ssd) + g_rwkv * (y_rwkv @ proj_rwkv)
              + g_mlstm * (y_mlstm @ proj_mlstm))
    return merged @ w_out


def conv_ffn(h, w_up, conv_w, conv_b, w_down):
    hcat = causal_dwconv(h @ w_up, conv_w, conv_b)
    gate, up = jnp.split(hcat, 2, axis=-1)
    return (jax.nn.gelu(gate, approximate=False) * up) @ w_down


def setup_inputs(seed: int = 0) -> dict:
    key = jax.random.key(seed)
    ks = iter(jax.random.split(key, 48))

    def nrm(shape, scale):
        return jax.random.normal(next(ks), shape, jnp.float32) * scale

    def unif(shape, lo, hi):
        return jax.random.uniform(next(ks), shape, jnp.float32, minval=lo, maxval=hi)

    L = DEPTH
    dt = jnp.exp(unif((L, SSD_N_HEADS), math.log(1e-3), math.log(1e-1)))
    return {
        "x": nrm((BATCH, SEQ, D_MODEL), 1.0),
        "ln_in_g": 1.0 + nrm((D_MODEL,), 0.02),
        "ln_in_b": nrm((D_MODEL,), 0.02),
        "w_in": nrm((L, D_MODEL, D_IN), D_MODEL ** -0.5),
        "ssd_conv_w": nrm((L, SSD_CONV, SSD_XBC), SSD_CONV ** -0.5),
        "ssd_conv_b": nrm((L, SSD_XBC), 0.02),
        "ssd_dt_bias": dt + jnp.log(-jnp.expm1(-dt)),
        "ssd_a_log": jnp.log(unif((L, SSD_N_HEADS), 1.0, 16.0)),
        "ssd_d": 1.0 + nrm((L, SSD_N_HEADS), 0.1),
        "ssd_norm_w": 1.0 + nrm((L, SSD_D_INNER), 0.02),
        "rwkv_mu": unif((L, RWKV_COLS), 0.0, 1.0),
        "rwkv_w0": unif((L, RWKV_D), -6.0, 1.0),
        "rwkv_w2": nrm((L, RWKV_DECAY_LORA, RWKV_D), 0.5 * RWKV_DECAY_LORA ** -0.5),
        "rwkv_a0": nrm((L, RWKV_D), 0.1),
        "rwkv_a2": nrm((L, RWKV_A_LORA, RWKV_D), RWKV_A_LORA ** -0.5),
        "rwkv_g2": nrm((L, RWKV_GATE_LORA, RWKV_D), RWKV_GATE_LORA ** -0.5),
        "rwkv_k_k": 0.85 + nrm((L, RWKV_D), 0.05),
        "rwkv_k_a": 1.0 + nrm((L, RWKV_D), 0.05),
        "rwkv_r_k": nrm((L, RWKV_N_HEADS, RWKV_HEAD_DIM), 0.1),
        "rwkv_ln_w": 1.0 + nrm((L, RWKV_D), 0.02),
        "rwkv_ln_b": nrm((L, RWKV_D), 0.02),
        "mlstm_conv_w": nrm((L, MLSTM_CONV, 2 * MLSTM_DQK), MLSTM_CONV ** -0.5),
        "mlstm_conv_b": nrm((L, 2 * MLSTM_DQK), 0.02),
        "mlstm_i_bias": nrm((L, MLSTM_N_HEADS), 0.1),
        "mlstm_f_bias": unif((L, MLSTM_N_HEADS), 3.0, 6.0),
        "mlstm_norm_w": 1.0 + nrm((L, MLSTM_DV), 0.02),
        "proj_ssd": nrm((L, SSD_D_INNER, D_MODEL), SSD_D_INNER ** -0.5),
        "proj_rwkv": nrm((L, RWKV_D, D_MODEL), RWKV_D ** -0.5),
        "proj_mlstm": nrm((L, MLSTM_DV, D_MODEL), MLSTM_DV ** -0.5),
        "w_out": nrm((L, D_MODEL, D_MODEL), DEEPNORM_BETA * D_MODEL ** -0.5),
        "ln1_g": 1.0 + nrm((L, D_MODEL), 0.02),
        "ln1_b": nrm((L, D_MODEL), 0.02),
        "ffn_w_up": nrm((L, D_MODEL, 2 * FFN_D), D_MODEL ** -0.5),
        "ffn_conv_w": nrm((L, FFN_CONV, 2 * FFN_D), FFN_CONV ** -0.5),
        "ffn_conv_b": nrm((L, 2 * FFN_D), 0.02),
        "ffn_w_down": nrm((L, FFN_D, D_MODEL), DEEPNORM_BETA * FFN_D ** -0.5),
        "ln2_g": 1.0 + nrm((L, D_MODEL), 0.02),
        "ln2_b": nrm((L, D_MODEL), 0.02),
    }


def reference(x, ln_in_g, ln_in_b, w_in, ssd_conv_w, ssd_conv_b, ssd_dt_bias, ssd_a_log, ssd_d,
              ssd_norm_w, rwkv_mu, rwkv_w0, rwkv_w2, rwkv_a0, rwkv_a2, rwkv_g2, rwkv_k_k,
              rwkv_k_a, rwkv_r_k, rwkv_ln_w, rwkv_ln_b, mlstm_conv_w, mlstm_conv_b, mlstm_i_bias,
              mlstm_f_bias, mlstm_norm_w, proj_ssd, proj_rwkv, proj_mlstm, w_out, ln1_g, ln1_b,
              ffn_w_up, ffn_conv_w, ffn_conv_b, ffn_w_down, ln2_g, ln2_b):
    h = layer_norm(x, ln_in_g, ln_in_b)
    for l in range(DEPTH):
        mix = hybrid_mixer(h, w_in[l], ssd_conv_w[l], ssd_conv_b[l], ssd_dt_bias[l], ssd_a_log[l],
                           ssd_d[l], ssd_norm_w[l], rwkv_mu[l], rwkv_w0[l], rwkv_w2[l], rwkv_a0[l],
                           rwkv_a2[l], rwkv_g2[l], rwkv_k_k[l], rwkv_k_a[l], rwkv_r_k[l],
                           rwkv_ln_w[l], rwkv_ln_b[l], mlstm_conv_w[l], mlstm_conv_b[l],
                           mlstm_i_bias[l], mlstm_f_bias[l], mlstm_norm_w[l], proj_ssd[l],
                           proj_rwkv[l], proj_mlstm[l], w_out[l])
        h = layer_norm(DEEPNORM_ALPHA * h + mix, ln1_g[l], ln1_b[l])
        ffn = conv_ffn(h, ffn_w_up[l], ffn_conv_w[l], ffn_conv_b[l], ffn_w_down[l])
        h = layer_norm(DEEPNORM_ALPHA * h + ffn, ln2_g[l], ln2_b[l])
    return h
```

```python
import functools
import math

import jax
import jax.numpy as jnp
from jax import lax
from jax.experimental import pallas as pl
from jax.experimental.pallas import tpu as pltpu

F32 = jnp.float32
BF16 = jnp.bfloat16

D_MODEL = 2048
DEPTH = 2
ALPHA = (2.0 * DEPTH) ** 0.25
LN_EPS = 1e-5

SSD_HEADS = 32
SSD_HEAD_DIM = 64
SSD_GROUPS = 4
SSD_STATE = 128
SSD_GN = SSD_GROUPS * SSD_STATE
SSD_EPS = 1e-5
SSD_GROUP_W = D_MODEL // SSD_GROUPS
SSD_HPG = SSD_HEADS // SSD_GROUPS

RWKV_HEAD_DIM = 64
RWKV_PAIRS = D_MODEL // 128
RWKV_LORA_W = 96
RWKV_LORA_A = 96
RWKV_LORA_G = 256
RWKV_LN_EPS = 64e-5
RWKV_CHUNK = 64
RWKV_SUB = 16

ML_HEADS = 8
ML_DK = 128
ML_DV = 256
ML_EPS = 1e-6

FFN_D = 5504
FFN_PAD = 5632
FFN_TN = 512

LANES = 128
HALO = 8
TIME_BLOCK = 128

C_Z = 0
C_X = 2048
C_R = 4096
C_K = 6144
C_V = 8192
C_QK = 10240
C_MV = 12288
C_MO = 14336
C_GS = 16384
C_GR = 18432
C_GM = 20480
C_BC = 22528
C_LORA = 23552
C_SMALL = 24064
N_PROJ = 24576

VMEM_LIMIT = 56 * 1024 * 1024


def _dot(a, b):
    return jnp.dot(a, b, preferred_element_type=F32)


def _dot_nt(a, b):
    return lax.dot_general(a, b, (((1,), (1,)), ((), ())), preferred_element_type=F32)


def _dot_tn(a, b):
    return lax.dot_general(a, b, (((0,), (0,)), ((), ())), preferred_element_type=F32)


def _split(x, n):
    parts, r = [], x
    for i in range(n):
        p = r.astype(BF16)
        parts.append(p)
        if i + 1 < n:
            r = r - p.astype(F32)
    return parts


def _dotx_l(x, w_exact, n, dot=_dot):
    acc = None
    for p in _split(x, n):
        t = dot(p, w_exact)
        acc = t if acc is None else acc + t
    return acc


def _dotx_r(w_exact, x, n, dot=_dot):
    acc = None
    for p in _split(x, n):
        t = dot(w_exact, p)
        acc = t if acc is None else acc + t
    return acc


def _mm(a, b, passes, dot=_dot):
    if passes == 1:
        return dot(a.astype(BF16), b.astype(BF16))
    ah, al = _split(a, 2)
    bh, bl = _split(b, 2)
    return dot(ah, bh) + (dot(ah, bl) + dot(al, bh))


def _layer_norm_rows(x, g, b):
    mu = jnp.mean(x, axis=-1, keepdims=True)
    xc = x - mu
    var = jnp.mean(xc * xc, axis=-1, keepdims=True)
    return xc * lax.rsqrt(var + LN_EPS) * g + b


def _cparams(n_axes):
    return pltpu.CompilerParams(dimension_semantics=("arbitrary",) * n_axes,
                                vmem_limit_bytes=VMEM_LIMIT)


def _row2(v):
    return v.reshape(1, -1)


def _ln_in_body(x_ref, g_ref, b_ref, h_ref, hb_ref):
    h = _layer_norm_rows(x_ref[...], g_ref[...], b_ref[...])
    h_ref[...] = h
    hb_ref[...] = h.astype(BF16)


def _ln_in(x2d, g, b, tm=512):
    n, d = x2d.shape
    tm = min(tm, n)
    return pl.pallas_call(
        _ln_in_body,
        grid=(n // tm,),
        in_specs=[pl.BlockSpec((tm, d), lambda i: (i, 0)),
                  pl.BlockSpec((1, d), lambda i: (0, 0)),
                  pl.BlockSpec((1, d), lambda i: (0, 0))],
        out_specs=[pl.BlockSpec((tm, d), lambda i: (i, 0)),
                   pl.BlockSpec((tm, d), lambda i: (i, 0))],
        out_shape=[jax.ShapeDtypeStruct((n, d), F32), jax.ShapeDtypeStruct((n, d), BF16)],
        compiler_params=_cparams(1),
        name="ln_in",
    )(x2d, _row2(g), _row2(b))


def _mm_body(a_ref, w_ref, o_ref):
    o_ref[...] = _dot(a_ref[...], w_ref[...]).astype(o_ref.dtype)


def _matmul(a, w, tm, tn, out_dtype, name):
    m, k = a.shape
    _, n = w.shape
    return pl.pallas_call(
        _mm_body,
        grid=(n // tn, m // tm),
        in_specs=[pl.BlockSpec((tm, k), lambda j, i: (i, 0)),
                  pl.BlockSpec((k, tn), lambda j, i: (0, j))],
        out_specs=pl.BlockSpec((tm, tn), lambda j, i: (i, j)),
        out_shape=jax.ShapeDtypeStruct((m, n), out_dtype),
        compiler_params=_cparams(2),
        name=name,
    )(a, w)


def _conv4_silu(buf, src_ref, w_ref, b_ref, rows):
    buf[HALO:HALO + rows, :] = src_ref[...]
    acc = b_ref[...] + w_ref[3:4, :] * buf[HALO:HALO + rows, :]
    for k in range(3):
        acc = acc + w_ref[k:k + 1, :] * buf[HALO - 3 + k:HALO - 3 + k + rows, :]
    buf[0:HALO, :] = buf[rows:rows + HALO, :]
    return jax.nn.silu(acc)


def _ssd_body(z_ref, x_ref, b_ref, c_ref, sm_ref, sel_ref, cwx_ref, cbx_ref, cwb_ref, cbb_ref,
              cwc_ref, cbc_ref, dtb_ref, alog_ref, dsk_ref, nw_ref, e_ref, tri_ref, o_ref,
              xbuf, bbuf, cbuf, st_ref):
    rows = x_ref.shape[0]

    @pl.when(pl.program_id(2) == 0)
    def _():
        xbuf[0:HALO, :] = jnp.zeros((HALO, xbuf.shape[1]), F32)
        bbuf[0:HALO, :] = jnp.zeros((HALO, bbuf.shape[1]), F32)
        cbuf[0:HALO, :] = jnp.zeros((HALO, cbuf.shape[1]), F32)
        st_ref[...] = jnp.zeros(st_ref.shape, F32)

    xs = _conv4_silu(xbuf, x_ref, cwx_ref, cbx_ref, rows)
    bm = _conv4_silu(bbuf, b_ref, cwb_ref, cbb_ref, rows)
    cm = _conv4_silu(cbuf, c_ref, cwc_ref, cbc_ref, rows)

    dt_raw = _dotx_l(sm_ref[...], sel_ref[...], 3)
    dt = jax.nn.softplus(dt_raw + dtb_ref[...])
    adt = dt * (-jnp.exp(alog_ref[...]))
    a_cs = _dotx_r(tri_ref[...], adt, 3)
    a_cs_t = a_cs.T
    expand = e_ref[...]
    a_exp = _dotx_l(a_cs, expand, 3)
    dt_exp = _dotx_l(dt, expand, 3)
    xdt = xs * dt_exp

    t_row = lax.broadcasted_iota(jnp.int32, (rows, rows), 0)
    t_col = lax.broadcasted_iota(jnp.int32, (rows, rows), 1)
    causal = t_row >= t_col
    lane = lax.broadcasted_iota(jnp.int32, (rows, LANES), 1)
    scores = _dot_nt(cm.astype(BF16), bm.astype(BF16))

    parts = []
    for pr in range(SSD_HPG // 2):
        xp = xdt[:, pr * LANES:(pr + 1) * LANES]
        acc = None
        for s in range(2):
            hh = 2 * pr + s
            seg = a_cs[:, hh:hh + 1] - a_cs_t[hh:hh + 1, :]
            m = jnp.where(causal, jnp.exp(seg), 0.0) * scores
            head_lanes = (lane < SSD_HEAD_DIM) if s == 0 else (lane >= SSD_HEAD_DIM)
            xm = jnp.where(head_lanes, xp, 0.0)
            t = _dot(m.astype(BF16), xm.astype(BF16))
            acc = t if acc is None else acc + t
        parts.append(acc)
    y = jnp.concatenate(parts, axis=1)

    st = st_ref[...]
    y = y + _dot(cm.astype(BF16), st.astype(BF16)) * jnp.exp(a_exp)
    a_last = a_exp[rows - 1:rows, :]
    st_ref[...] = jnp.exp(a_last) * st + _dot_tn(
        bm.astype(BF16), (xdt * jnp.exp(a_last - a_exp)).astype(BF16))

    y = y + xs * dsk_ref[...]
    y = y * jax.nn.silu(z_ref[...])
    ms = jnp.mean(y * y, axis=-1, keepdims=True)
    o_ref[...] = (y * lax.rsqrt(ms + SSD_EPS) * nw_ref[...]).astype(o_ref.dtype)


def _ssd_mixer(proj, bsz, seq, conv_w, conv_b, dt_bias, a_log, d_skip, norm_w, tri):
    tb = TIME_BLOCK
    nc = seq // tb
    gw = SSD_GROUP_W
    cwx, cwb, cwc = conv_w[:, :D_MODEL], conv_w[:, D_MODEL:D_MODEL + SSD_GN], conv_w[:, D_MODEL + SSD_GN:]
    cbx = _row2(conv_b[:D_MODEL])
    cbb = _row2(conv_b[D_MODEL:D_MODEL + SSD_GN])
    cbc = _row2(conv_b[D_MODEL + SSD_GN:])

    def per_group_lanes(v):
        v = v.reshape(SSD_GROUPS, 1, SSD_HPG)
        return jnp.pad(v, ((0, 0), (0, 0), (0, LANES - SSD_HPG)))

    src = jnp.arange(LANES)[None, :, None]
    dst = jnp.arange(LANES)[None, None, :]
    grp = jnp.arange(SSD_GROUPS)[:, None, None]
    sel = ((src == grp * SSD_HPG + dst) & (dst < SSD_HPG)).astype(BF16)
    expand = ((jnp.arange(gw)[None, :] // SSD_HEAD_DIM == jnp.arange(LANES)[:, None])).astype(BF16)

    row = lambda b, g, c: b * nc + c
    in_specs = [
        pl.BlockSpec((tb, gw), lambda b, g, c: (row(b, g, c), C_Z // gw + g)),
        pl.BlockSpec((tb, gw), lambda b, g, c: (row(b, g, c), C_X // gw + g)),
        pl.BlockSpec((tb, LANES), lambda b, g, c: (row(b, g, c), C_BC // LANES + g)),
        pl.BlockSpec((tb, LANES), lambda b, g, c: (row(b, g, c), C_BC // LANES + SSD_GROUPS + g)),
        pl.BlockSpec((tb, LANES), lambda b, g, c: (row(b, g, c), C_SMALL // LANES)),
        pl.BlockSpec((None, LANES, LANES), lambda b, g, c: (g, 0, 0)),
        pl.BlockSpec((4, gw), lambda b, g, c: (0, g)),
        pl.BlockSpec((1, gw), lambda b, g, c: (0, g)),
        pl.BlockSpec((4, LANES), lambda b, g, c: (0, g)),
        pl.BlockSpec((1, LANES), lambda b, g, c: (0, g)),
        pl.BlockSpec((4, LANES), lambda b, g, c: (0, g)),
        pl.BlockSpec((1, LANES), lambda b, g, c: (0, g)),
        pl.BlockSpec((None, 1, LANES), lambda b, g, c: (g, 0, 0)),
        pl.BlockSpec((None, 1, LANES), lambda b, g, c: (g, 0, 0)),
        pl.BlockSpec((1, gw), lambda b, g, c: (0, g)),
        pl.BlockSpec((1, gw), lambda b, g, c: (0, g)),
        pl.BlockSpec((LANES, gw), lambda b, g, c: (0, 0)),
        pl.BlockSpec((tb, tb), lambda b, g, c: (0, 0)),
    ]
    return pl.pallas_call(
        _ssd_body,
        grid=(bsz, SSD_GROUPS, nc),
        in_specs=in_specs,
        out_specs=pl.BlockSpec((tb, gw), lambda b, g, c: (row(b, g, c), g)),
        out_shape=jax.ShapeDtypeStruct((bsz * seq, D_MODEL), BF16),
        scratch_shapes=[pltpu.VMEM((tb + HALO, gw), F32),
                        pltpu.VMEM((tb + HALO, LANES), F32),
                        pltpu.VMEM((tb + HALO, LANES), F32),
                        pltpu.VMEM((SSD_STATE, gw), F32)],
        compiler_params=_cparams(3),
        name="ssd_mixer",
    )(proj, proj, proj, proj, proj, sel, cwx, cbx, cwb, cbb, cwc, cbc,
      per_group_lanes(dt_bias), per_group_lanes(a_log),
      _row2(jnp.repeat(d_skip, SSD_HEAD_DIM)), _row2(norm_w), expand, tri)


def _mlstm_body(q_ref, k_ref, v_ref, og_ref, sm_ref, sel_ref, cwq_ref, cbq_ref, cwk_ref, cbk_ref,
                gb_ref, nw_ref, tri_ref, o_ref, qbuf, kbuf, ct_ref, n_ref, m_ref):
    rows = q_ref.shape[0]

    @pl.when(pl.program_id(2) == 0)
    def _():
        qbuf[0:HALO, :] = jnp.zeros((HALO, qbuf.shape[1]), F32)
        kbuf[0:HALO, :] = jnp.zeros((HALO, kbuf.shape[1]), F32)
        ct_ref[...] = jnp.zeros(ct_ref.shape, F32)
        n_ref[...] = jnp.zeros(n_ref.shape, F32)
        m_ref[...] = jnp.zeros(m_ref.shape, F32)

    q = _conv4_silu(qbuf, q_ref, cwq_ref, cbq_ref, rows) * (ML_DK ** -0.5)
    k = _conv4_silu(kbuf, k_ref, cwk_ref, cbk_ref, rows)
    v = v_ref[...]

    pre = _dotx_l(sm_ref[...], sel_ref[...], 3) + gb_ref[...]
    log_f = jax.nn.log_sigmoid(pre)
    bcum_all = _dotx_r(tri_ref[...], log_f, 3)
    lane = lax.broadcasted_iota(jnp.int32, (rows, LANES), 1)
    gates_t = jnp.where(lane == 0, pre, bcum_all).T
    li = pre[:, 0:1]
    bcum = bcum_all[:, 1:2]
    li_row = gates_t[0:1, :]
    bcum_row = gates_t[1:2, :]

    t_row = lax.broadcasted_iota(jnp.int32, (rows, rows), 0)
    t_col = lax.broadcasted_iota(jnp.int32, (rows, rows), 1)
    log_d = jnp.where(t_row >= t_col, bcum - bcum_row + li_row, -jnp.inf)
    m_prev = m_ref[0:1, 0:1]
    m_inter = bcum + m_prev
    m_t = jnp.maximum(jnp.max(log_d, axis=-1, keepdims=True), m_inter)
    w = jnp.exp(log_d - m_t) * _dot_nt(q.astype(BF16), k.astype(BF16))
    s_inter = jnp.exp(m_inter - m_t)
    ct = ct_ref[...]
    nvec = n_ref[0:1, :]
    num = _dot(w.astype(BF16), v.astype(BF16)) + s_inter * _dot(q.astype(BF16), ct.astype(BF16))
    den = (jnp.sum(w, axis=-1, keepdims=True)
           + s_inter * jnp.sum(q * nvec, axis=-1, keepdims=True))
    h = num / jnp.maximum(jnp.abs(den), jnp.exp(-m_t))

    b_tot = bcum[rows - 1:rows, :]
    log_w = b_tot - bcum + li
    m_new = jnp.maximum(b_tot + m_prev, jnp.max(log_w, axis=0, keepdims=True))
    w_end = jnp.exp(log_w - m_new)
    s_old = jnp.exp(b_tot + m_prev - m_new)
    ct_ref[...] = s_old * ct + _dot_tn(k.astype(BF16), (w_end * v).astype(BF16))
    n_ref[...] = jnp.broadcast_to(s_old * nvec + jnp.sum(w_end * k, axis=0, keepdims=True),
                                  n_ref.shape)
    m_ref[...] = jnp.broadcast_to(m_new, m_ref.shape)

    hn = h * lax.rsqrt(jnp.mean(h * h, axis=-1, keepdims=True) + ML_EPS) * nw_ref[...]
    o_ref[...] = (jax.nn.sigmoid(og_ref[...]) * hn).astype(o_ref.dtype)


def _mlstm_mixer(proj, bsz, seq, conv_w, conv_b, i_bias, f_bias, norm_w, tri):
    tb = TIME_BLOCK
    nc = seq // tb
    small_i = 32
    small_f = 40
    src = jnp.arange(LANES)[None, :, None]
    dst = jnp.arange(LANES)[None, None, :]
    hd = jnp.arange(ML_HEADS)[:, None, None]
    sel = (((src == small_i + hd) & (dst == 0)) | ((src == small_f + hd) & (dst == 1))).astype(BF16)
    gate_bias = jnp.zeros((ML_HEADS, 1, LANES), F32)
    gate_bias = gate_bias.at[:, 0, 0].set(i_bias).at[:, 0, 1].set(f_bias)
    conv_b2 = _row2(conv_b)
    kq = ML_HEADS
    row = lambda b, h, c: b * nc + c
    in_specs = [
        pl.BlockSpec((tb, ML_DK), lambda b, h, c: (row(b, h, c), C_QK // ML_DK + h)),
        pl.BlockSpec((tb, ML_DK), lambda b, h, c: (row(b, h, c), C_QK // ML_DK + kq + h)),
        pl.BlockSpec((tb, ML_DV), lambda b, h, c: (row(b, h, c), C_MV // ML_DV + h)),
        pl.BlockSpec((tb, ML_DV), lambda b, h, c: (row(b, h, c), C_MO // ML_DV + h)),
        pl.BlockSpec((tb, LANES), lambda b, h, c: (row(b, h, c), C_SMALL // LANES)),
        pl.BlockSpec((None, LANES, LANES), lambda b, h, c: (h, 0, 0)),
        pl.BlockSpec((4, ML_DK), lambda b, h, c: (0, h)),
        pl.BlockSpec((1, ML_DK), lambda b, h, c: (0, h)),
        pl.BlockSpec((4, ML_DK), lambda b, h, c: (0, kq + h)),
        pl.BlockSpec((1, ML_DK), lambda b, h, c: (0, kq + h)),
        pl.BlockSpec((None, 1, LANES), lambda b, h, c: (h, 0, 0)),
        pl.BlockSpec((1, ML_DV), lambda b, h, c: (0, h)),
        pl.BlockSpec((tb, tb), lambda b, h, c: (0, 0)),
    ]
    return pl.pallas_call(
        _mlstm_body,
        grid=(bsz, ML_HEADS, nc),
        in_specs=in_specs,
        out_specs=pl.BlockSpec((tb, ML_DV), lambda b, h, c: (row(b, h, c), h)),
        out_shape=jax.ShapeDtypeStruct((bsz * seq, D_MODEL), BF16),
        scratch_shapes=[pltpu.VMEM((tb + HALO, ML_DK), F32),
                        pltpu.VMEM((tb + HALO, ML_DK), F32),
                        pltpu.VMEM((ML_DK, ML_DV), F32),
                        pltpu.VMEM((HALO, LANES), F32),
                        pltpu.VMEM((HALO, LANES), F32)],
        compiler_params=_cparams(3),
        name="mlstm_mixer",
    )(proj, proj, proj, proj, proj, sel, conv_w, conv_b2, conv_w, conv_b2, gate_bias,
      _row2(norm_w), tri)


def _stack_heads(x):
    lane = lax.broadcasted_iota(jnp.int32, x.shape, 1)
    return jnp.concatenate([jnp.where(lane < RWKV_HEAD_DIM, x, 0.0),
                            jnp.where(lane >= RWKV_HEAD_DIM, x, 0.0)], axis=0)


def _shift_lerp(buf, src_ref, mu_ref, rows):
    buf[HALO:HALO + rows, :] = src_ref[...]
    cur = buf[HALO:HALO + rows, :]
    prev = buf[HALO - 1:HALO - 1 + rows, :]
    buf[0:HALO, :] = buf[rows:rows + HALO, :]
    return cur + (prev - cur) * mu_ref[...]


RWKV_PASSES_SCORES = 3
RWKV_PASSES_SOLVE = 3
RWKV_PASSES_VALUES = 1
RWKV_PASSES_STATE = 3


def _rwkv_body(r_ref, k_ref, v_ref, lo_ref, mur_ref, muk_ref, muv_ref, mulo_ref, w0_ref, w2_ref,
               a0_ref, a2_ref, g2_ref, kk_ref, ka_ref, rk_ref, lnw_ref, lnb_ref, tri_ref,
               ones_ref, o_ref, rbuf, kbuf, vbuf, lobuf, z_ref, y_buf):
    rows = r_ref.shape[0]
    ch = RWKV_CHUNK
    hd = RWKV_HEAD_DIM

    @pl.when(pl.program_id(2) == 0)
    def _():
        for buf in (rbuf, kbuf, vbuf, lobuf):
            buf[0:HALO, :] = jnp.zeros((HALO, buf.shape[1]), F32)
        z_ref[...] = jnp.zeros(z_ref.shape, F32)

    r = _shift_lerp(rbuf, r_ref, mur_ref, rows)
    k = _shift_lerp(kbuf, k_ref, muk_ref, rows)
    v = _shift_lerp(vbuf, v_ref, muv_ref, rows)
    lo = _shift_lerp(lobuf, lo_ref, mulo_ref, rows)
    w_lo, a_lo, g_lo = lo[:, 0:LANES], lo[:, LANES:2 * LANES], lo[:, 2 * LANES:]

    log_w = -jax.nn.softplus(-(w0_ref[...] + _mm(jnp.tanh(w_lo), w2_ref[...], 3))) - 0.5
    lw = -jnp.exp(log_w)
    a = jax.nn.sigmoid(a0_ref[...] + _mm(a_lo, a2_ref[...], 3))
    gate = _mm(jax.nn.sigmoid(g_lo), g2_ref[...], 1)

    head_sum = ones_ref[...]
    kk = k * kk_ref[...]
    kk = kk * lax.rsqrt(jnp.maximum(_dotx_l(kk * kk, head_sum, 2), 1e-24))
    kh = k * (1.0 + (a - 1.0) * ka_ref[...])
    kb = kk * a
    cum = _dotx_r(tri_ref[...], lw, 3)

    c_row = lax.broadcasted_iota(jnp.int32, (ch, LANES), 0)
    c_col = lax.broadcasted_iota(jnp.int32, (ch, LANES), 1) % hd
    strict = c_col < c_row
    incl = c_col <= c_row
    same_sub = (c_col // RWKV_SUB) == (c_row // RWKV_SUB)
    eye_pair = (c_col == c_row).astype(F32)
    s_row = lax.broadcasted_iota(jnp.int32, (LANES, LANES), 0)
    s_col = lax.broadcasted_iota(jnp.int32, (LANES, LANES), 1)
    same_head = (s_row // hd) == (s_col // hd)
    eye_full = (s_row == s_col).astype(F32)

    solve = functools.partial(_mm, passes=RWKV_PASSES_SOLVE)
    vals = functools.partial(_mm, passes=RWKV_PASSES_VALUES)

    for j in range(rows // ch):
        sl = slice(j * ch, (j + 1) * ch)
        cm, lwj, kkj, khj, kbj, rj, vj = cum[sl], lw[sl], kk[sl], kh[sl], kb[sl], r[sl], v[sl]
        cm_end = cm[ch - 1:ch, :]
        e_neg = jnp.exp(-cm)
        e_end = jnp.exp(cm_end - cm)
        a_hat = -kkj * jnp.exp(cm - lwj)
        r_hat = rj * jnp.exp(cm)
        k_til, b_til = khj * e_neg, kbj * e_neg
        k_end, b_end = khj * e_end, kbj * e_end

        big = _mm(jnp.concatenate([a_hat, r_hat], axis=0),
                  jnp.concatenate([_stack_heads(k_til), _stack_heads(b_til)], axis=0),
                  RWKV_PASSES_SCORES, dot=_dot_nt)
        l_ak = jnp.where(strict, big[0:ch, 0:LANES], 0.0)
        l_ab = jnp.where(strict, big[0:ch, LANES:], 0.0)
        m_rk = jnp.where(incl, big[ch:, 0:LANES], 0.0)
        m_rb = jnp.where(incl, big[ch:, LANES:], 0.0)

        l_d = jnp.where(same_sub, l_ab, 0.0)
        l_o = l_ab - l_d
        x = eye_pair + l_d
        p = l_d
        for _ in range(3):
            p = solve(p, _stack_heads(p))
            x = x + solve(x, _stack_heads(p))
        t_d = x
        nn = solve(t_d, _stack_heads(l_o))
        n2 = solve(nn, _stack_heads(nn))
        mx = eye_pair + nn
        mx = mx + solve(mx, _stack_heads(n2))
        t = solve(mx, _stack_heads(t_d))

        lv = vals(l_ak, _stack_heads(vj))
        wu = vals(t, jnp.concatenate([_stack_heads(a_hat), _stack_heads(lv)], axis=1))
        w_a, u_v = wu[:, 0:LANES], wu[:, LANES:]
        mw = vals(m_rb, jnp.concatenate([_stack_heads(w_a), _stack_heads(u_v)], axis=1))
        q_eff = r_hat + mw[:, 0:LANES]
        y_loc = vals(m_rk, _stack_heads(vj)) + mw[:, LANES:]

        z = z_ref[...]
        y_buf[sl, :] = _mm(q_eff, z, RWKV_PASSES_STATE) + y_loc
        a_eff = (jnp.where(same_head, _mm(b_end, w_a, RWKV_PASSES_STATE, dot=_dot_tn), 0.0)
                 + eye_full * jnp.exp(cm_end))
        z_loc = jnp.where(same_head,
                          _mm(jnp.concatenate([k_end, b_end], axis=0),
                              jnp.concatenate([vj, u_v], axis=0), RWKV_PASSES_VALUES, dot=_dot_tn),
                          0.0)
        z_ref[...] = _mm(a_eff, z, RWKV_PASSES_STATE) + z_loc

    y = y_buf[...]
    mean = _dotx_l(y, head_sum, 2) * (1.0 / hd)
    yc = y - mean
    var = _dotx_l(yc * yc, head_sum, 2) * (1.0 / hd)
    yn = yc * lax.rsqrt(var + RWKV_LN_EPS) * lnw_ref[...] + lnb_ref[...]
    bonus = _dotx_l(r * kh * rk_ref[...], head_sum, 2) * v
    o_ref[...] = ((yn + bonus) * gate).astype(o_ref.dtype)


def _rwkv_mixer(proj, bsz, seq, mu, w0, w2, a0, a2, g2, k_k, k_a, r_k, ln_w, ln_b):
    tb = TIME_BLOCK
    nc = seq // tb
    d = D_MODEL
    lora_w = 2 * LANES + RWKV_LORA_G
    mur, muk, muv = _row2(mu[0:d]), _row2(mu[d:2 * d]), _row2(mu[2 * d:3 * d])
    o = 3 * d
    pad_l = lambda x, n: jnp.pad(x, (0, n - x.shape[0]))
    mulo = _row2(jnp.concatenate([pad_l(mu[o:o + RWKV_LORA_W], LANES),
                                  pad_l(mu[o + RWKV_LORA_W:o + RWKV_LORA_W + RWKV_LORA_A], LANES),
                                  mu[o + RWKV_LORA_W + RWKV_LORA_A:]]))
    w2p = jnp.pad(w2, ((0, LANES - RWKV_LORA_W), (0, 0)))
    a2p = jnp.pad(a2, ((0, LANES - RWKV_LORA_A), (0, 0)))
    ch = RWKV_CHUNK
    t_i = jnp.arange(tb)[:, None]
    t_j = jnp.arange(tb)[None, :]
    tri = ((t_i >= t_j) & (t_i // ch == t_j // ch)).astype(BF16)
    l_i = jnp.arange(LANES)[:, None]
    l_j = jnp.arange(LANES)[None, :]
    head_sum = (l_i // RWKV_HEAD_DIM == l_j // RWKV_HEAD_DIM).astype(BF16)

    row = lambda b, p, c: b * nc + c
    col = lambda base: (lambda b, p, c: (row(b, p, c), base // LANES + p))
    vec = pl.BlockSpec((1, LANES), lambda b, p, c: (0, p))
    in_specs = [
        pl.BlockSpec((tb, LANES), col(C_R)),
        pl.BlockSpec((tb, LANES), col(C_K)),
        pl.BlockSpec((tb, LANES), col(C_V)),
        pl.BlockSpec((tb, lora_w), lambda b, p, c: (row(b, p, c), C_LORA // lora_w)),
        vec, vec, vec,
        pl.BlockSpec((1, lora_w), lambda b, p, c: (0, 0)),
        vec,
        pl.BlockSpec((LANES, LANES), lambda b, p, c: (0, p)),
        vec,
        pl.BlockSpec((LANES, LANES), lambda b, p, c: (0, p)),
        pl.BlockSpec((RWKV_LORA_G, LANES), lambda b, p, c: (0, p)),
        vec, vec, vec, vec, vec,
        pl.BlockSpec((tb, tb), lambda b, p, c: (0, 0)),
        pl.BlockSpec((LANES, LANES), lambda b, p, c: (0, 0)),
    ]
    return pl.pallas_call(
        _rwkv_body,
        grid=(bsz, RWKV_PAIRS, nc),
        in_specs=in_specs,
        out_specs=pl.BlockSpec((tb, LANES), lambda b, p, c: (row(b, p, c), p)),
        out_shape=jax.ShapeDtypeStruct((bsz * seq, D_MODEL), BF16),
        scratch_shapes=[pltpu.VMEM((tb + HALO, LANES), F32),
                        pltpu.VMEM((tb + HALO, LANES), F32),
                        pltpu.VMEM((tb + HALO, LANES), F32),
                        pltpu.VMEM((tb + HALO, lora_w), F32),
                        pltpu.VMEM((LANES, LANES), F32),
                        pltpu.VMEM((tb, LANES), F32)],
        compiler_params=_cparams(3),
        name="rwkv_mixer",
    )(proj, proj, proj, proj, mur, muk, muv, mulo, _row2(w0), w2p, _row2(a0), a2p, g2,
      _row2(k_k), _row2(k_a), _row2(r_k.reshape(-1)), _row2(ln_w), _row2(ln_b), tri, head_sum)


def _gated_first_body(y_ref, p_ref, g_ref, o_ref):
    o_ref[...] = (jax.nn.sigmoid(g_ref[...]) * _dot(y_ref[...], p_ref[...])).astype(o_ref.dtype)


def _gated_next_body(y_ref, p_ref, g_ref, acc_ref, o_ref):
    o_ref[...] = (acc_ref[...] + jax.nn.sigmoid(g_ref[...]) * _dot(y_ref[...], p_ref[...])
                  ).astype(o_ref.dtype)


def _gated_proj(y, p, proj, gate_col, acc, out_dtype, tm=512, tn=1024):
    m, k = y.shape
    tm = min(tm, m)
    n = p.shape[1]
    specs = [pl.BlockSpec((tm, k), lambda j, i: (i, 0)),
             pl.BlockSpec((k, tn), lambda j, i: (0, j)),
             pl.BlockSpec((tm, tn), lambda j, i: (i, gate_col // tn + j))]
    args = [y, p, proj]
    body = _gated_first_body
    if acc is not None:
        specs.append(pl.BlockSpec((tm, tn), lambda j, i: (i, j)))
        args.append(acc)
        body = _gated_next_body
    return pl.pallas_call(
        body,
        grid=(n // tn, m // tm),
        in_specs=specs,
        out_specs=pl.BlockSpec((tm, tn), lambda j, i: (i, j)),
        out_shape=jax.ShapeDtypeStruct((m, n), out_dtype),
        compiler_params=_cparams(2),
        name="gated_proj",
    )(*args)


def _mm_res_ln_body(a_ref, w_ref, h_ref, g_ref, b_ref, o_ref, ob_ref, acc_ref):
    kk = pl.program_id(1)

    @pl.when(kk == 0)
    def _():
        acc_ref[...] = jnp.zeros(acc_ref.shape, F32)

    acc_ref[...] += _dot(a_ref[...], w_ref[...])

    @pl.when(kk == pl.num_programs(1) - 1)
    def _():
        h = _layer_norm_rows(ALPHA * h_ref[...] + acc_ref[...], g_ref[...], b_ref[...])
        o_ref[...] = h
        ob_ref[...] = h.astype(BF16)


def _matmul_res_ln(a, w, h, g, b, tm, tk):
    m, k = a.shape
    n = w.shape[1]
    return pl.pallas_call(
        _mm_res_ln_body,
        grid=(m // tm, k // tk),
        in_specs=[pl.BlockSpec((tm, tk), lambda i, kk: (i, kk)),
                  pl.BlockSpec((tk, n), lambda i, kk: (kk, 0)),
                  pl.BlockSpec((tm, n), lambda i, kk: (i, 0)),
                  pl.BlockSpec((1, n), lambda i, kk: (0, 0)),
                  pl.BlockSpec((1, n), lambda i, kk: (0, 0))],
        out_specs=[pl.BlockSpec((tm, n), lambda i, kk: (i, 0)),
                   pl.BlockSpec((tm, n), lambda i, kk: (i, 0))],
        out_shape=[jax.ShapeDtypeStruct((m, n), F32), jax.ShapeDtypeStruct((m, n), BF16)],
        scratch_shapes=[pltpu.VMEM((tm, n), F32)],
        compiler_params=_cparams(2),
        name="matmul_res_ln",
    )(a, w, h, _row2(g), _row2(b))


FFN_HALO = 16


def _ffn_up_body(h_ref, hp_ref, wg_ref, wu_ref, cwg_ref, cbg_ref, cwu_ref, cbu_ref, o_ref,
                 gbuf, ubuf, *, blocks_per_seq):
    rows = h_ref.shape[0]
    first = (pl.program_id(1) % blocks_per_seq) == 0
    keep = jnp.where(first, 0.0, 1.0)

    def branch(buf, w_ref, cw_ref, cb_ref):
        buf[0:FFN_HALO, :] = _dot(hp_ref[...], w_ref[...]) * keep
        buf[FFN_HALO:FFN_HALO + rows, :] = _dot(h_ref[...], w_ref[...])
        acc = cb_ref[...] + cw_ref[2:3, :] * buf[FFN_HALO:FFN_HALO + rows, :]
        acc = acc + cw_ref[1:2, :] * buf[FFN_HALO - 1:FFN_HALO - 1 + rows, :]
        acc = acc + cw_ref[0:1, :] * buf[FFN_HALO - 2:FFN_HALO - 2 + rows, :]
        return acc

    gate = branch(gbuf, wg_ref, cwg_ref, cbg_ref)
    up = branch(ubuf, wu_ref, cwu_ref, cbu_ref)
    gelu = 0.5 * gate * (1.0 + lax.erf(gate * (2.0 ** -0.5)))
    o_ref[...] = (gelu * up).astype(o_ref.dtype)


def _ffn_up(hb, wg, wu, cwg, cbg, cwu, cbu, seq, tm=1024):
    m, k = hb.shape
    n = wg.shape[1]
    tn = FFN_TN
    halo_blocks = tm // FFN_HALO
    body = functools.partial(_ffn_up_body, blocks_per_seq=seq // tm)
    return pl.pallas_call(
        body,
        grid=(n // tn, m // tm),
        in_specs=[pl.BlockSpec((tm, k), lambda j, i: (i, 0)),
                  pl.BlockSpec((FFN_HALO, k), lambda j, i: (jnp.maximum(i * halo_blocks - 1, 0), 0)),
                  pl.BlockSpec((k, tn), lambda j, i: (0, j)),
                  pl.BlockSpec((k, tn), lambda j, i: (0, j)),
                  pl.BlockSpec((3, tn), lambda j, i: (0, j)),
                  pl.BlockSpec((1, tn), lambda j, i: (0, j)),
                  pl.BlockSpec((3, tn), lambda j, i: (0, j)),
                  pl.BlockSpec((1, tn), lambda j, i: (0, j))],
        out_specs=pl.BlockSpec((tm, tn), lambda j, i: (i, j)),
        out_shape=jax.ShapeDtypeStruct((m, n), BF16),
        scratch_shapes=[pltpu.VMEM((tm + FFN_HALO, tn), F32),
                        pltpu.VMEM((tm + FFN_HALO, tn), F32)],
        compiler_params=_cparams(2),
        name="ffn_up",
    )(hb, hb, wg, wu, cwg, cbg, cwu, cbu)


def _pack_w_in(w):
    d = D_MODEL
    o = 0

    def take(n):
        nonlocal o
        s = w[:, o:o + n]
        o += n
        return s

    z = take(d)
    x = take(d)
    bc = take(2 * SSD_GN)
    dt = take(SSD_HEADS)
    r, k, v = take(d), take(d), take(d)
    w_lo, a_lo, g_lo = take(RWKV_LORA_W), take(RWKV_LORA_A), take(RWKV_LORA_G)
    qk, mv, mo = take(d), take(d), take(d)
    gi, gf = take(ML_HEADS), take(ML_HEADS)
    gs, gr, gm = take(d), take(d), take(d)
    zeros = lambda n: jnp.zeros((d, n), w.dtype)
    small = jnp.concatenate([dt, gi, gf, zeros(LANES - SSD_HEADS - 2 * ML_HEADS)], axis=1)
    lora = jnp.concatenate([w_lo, zeros(LANES - RWKV_LORA_W), a_lo, zeros(LANES - RWKV_LORA_A), g_lo],
                           axis=1)
    packed = jnp.concatenate([z, x, r, k, v, qk, mv, mo, gs, gr, gm, bc, lora, small], axis=1)
    packed = jnp.concatenate([packed, zeros(N_PROJ - packed.shape[1])], axis=1)
    return packed.astype(BF16)


def _pad_cols(x, n):
    return jnp.pad(x, ((0, 0), (0, n - x.shape[1])))


def kernel(x, ln_in_g, ln_in_b, w_in, ssd_conv_w, ssd_conv_b, ssd_dt_bias, ssd_a_log, ssd_d, ssd_norm_w, rwkv_mu, rwkv_w0, rwkv_w2, rwkv_a0, rwkv_a2, rwkv_g2, rwkv_k_k, rwkv_k_a, rwkv_r_k, rwkv_ln_w, rwkv_ln_b, mlstm_conv_w, mlstm_conv_b, mlstm_i_bias, mlstm_f_bias, mlstm_norm_w, proj_ssd, proj_rwkv, proj_mlstm, w_out, ln1_g, ln1_b, ffn_w_up, ffn_conv_w, ffn_conv_b, ffn_w_down, ln2_g, ln2_b):
    bsz, seq, d = x.shape
    n_tok = bsz * seq
    tb = TIME_BLOCK
    t_i = jnp.arange(tb)[:, None]
    t_j = jnp.arange(tb)[None, :]
    tri = (t_i >= t_j).astype(BF16)

    h, hb = _ln_in(x.reshape(n_tok, d), ln_in_g, ln_in_b)
    for l in range(DEPTH):
        proj = _matmul(hb, _pack_w_in(w_in[l]), tm=min(1024, n_tok), tn=2048, out_dtype=F32,
                       name="in_proj")
        y_ssd = _ssd_mixer(proj, bsz, seq, ssd_conv_w[l], ssd_conv_b[l], ssd_dt_bias[l],
                           ssd_a_log[l], ssd_d[l], ssd_norm_w[l], tri)
        y_rwkv = _rwkv_mixer(proj, bsz, seq, rwkv_mu[l], rwkv_w0[l], rwkv_w2[l], rwkv_a0[l],
                             rwkv_a2[l], rwkv_g2[l], rwkv_k_k[l], rwkv_k_a[l], rwkv_r_k[l],
                             rwkv_ln_w[l], rwkv_ln_b[l])
        y_mlstm = _mlstm_mixer(proj, bsz, seq, mlstm_conv_w[l], mlstm_conv_b[l], mlstm_i_bias[l],
                               mlstm_f_bias[l], mlstm_norm_w[l], tri)
        merged = _gated_proj(y_ssd, proj_ssd[l].astype(BF16), proj, C_GS, None, F32)
        merged = _gated_proj(y_rwkv, proj_rwkv[l].astype(BF16), proj, C_GR, merged, F32)
        merged = _gated_proj(y_mlstm, proj_mlstm[l].astype(BF16), proj, C_GM, merged, BF16)
        h, hb = _matmul_res_ln(merged, w_out[l].astype(BF16), h, ln1_g[l], ln1_b[l],
                               tm=min(512, n_tok), tk=d)

        wg = _pad_cols(ffn_w_up[l][:, :FFN_D], FFN_PAD).astype(BF16)
        wu = _pad_cols(ffn_w_up[l][:, FFN_D:], FFN_PAD).astype(BF16)
        cwg = _pad_cols(ffn_conv_w[l][:, :FFN_D], FFN_PAD)
        cwu = _pad_cols(ffn_conv_w[l][:, FFN_D:], FFN_PAD)
        cbg = _pad_cols(_row2(ffn_conv_b[l][:FFN_D]), FFN_PAD)
        cbu = _pad_cols(_row2(ffn_conv_b[l][FFN_D:]), FFN_PAD)
        act = _ffn_up(hb, wg, wu, cwg, cbg, cwu, cbu, seq, tm=min(1024, seq))
        wd = jnp.pad(ffn_w_down[l], ((0, FFN_PAD - FFN_D), (0, 0))).astype(BF16)
        h, hb = _matmul_res_ln(act, wd, h, ln2_g[l], ln2_b[l], tm=min(512, n_tok), tk=FFN_TN)
    return h.reshape(bsz, seq, d)
```

```python
import functools
import math

import jax
import jax.numpy as jnp
from jax import lax
from jax.experimental import pallas as pl
from jax.experimental.pallas import tpu as pltpu

F32 = jnp.float32
BF16 = jnp.bfloat16

D_MODEL = 2048
DEPTH = 2
ALPHA = (2.0 * DEPTH) ** 0.25
LN_EPS = 1e-5

SSD_HEADS = 32
SSD_HEAD_DIM = 64
SSD_GROUPS = 4
SSD_STATE = 128
SSD_GN = SSD_GROUPS * SSD_STATE
SSD_EPS = 1e-5
SSD_GROUP_W = D_MODEL // SSD_GROUPS
SSD_HPG = SSD_HEADS // SSD_GROUPS

RWKV_HEAD_DIM = 64
RWKV_PAIRS = D_MODEL // 128
RWKV_LORA_W = 96
RWKV_LORA_A = 96
RWKV_LORA_G = 256
RWKV_LN_EPS = 64e-5
RWKV_CHUNK = 64
RWKV_SUB = 16

ML_HEADS = 8
ML_DK = 128
ML_DV = 256
ML_EPS = 1e-6
ML_GROUP = 8

FFN_D = 5504
FFN_PAD = 5632
FFN_TN = 512

LANES = 128
HALO = 8
TIME_BLOCK = 128

C_Z = 0
C_X = 2048
C_R = 4096
C_K = 6144
C_V = 8192
C_QK = 10240
C_MV = 12288
C_MO = 14336
C_GS = 16384
C_GR = 18432
C_GM = 20480
C_BC = 22528
C_LORA = 23552
C_SMALL = 24064
N_PROJ = 24576

VMEM_LIMIT = 56 * 1024 * 1024


def _dot(a, b):
    return jnp.dot(a, b, preferred_element_type=F32)


def _dot_nt(a, b):
    return lax.dot_general(a, b, (((1,), (1,)), ((), ())), preferred_element_type=F32)


def _dot_tn(a, b):
    return lax.dot_general(a, b, (((0,), (0,)), ((), ())), preferred_element_type=F32)


def _split(x, n):
    parts, r = [], x
    for i in range(n):
        p = r.astype(BF16)
        parts.append(p)
        if i + 1 < n:
            r = r - p.astype(F32)
    return parts


def _dotx_l(x, w_exact, n, dot=_dot):
    acc = None
    for p in _split(x, n):
        t = dot(p, w_exact)
        acc = t if acc is None else acc + t
    return acc


def _dotx_r(w_exact, x, n, dot=_dot):
    acc = None
    for p in _split(x, n):
        t = dot(w_exact, p)
        acc = t if acc is None else acc + t
    return acc


def _mm(a, b, passes, dot=_dot):
    if passes == 1:
        return dot(a.astype(BF16), b.astype(BF16))
    ah, al = _split(a, 2)
    bh, bl = _split(b, 2)
    return dot(ah, bh) + (dot(ah, bl) + dot(al, bh))


def _layer_norm_rows(x, g, b):
    mu = jnp.mean(x, axis=-1, keepdims=True)
    xc = x - mu
    var = jnp.mean(xc * xc, axis=-1, keepdims=True)
    return xc * lax.rsqrt(var + LN_EPS) * g + b


def _cparams(n_axes):
    return pltpu.CompilerParams(dimension_semantics=("arbitrary",) * n_axes,
                                vmem_limit_bytes=VMEM_LIMIT)


def _row2(v):
    return v.reshape(1, -1)


def _ln_in_body(x_ref, g_ref, b_ref, h_ref, hb_ref):
    h = _layer_norm_rows(x_ref[...], g_ref[...], b_ref[...])
    h_ref[...] = h
    hb_ref[...] = h.astype(BF16)


def _ln_in(x2d, g, b, tm=512):
    n, d = x2d.shape
    tm = min(tm, n)
    return pl.pallas_call(
        _ln_in_body,
        grid=(n // tm,),
        in_specs=[pl.BlockSpec((tm, d), lambda i: (i, 0)),
                  pl.BlockSpec((1, d), lambda i: (0, 0)),
                  pl.BlockSpec((1, d), lambda i: (0, 0))],
        out_specs=[pl.BlockSpec((tm, d), lambda i: (i, 0)),
                   pl.BlockSpec((tm, d), lambda i: (i, 0))],
        out_shape=[jax.ShapeDtypeStruct((n, d), F32), jax.ShapeDtypeStruct((n, d), BF16)],
        compiler_params=_cparams(1),
        name="ln_in",
    )(x2d, _row2(g), _row2(b))


def _mm_body(a_ref, w_ref, o_ref):
    o_ref[...] = _dot(a_ref[...], w_ref[...]).astype(o_ref.dtype)


def _matmul(a, w, tm, tn, out_dtype, name):
    m, k = a.shape
    _, n = w.shape
    return pl.pallas_call(
        _mm_body,
        grid=(n // tn, m // tm),
        in_specs=[pl.BlockSpec((tm, k), lambda j, i: (i, 0)),
                  pl.BlockSpec((k, tn), lambda j, i: (0, j))],
        out_specs=pl.BlockSpec((tm, tn), lambda j, i: (i, j)),
        out_shape=jax.ShapeDtypeStruct((m, n), out_dtype),
        compiler_params=_cparams(2),
        name=name,
    )(a, w)


CONV_HALO = 16


def _conv_shift_matrix(rows):
    t = jnp.arange(rows)[:, None]
    c = jnp.arange(3 * (rows + CONV_HALO))[None, :]
    k, j = c // (rows + CONV_HALO), c % (rows + CONV_HALO)
    return (j == CONV_HALO + t - 3 + k).astype(BF16)


def _conv4_silu(buf, src_ref, w_ref, b_ref, shift_ref, rows):
    x = src_ref[...]
    buf[CONV_HALO:CONV_HALO + rows, :] = x
    hist = buf[...]
    taps = jnp.concatenate([(hist * w_ref[k:k + 1, :]).astype(BF16) for k in range(3)], axis=0)
    acc = b_ref[...] + w_ref[3:4, :] * x + _dot(shift_ref[...], taps)
    buf[0:CONV_HALO, :] = buf[rows:rows + CONV_HALO, :]
    return jax.nn.silu(acc)


def _ssd_body(z_ref, x_ref, bc_ref, sm_ref, cwx_ref, cbx_ref, cwbc_ref, cbbc_ref, dtb_ref,
              alog_ref, dsk_ref, nw_ref, e_ref, tri_ref, shift_ref, o_ref, xbuf, bcbuf, st_ref):
    rows = x_ref.shape[0]
    gw = SSD_GROUP_W
    groups = range(SSD_GROUPS)

    @pl.when(pl.program_id(1) == 0)
    def _():
        xbuf[0:CONV_HALO, :] = jnp.zeros((CONV_HALO, xbuf.shape[1]), F32)
        bcbuf[0:CONV_HALO, :] = jnp.zeros((CONV_HALO, bcbuf.shape[1]), F32)
        st_ref[...] = jnp.zeros(st_ref.shape, F32)

    xs_all = _conv4_silu(xbuf, x_ref, cwx_ref, cbx_ref, shift_ref, rows)
    bc = _conv4_silu(bcbuf, bc_ref, cwbc_ref, cbbc_ref, shift_ref, rows)

    dt = jax.nn.softplus(sm_ref[...] + dtb_ref[...])
    adt = dt * (-jnp.exp(alog_ref[...]))
    a_cs = _dotx_r(tri_ref[...], adt, 3)
    a_cs_t = a_cs.T

    t_row = lax.broadcasted_iota(jnp.int32, (rows, rows), 0)
    t_col = lax.broadcasted_iota(jnp.int32, (rows, rows), 1)
    causal = t_row >= t_col
    lane = lax.broadcasted_iota(jnp.int32, (rows, LANES), 1)

    xs = [xs_all[:, g * gw:(g + 1) * gw] for g in groups]
    bm = [bc[:, g * SSD_STATE:(g + 1) * SSD_STATE].astype(BF16) for g in groups]
    cm = [bc[:, SSD_GN + g * SSD_STATE:SSD_GN + (g + 1) * SSD_STATE].astype(BF16) for g in groups]
    a_exp = [_dotx_l(a_cs, e_ref[g], 3) for g in groups]
    dt_exp = [_dotx_l(dt, e_ref[g], 3) for g in groups]
    scores = [_dot_nt(cm[g], bm[g]) for g in groups]
    st = [st_ref[g] for g in groups]
    y_off = [_dot(cm[g], st[g].astype(BF16)) for g in groups]
    xdt = [xs[g] * dt_exp[g] for g in groups]

    y_diag = []
    for g in groups:
        parts = []
        for pr in range(SSD_HPG // 2):
            xp = xdt[g][:, pr * LANES:(pr + 1) * LANES]
            acc = None
            for s in range(2):
                hh = g * SSD_HPG + 2 * pr + s
                seg = a_cs[:, hh:hh + 1] - a_cs_t[hh:hh + 1, :]
                m = jnp.where(causal, jnp.exp(seg), 0.0) * scores[g]
                head_lanes = (lane < SSD_HEAD_DIM) if s == 0 else (lane >= SSD_HEAD_DIM)
                xm = jnp.where(head_lanes, xp, 0.0)
                t = _dot(m.astype(BF16), xm.astype(BF16))
                acc = t if acc is None else acc + t
            parts.append(acc)
        y_diag.append(jnp.concatenate(parts, axis=1))

    for g in groups:
        cols = slice(g * gw, (g + 1) * gw)
        a_last = a_exp[g][rows - 1:rows, :]
        st_ref[g] = jnp.exp(a_last) * st[g] + _dot_tn(
            bm[g], (xdt[g] * jnp.exp(a_last - a_exp[g])).astype(BF16))
        y = y_diag[g] + y_off[g] * jnp.exp(a_exp[g]) + xs[g] * dsk_ref[:, cols]
        y = y * jax.nn.silu(z_ref[:, cols])
        ms = jnp.mean(y * y, axis=-1, keepdims=True)
        o_ref[:, cols] = (y * lax.rsqrt(ms + SSD_EPS) * nw_ref[:, cols]).astype(o_ref.dtype)


def _ssd_mixer(proj, bsz, seq, conv_w, conv_b, dt_bias, a_log, d_skip, norm_w, tri):
    tb = TIME_BLOCK
    nc = seq // tb
    gw = SSD_GROUP_W
    d = D_MODEL
    bcw = 2 * SSD_GN
    cwx, cwbc = conv_w[:, :d], conv_w[:, d:]
    cbx, cbbc = _row2(conv_b[:d]), _row2(conv_b[d:])
    head_lanes = lambda v: _row2(jnp.pad(v, (0, LANES - SSD_HEADS)))
    grp = jnp.arange(SSD_GROUPS)[:, None, None]
    src = jnp.arange(LANES)[None, :, None]
    dst = jnp.arange(gw)[None, None, :]
    expand = (src == grp * SSD_HPG + dst // SSD_HEAD_DIM).astype(BF16)

    row = lambda b, c: b * nc + c
    full = lambda shape: pl.BlockSpec(shape, lambda b, c: (0,) * len(shape))
    in_specs = [
        pl.BlockSpec((tb, d), lambda b, c: (row(b, c), C_Z // d)),
        pl.BlockSpec((tb, d), lambda b, c: (row(b, c), C_X // d)),
        pl.BlockSpec((tb, bcw), lambda b, c: (row(b, c), C_BC // bcw)),
        pl.BlockSpec((tb, LANES), lambda b, c: (row(b, c), C_SMALL // LANES)),
        full((4, d)), full((1, d)), full((4, bcw)), full((1, bcw)),
        full((1, LANES)), full((1, LANES)), full((1, d)), full((1, d)),
        full((SSD_GROUPS, LANES, gw)), full((tb, tb)), full((tb, 3 * (tb + CONV_HALO))),
    ]
    return pl.pallas_call(
        _ssd_body,
        grid=(bsz, nc),
        in_specs=in_specs,
        out_specs=pl.BlockSpec((tb, d), lambda b, c: (row(b, c), 0)),
        out_shape=jax.ShapeDtypeStruct((bsz * seq, d), BF16),
        scratch_shapes=[pltpu.VMEM((tb + CONV_HALO, d), F32),
                        pltpu.VMEM((tb + CONV_HALO, bcw), F32),
                        pltpu.VMEM((SSD_GROUPS, SSD_STATE, gw), F32)],
        compiler_params=_cparams(2),
        name="ssd_mixer",
    )(proj, proj, proj, proj, cwx, cbx, cwbc, cbbc, head_lanes(dt_bias), head_lanes(a_log),
      _row2(jnp.repeat(d_skip, SSD_HEAD_DIM)), _row2(norm_w), expand, tri, _conv_shift_matrix(tb))


def _mlstm_body(q_ref, k_ref, v_ref, og_ref, sm_ref, sel_ref, cwq_ref, cbq_ref, cwk_ref, cbk_ref,
                gb_ref, nw_ref, tri_ref, shift_ref, o_ref, qbuf, kbuf, ct_ref, n_ref, m_ref):
    rows = q_ref.shape[0]

    @pl.when(pl.program_id(2) == 0)
    def _():
        qbuf[0:CONV_HALO, :] = jnp.zeros((CONV_HALO, qbuf.shape[1]), F32)
        kbuf[0:CONV_HALO, :] = jnp.zeros((CONV_HALO, kbuf.shape[1]), F32)
        ct_ref[...] = jnp.zeros(ct_ref.shape, F32)
        n_ref[...] = jnp.zeros(n_ref.shape, F32)
        m_ref[...] = jnp.zeros(m_ref.shape, F32)

    heads = range(q_ref.shape[1] // ML_DK)
    q_all = _conv4_silu(qbuf, q_ref, cwq_ref, cbq_ref, shift_ref, rows) * (ML_DK ** -0.5)
    k_all = _conv4_silu(kbuf, k_ref, cwk_ref, cbk_ref, shift_ref, rows)
    qf = [q_all[:, i * ML_DK:(i + 1) * ML_DK] for i in heads]
    q = [qf[i].astype(BF16) for i in heads]
    k = [k_all[:, i * ML_DK:(i + 1) * ML_DK] for i in heads]
    kb = [k[i].astype(BF16) for i in heads]
    v = [v_ref[:, i * ML_DV:(i + 1) * ML_DV] for i in heads]

    pre = _dotx_l(sm_ref[...], sel_ref[...], 3) + gb_ref[...]
    log_f = jax.nn.log_sigmoid(pre)
    bcum_all = _dotx_r(tri_ref[...], log_f, 3)
    lane = lax.broadcasted_iota(jnp.int32, (rows, LANES), 1)
    gates_t = jnp.where(lane % 2 == 0, pre, bcum_all).T
    t_row = lax.broadcasted_iota(jnp.int32, (rows, rows), 0)
    t_col = lax.broadcasted_iota(jnp.int32, (rows, rows), 1)
    causal = t_row >= t_col

    li = [pre[:, 2 * i:2 * i + 1] for i in heads]
    bcum = [bcum_all[:, 2 * i + 1:2 * i + 2] for i in heads]
    log_d = [jnp.where(causal, bcum[i] - gates_t[2 * i + 1:2 * i + 2, :] + gates_t[2 * i:2 * i + 1, :],
                       -jnp.inf) for i in heads]
    qk = [_dot_nt(q[i], kb[i]) for i in heads]
    ct = [ct_ref[i] for i in heads]
    inter = [_dot(q[i], ct[i].astype(BF16)) for i in heads]
    m_prev = [m_ref[i, 0:1, 0:1] for i in heads]
    m_inter = [bcum[i] + m_prev[i] for i in heads]
    m_t = [jnp.maximum(jnp.max(log_d[i], axis=-1, keepdims=True), m_inter[i]) for i in heads]
    w = [jnp.exp(log_d[i] - m_t[i]) * qk[i] for i in heads]
    s_inter = [jnp.exp(m_inter[i] - m_t[i]) for i in heads]
    nvec = [n_ref[i, 0:1, :] for i in heads]
    num = [_dot(w[i].astype(BF16), v[i].astype(BF16)) + s_inter[i] * inter[i] for i in heads]
    den = [jnp.sum(w[i], axis=-1, keepdims=True)
           + s_inter[i] * jnp.sum(qf[i] * nvec[i], axis=-1, keepdims=True)
           for i in heads]
    h = [num[i] / jnp.maximum(jnp.abs(den[i]), jnp.exp(-m_t[i])) for i in heads]

    b_tot = [bcum[i][rows - 1:rows, :] for i in heads]
    log_w = [b_tot[i] - bcum[i] + li[i] for i in heads]
    m_new = [jnp.maximum(b_tot[i] + m_prev[i], jnp.max(log_w[i], axis=0, keepdims=True))
             for i in heads]
    w_end = [jnp.exp(log_w[i] - m_new[i]) for i in heads]
    s_old = [jnp.exp(b_tot[i] + m_prev[i] - m_new[i]) for i in heads]
    ct_new = [s_old[i] * ct[i] + _dot_tn(kb[i], (w_end[i] * v[i]).astype(BF16)) for i in heads]
    for i in heads:
        ct_ref[i] = ct_new[i]
        n_ref[i] = jnp.broadcast_to(
            s_old[i] * nvec[i] + jnp.sum(w_end[i] * k[i], axis=0, keepdims=True), n_ref.shape[1:])
        m_ref[i] = jnp.broadcast_to(m_new[i], m_ref.shape[1:])
        hn = h[i] * lax.rsqrt(jnp.mean(h[i] * h[i], axis=-1, keepdims=True) + ML_EPS)
        cols = slice(i * ML_DV, (i + 1) * ML_DV)
        o_ref[:, cols] = (jax.nn.sigmoid(og_ref[:, cols]) * hn * nw_ref[:, cols]).astype(o_ref.dtype)


def _mlstm_mixer(proj, bsz, seq, conv_w, conv_b, i_bias, f_bias, norm_w, tri):
    tb = TIME_BLOCK
    nc = seq // tb
    hg = ML_GROUP
    n_groups = ML_HEADS // hg
    qw, vw = hg * ML_DK, hg * ML_DV
    small_i = SSD_HEADS
    small_f = SSD_HEADS + ML_HEADS
    src = jnp.arange(LANES)[None, :, None]
    dst = jnp.arange(LANES)[None, None, :]
    grp = jnp.arange(n_groups)[:, None, None]
    head = grp * hg + dst // 2
    sel = ((dst < 2 * hg) & (((dst % 2 == 0) & (src == small_i + head))
                             | ((dst % 2 == 1) & (src == small_f + head)))).astype(BF16)
    gate_bias = jnp.stack([i_bias.reshape(n_groups, hg), f_bias.reshape(n_groups, hg)], axis=-1)
    gate_bias = jnp.pad(gate_bias.reshape(n_groups, 1, 2 * hg), ((0, 0), (0, 0), (0, LANES - 2 * hg)))
    conv_b2 = _row2(conv_b)
    kq = n_groups
    row = lambda b, h, c: b * nc + c
    in_specs = [
        pl.BlockSpec((tb, qw), lambda b, h, c: (row(b, h, c), C_QK // qw + h)),
        pl.BlockSpec((tb, qw), lambda b, h, c: (row(b, h, c), C_QK // qw + kq + h)),
        pl.BlockSpec((tb, vw), lambda b, h, c: (row(b, h, c), C_MV // vw + h)),
        pl.BlockSpec((tb, vw), lambda b, h, c: (row(b, h, c), C_MO // vw + h)),
        pl.BlockSpec((tb, LANES), lambda b, h, c: (row(b, h, c), C_SMALL // LANES)),
        pl.BlockSpec((None, LANES, LANES), lambda b, h, c: (h, 0, 0)),
        pl.BlockSpec((4, qw), lambda b, h, c: (0, h)),
        pl.BlockSpec((1, qw), lambda b, h, c: (0, h)),
        pl.BlockSpec((4, qw), lambda b, h, c: (0, kq + h)),
        pl.BlockSpec((1, qw), lambda b, h, c: (0, kq + h)),
        pl.BlockSpec((None, 1, LANES), lambda b, h, c: (h, 0, 0)),
        pl.BlockSpec((1, vw), lambda b, h, c: (0, h)),
        pl.BlockSpec((tb, tb), lambda b, h, c: (0, 0)),
        pl.BlockSpec((tb, 3 * (tb + CONV_HALO)), lambda b, h, c: (0, 0)),
    ]
    return pl.pallas_call(
        _mlstm_body,
        grid=(bsz, n_groups, nc),
        in_specs=in_specs,
        out_specs=pl.BlockSpec((tb, vw), lambda b, h, c: (row(b, h, c), h)),
        out_shape=jax.ShapeDtypeStruct((bsz * seq, D_MODEL), BF16),
        scratch_shapes=[pltpu.VMEM((tb + CONV_HALO, qw), F32),
                        pltpu.VMEM((tb + CONV_HALO, qw), F32),
                        pltpu.VMEM((hg, ML_DK, ML_DV), F32),
                        pltpu.VMEM((hg, HALO, LANES), F32),
                        pltpu.VMEM((hg, HALO, LANES), F32)],
        compiler_params=_cparams(3),
        name="mlstm_mixer",
    )(proj, proj, proj, proj, proj, sel, conv_w, conv_b2, conv_w, conv_b2, gate_bias,
      _row2(norm_w), tri, _conv_shift_matrix(tb))


def _stack_heads(x):
    lane = lax.broadcasted_iota(jnp.int32, x.shape, 1)
    return jnp.concatenate([jnp.where(lane < RWKV_HEAD_DIM, x, 0.0),
                            jnp.where(lane >= RWKV_HEAD_DIM, x, 0.0)], axis=0)


def _shift_lerp(buf, src_ref, mu_ref, rows):
    buf[HALO:HALO + rows, :] = src_ref[...]
    cur = buf[HALO:HALO + rows, :]
    prev = buf[HALO - 1:HALO - 1 + rows, :]
    buf[0:HALO, :] = buf[rows:rows + HALO, :]
    return cur + (prev - cur) * mu_ref[...]


RWKV_PASSES_SCORES = 1
RWKV_PASSES_SOLVE = 1
RWKV_PASSES_VALUES = 1
RWKV_PASSES_STATE = 1
RWKV_GROUP = 8
RWKV_TIME_BLOCK = 256
RWKV_WAVE_CHUNKS = 4


def _rwkv_body(r_ref, k_ref, v_ref, lo_ref, mur_ref, muk_ref, muv_ref, mulo_ref, w0_ref, w2_ref,
               a0_ref, a2_ref, g2_ref, kk_ref, ka_ref, rk_ref, lnw_ref, lnb_ref, tri_ref,
               ones_ref, o_ref, rbuf, kbuf, vbuf, lobuf, z_ref, y_buf):
    rows = r_ref.shape[0]
    n_pairs = r_ref.shape[1] // LANES
    ch = RWKV_CHUNK
    hd = RWKV_HEAD_DIM

    @pl.when(pl.program_id(2) == 0)
    def _():
        for buf in (rbuf, kbuf, vbuf, lobuf):
            buf[0:HALO, :] = jnp.zeros((HALO, buf.shape[1]), F32)
        z_ref[...] = jnp.zeros(z_ref.shape, F32)

    r = _shift_lerp(rbuf, r_ref, mur_ref, rows)
    k = _shift_lerp(kbuf, k_ref, muk_ref, rows)
    v = _shift_lerp(vbuf, v_ref, muv_ref, rows)
    lo = _shift_lerp(lobuf, lo_ref, mulo_ref, rows)
    w_lo, a_lo, g_lo = lo[:, 0:LANES], lo[:, LANES:2 * LANES], lo[:, 2 * LANES:]

    log_w = -jax.nn.softplus(-(w0_ref[...] + _mm(jnp.tanh(w_lo), w2_ref[...], 3))) - 0.5
    lw = -jnp.exp(log_w)
    a = jax.nn.sigmoid(a0_ref[...] + _mm(a_lo, a2_ref[...], 3))
    gate = _mm(jax.nn.sigmoid(g_lo), g2_ref[...], 1)

    head_sum = ones_ref[...]

    def head_total(x):
        return jnp.concatenate([_dotx_l(x[:, p * LANES:(p + 1) * LANES], head_sum, 2)
                                for p in range(n_pairs)], axis=1)

    kk = k * kk_ref[...]
    kk = kk * lax.rsqrt(jnp.maximum(head_total(kk * kk), 1e-24))
    kh = k * (1.0 + (a - 1.0) * ka_ref[...])
    kb = kk * a
    cum = _dotx_r(tri_ref[...], lw, 3)

    c_row = lax.broadcasted_iota(jnp.int32, (ch, LANES), 0)
    c_col = lax.broadcasted_iota(jnp.int32, (ch, LANES), 1) % hd
    strict = c_col < c_row
    incl = c_col <= c_row
    same_sub = (c_col // RWKV_SUB) == (c_row // RWKV_SUB)
    eye_pair = (c_col == c_row).astype(F32)
    s_row = lax.broadcasted_iota(jnp.int32, (LANES, LANES), 0)
    s_col = lax.broadcasted_iota(jnp.int32, (LANES, LANES), 1)
    same_head = (s_row // hd) == (s_col // hd)
    eye_full = (s_row == s_col).astype(F32)

    solve = functools.partial(_mm, passes=RWKV_PASSES_SOLVE)
    vals = functools.partial(_mm, passes=RWKV_PASSES_VALUES)

    n_chunks = rows // ch
    a_hat, r_hat, k_til, b_til, k_end, b_end, vp, decay_end = [], [], [], [], [], [], [], []
    for j in range(n_chunks):
        sl = slice(j * ch, (j + 1) * ch)
        cm, lwj, kkj, khj, kbj, rj, vj = cum[sl], lw[sl], kk[sl], kh[sl], kb[sl], r[sl], v[sl]
        cm_end = cm[ch - 1:ch, :]
        e_neg = jnp.exp(-cm)
        e_end = jnp.exp(cm_end - cm)
        wide = (-kkj * jnp.exp(cm - lwj), rj * jnp.exp(cm), khj * e_neg, kbj * e_neg,
                khj * e_end, kbj * e_end, vj, jnp.exp(cm_end))
        for pi in range(n_pairs):
            ls = slice(pi * LANES, (pi + 1) * LANES)
            for dst, src in zip((a_hat, r_hat, k_til, b_til, k_end, b_end, vp, decay_end), wide):
                dst.append(src[:, ls])
    z = [z_ref[pi] for pi in range(n_pairs)]
    for c0 in range(0, n_chunks, RWKV_WAVE_CHUNKS):
        wave = range(c0, min(c0 + RWKV_WAVE_CHUNKS, n_chunks))
        units = [j * n_pairs + pi for j in wave for pi in range(n_pairs)]
        each = lambda fn: {u: fn(u) for u in units}

        big = each(lambda u: _mm(
            jnp.concatenate([a_hat[u], r_hat[u]], axis=0),
            jnp.concatenate([_stack_heads(k_til[u]), _stack_heads(b_til[u])], axis=0),
            RWKV_PASSES_SCORES, dot=_dot_nt))
        l_ak = each(lambda u: jnp.where(strict, big[u][0:ch, 0:LANES], 0.0))
        l_ab = each(lambda u: jnp.where(strict, big[u][0:ch, LANES:], 0.0))
        m_rk = each(lambda u: jnp.where(incl, big[u][ch:, 0:LANES], 0.0))
        m_rb = each(lambda u: jnp.where(incl, big[u][ch:, LANES:], 0.0))

        l_d = each(lambda u: jnp.where(same_sub, l_ab[u], 0.0))
        l_o = each(lambda u: l_ab[u] - l_d[u])
        x = each(lambda u: eye_pair + l_d[u])
        p = each(lambda u: solve(l_d[u], _stack_heads(l_d[u])))
        for _ in range(2):
            both = each(lambda u: solve(jnp.concatenate([p[u], x[u]], axis=0), _stack_heads(p[u])))
            x = each(lambda u: x[u] + both[u][ch:])
            p = each(lambda u: both[u][:ch])
        t_d = each(lambda u: x[u] + solve(x[u], _stack_heads(p[u])))
        nn = each(lambda u: solve(t_d[u], _stack_heads(l_o[u])))
        n2 = each(lambda u: solve(nn[u], _stack_heads(nn[u])))
        mx = each(lambda u: eye_pair + nn[u])
        mx = each(lambda u: mx[u] + solve(mx[u], _stack_heads(n2[u])))
        t = each(lambda u: solve(mx[u], _stack_heads(t_d[u])))

        lmv = each(lambda u: vals(jnp.concatenate([l_ak[u], m_rk[u]], axis=0), _stack_heads(vp[u])))
        wu = each(lambda u: vals(t[u], jnp.concatenate(
            [_stack_heads(a_hat[u]), _stack_heads(lmv[u][:ch])], axis=1)))
        w_a = each(lambda u: wu[u][:, 0:LANES])
        u_v = each(lambda u: wu[u][:, LANES:])
        mw = each(lambda u: vals(m_rb[u], jnp.concatenate(
            [_stack_heads(w_a[u]), _stack_heads(u_v[u])], axis=1)))
        q_eff = each(lambda u: r_hat[u] + mw[u][:, 0:LANES])
        y_loc = each(lambda u: lmv[u][ch:] + mw[u][:, LANES:])
        a_eff = each(lambda u: jnp.where(
            same_head, _mm(b_end[u], w_a[u], RWKV_PASSES_STATE, dot=_dot_tn), 0.0)
            + eye_full * decay_end[u])
        z_loc = each(lambda u: jnp.where(
            same_head, _mm(jnp.concatenate([k_end[u], b_end[u]], axis=0),
                           jnp.concatenate([vp[u], u_v[u]], axis=0), RWKV_PASSES_VALUES,
                           dot=_dot_tn), 0.0))

        for j in wave:
            sl = slice(j * ch, (j + 1) * ch)
            us = [j * n_pairs + pi for pi in range(n_pairs)]
            qz = [_mm(jnp.concatenate([q_eff[u], a_eff[u]], axis=0), z[pi], RWKV_PASSES_STATE)
                  for pi, u in enumerate(us)]
            for pi, u in enumerate(us):
                y_buf[sl, pi * LANES:(pi + 1) * LANES] = qz[pi][:ch] + y_loc[u]
            z = [qz[pi][ch:] + z_loc[u] for pi, u in enumerate(us)]
    for pi in range(n_pairs):
        z_ref[pi] = z[pi]

    y = y_buf[...]
    mean = head_total(y) * (1.0 / hd)
    yc = y - mean
    var = head_total(yc * yc) * (1.0 / hd)
    yn = yc * lax.rsqrt(var + RWKV_LN_EPS) * lnw_ref[...] + lnb_ref[...]
    bonus = head_total(r * kh * rk_ref[...]) * v
    o_ref[...] = ((yn + bonus) * gate).astype(o_ref.dtype)


def _rwkv_mixer(proj, bsz, seq, mu, w0, w2, a0, a2, g2, k_k, k_a, r_k, ln_w, ln_b):
    tb = RWKV_TIME_BLOCK
    nc = seq // tb
    d = D_MODEL
    lora_w = 2 * LANES + RWKV_LORA_G
    mur, muk, muv = _row2(mu[0:d]), _row2(mu[d:2 * d]), _row2(mu[2 * d:3 * d])
    o = 3 * d
    pad_l = lambda x, n: jnp.pad(x, (0, n - x.shape[0]))
    mulo = _row2(jnp.concatenate([pad_l(mu[o:o + RWKV_LORA_W], LANES),
                                  pad_l(mu[o + RWKV_LORA_W:o + RWKV_LORA_W + RWKV_LORA_A], LANES),
                                  mu[o + RWKV_LORA_W + RWKV_LORA_A:]]))
    w2p = jnp.pad(w2, ((0, LANES - RWKV_LORA_W), (0, 0)))
    a2p = jnp.pad(a2, ((0, LANES - RWKV_LORA_A), (0, 0)))
    ch = RWKV_CHUNK
    t_i = jnp.arange(tb)[:, None]
    t_j = jnp.arange(tb)[None, :]
    tri = ((t_i >= t_j) & (t_i // ch == t_j // ch)).astype(BF16)
    l_i = jnp.arange(LANES)[:, None]
    l_j = jnp.arange(LANES)[None, :]
    head_sum = (l_i // RWKV_HEAD_DIM == l_j // RWKV_HEAD_DIM).astype(BF16)

    gw = RWKV_GROUP * LANES
    row = lambda b, p, c: b * nc + c
    col = lambda base: (lambda b, p, c: (row(b, p, c), base // gw + p))
    vec = pl.BlockSpec((1, gw), lambda b, p, c: (0, p))
    in_specs = [
        pl.BlockSpec((tb, gw), col(C_R)),
        pl.BlockSpec((tb, gw), col(C_K)),
        pl.BlockSpec((tb, gw), col(C_V)),
        pl.BlockSpec((tb, lora_w), lambda b, p, c: (row(b, p, c), C_LORA // lora_w)),
        vec, vec, vec,
        pl.BlockSpec((1, lora_w), lambda b, p, c: (0, 0)),
        vec,
        pl.BlockSpec((LANES, gw), lambda b, p, c: (0, p)),
        vec,
        pl.BlockSpec((LANES, gw), lambda b, p, c: (0, p)),
        pl.BlockSpec((RWKV_LORA_G, gw), lambda b, p, c: (0, p)),
        vec, vec, vec, vec, vec,
        pl.BlockSpec((tb, tb), lambda b, p, c: (0, 0)),
        pl.BlockSpec((LANES, LANES), lambda b, p, c: (0, 0)),
    ]
    return pl.pallas_call(
        _rwkv_body,
        grid=(bsz, RWKV_PAIRS // RWKV_GROUP, nc),
        in_specs=in_specs,
        out_specs=pl.BlockSpec((tb, gw), lambda b, p, c: (row(b, p, c), p)),
        out_shape=jax.ShapeDtypeStruct((bsz * seq, D_MODEL), BF16),
        scratch_shapes=[pltpu.VMEM((tb + HALO, gw), F32),
                        pltpu.VMEM((tb + HALO, gw), F32),
                        pltpu.VMEM((tb + HALO, gw), F32),
                        pltpu.VMEM((tb + HALO, lora_w), F32),
                        pltpu.VMEM((RWKV_GROUP, LANES, LANES), F32),
                        pltpu.VMEM((tb, gw), F32)],
        compiler_params=_cparams(3),
        name="rwkv_mixer",
    )(proj, proj, proj, proj, mur, muk, muv, mulo, _row2(w0), w2p, _row2(a0), a2p, g2,
      _row2(k_k), _row2(k_a), _row2(r_k.reshape(-1)), _row2(ln_w), _row2(ln_b), tri, head_sum)


def _gated_merge_body(ys_ref, yr_ref, ym_ref, ps_ref, pr_ref, pm_ref, gs_ref, gr_ref, gm_ref, o_ref):
    acc = jax.nn.sigmoid(gs_ref[...]) * _dot(ys_ref[...], ps_ref[...])
    acc = acc + jax.nn.sigmoid(gr_ref[...]) * _dot(yr_ref[...], pr_ref[...])
    acc = acc + jax.nn.sigmoid(gm_ref[...]) * _dot(ym_ref[...], pm_ref[...])
    o_ref[...] = acc.astype(o_ref.dtype)


def _gated_merge(ys, ps, proj, tm=512, tn=1024):
    m, k = ys[0].shape
    tm = min(tm, m)
    n = ps[0].shape[1]
    y_spec = pl.BlockSpec((tm, k), lambda j, i: (i, 0))
    p_spec = pl.BlockSpec((k, tn), lambda j, i: (0, j), pipeline_mode=pl.Buffered(1))
    gate_spec = lambda col: pl.BlockSpec((tm, tn), lambda j, i: (i, col // tn + j))
    return pl.pallas_call(
        _gated_merge_body,
        grid=(n // tn, m // tm),
        in_specs=[y_spec] * 3 + [p_spec] * 3 + [gate_spec(C_GS), gate_spec(C_GR), gate_spec(C_GM)],
        out_specs=pl.BlockSpec((tm, tn), lambda j, i: (i, j)),
        out_shape=jax.ShapeDtypeStruct((m, n), BF16),
        compiler_params=_cparams(2),
        name="gated_merge",
    )(*ys, *ps, proj, proj, proj)


def _mm_res_ln_body(a_ref, w_ref, h_ref, g_ref, b_ref, o_ref, ob_ref, *, row_split):
    n = a_ref.shape[0] // row_split
    subs = [slice(s * n, (s + 1) * n) for s in range(row_split)]
    acc = [_dot(a_ref[rows, :], w_ref[...]) for rows in subs]
    for rows, mix in zip(subs, acc):
        h = _layer_norm_rows(ALPHA * h_ref[rows, :] + mix, g_ref[...], b_ref[...])
        o_ref[rows, :] = h
        ob_ref[rows, :] = h.astype(BF16)


def _matmul_res_ln(a, w, h, g, b, tm, row_split):
    m, k = a.shape
    n = w.shape[1]
    const = lambda shape: pl.BlockSpec(shape, lambda i: (0, 0), pipeline_mode=pl.Buffered(1))
    return pl.pallas_call(
        functools.partial(_mm_res_ln_body, row_split=row_split),
        grid=(m // tm,),
        in_specs=[pl.BlockSpec((tm, k), lambda i: (i, 0)),
                  const((k, n)),
                  pl.BlockSpec((tm, n), lambda i: (i, 0)),
                  const((1, n)), const((1, n))],
        out_specs=[pl.BlockSpec((tm, n), lambda i: (i, 0)),
                   pl.BlockSpec((tm, n), lambda i: (i, 0))],
        out_shape=[jax.ShapeDtypeStruct((m, n), F32), jax.ShapeDtypeStruct((m, n), BF16)],
        compiler_params=_cparams(1),
        name="matmul_res_ln",
    )(a, w, h, _row2(g), _row2(b))


FFN_HALO = 16


def _ffn_up_body(h_ref, hp_ref, wg_ref, wu_ref, cwg_ref, cbg_ref, cwu_ref, cbu_ref, o_ref,
                 gbuf, ubuf, *, blocks_per_seq):
    rows = h_ref.shape[0]
    first = (pl.program_id(1) % blocks_per_seq) == 0
    keep = jnp.where(first, 0.0, 1.0)

    def conv(buf, cw_ref, cb_ref):
        acc = cb_ref[...] + cw_ref[2:3, :] * buf[FFN_HALO:FFN_HALO + rows, :]
        acc = acc + cw_ref[1:2, :] * buf[FFN_HALO - 1:FFN_HALO - 1 + rows, :]
        return acc + cw_ref[0:1, :] * buf[FFN_HALO - 2:FFN_HALO - 2 + rows, :]

    for buf, w_ref in ((gbuf, wg_ref), (ubuf, wu_ref)):
        buf[0:FFN_HALO, :] = _dot(hp_ref[...], w_ref[...]) * keep
        buf[FFN_HALO:FFN_HALO + rows, :] = _dot(h_ref[...], w_ref[...])
    gate = conv(gbuf, cwg_ref, cbg_ref)
    up = conv(ubuf, cwu_ref, cbu_ref)
    gelu = 0.5 * gate * (1.0 + lax.erf(gate * (2.0 ** -0.5)))
    o_ref[...] = (gelu * up).astype(o_ref.dtype)


def _ffn_up(hb, wg, wu, cwg, cbg, cwu, cbu, seq, tm=1024):
    m, k = hb.shape
    n = wg.shape[1]
    tn = FFN_TN
    halo_blocks = tm // FFN_HALO
    body = functools.partial(_ffn_up_body, blocks_per_seq=seq // tm)
    return pl.pallas_call(
        body,
        grid=(n // tn, m // tm),
        in_specs=[pl.BlockSpec((tm, k), lambda j, i: (i, 0)),
                  pl.BlockSpec((FFN_HALO, k), lambda j, i: (jnp.maximum(i * halo_blocks - 1, 0), 0)),
                  pl.BlockSpec((k, tn), lambda j, i: (0, j)),
                  pl.BlockSpec((k, tn), lambda j, i: (0, j)),
                  pl.BlockSpec((3, tn), lambda j, i: (0, j)),
                  pl.BlockSpec((1, tn), lambda j, i: (0, j)),
                  pl.BlockSpec((3, tn), lambda j, i: (0, j)),
                  pl.BlockSpec((1, tn), lambda j, i: (0, j))],
        out_specs=pl.BlockSpec((tm, tn), lambda j, i: (i, j)),
        out_shape=jax.ShapeDtypeStruct((m, n), BF16),
        scratch_shapes=[pltpu.VMEM((tm + FFN_HALO, tn), F32),
                        pltpu.VMEM((tm + FFN_HALO, tn), F32)],
        compiler_params=_cparams(2),
        name="ffn_up",
    )(hb, hb, wg, wu, cwg, cbg, cwu, cbu)


def _pack_w_in(w):
    d = D_MODEL
    o = 0

    def take(n):
        nonlocal o
        s = w[:, o:o + n]
        o += n
        return s

    z = take(d)
    x = take(d)
    bc = take(2 * SSD_GN)
    dt = take(SSD_HEADS)
    r, k, v = take(d), take(d), take(d)
    w_lo, a_lo, g_lo = take(RWKV_LORA_W), take(RWKV_LORA_A), take(RWKV_LORA_G)
    qk, mv, mo = take(d), take(d), take(d)
    gi, gf = take(ML_HEADS), take(ML_HEADS)
    gs, gr, gm = take(d), take(d), take(d)
    zeros = lambda n: jnp.zeros((d, n), w.dtype)
    small = jnp.concatenate([dt, gi, gf, zeros(LANES - SSD_HEADS - 2 * ML_HEADS)], axis=1)
    lora = jnp.concatenate([w_lo, zeros(LANES - RWKV_LORA_W), a_lo, zeros(LANES - RWKV_LORA_A), g_lo],
                           axis=1)
    packed = jnp.concatenate([z, x, r, k, v, qk, mv, mo, gs, gr, gm, bc, lora, small], axis=1)
    packed = jnp.concatenate([packed, zeros(N_PROJ - packed.shape[1])], axis=1)
    return packed.astype(BF16)


def _pad_cols(x, n):
    return jnp.pad(x, ((0, 0), (0, n - x.shape[1])))


def kernel(x, ln_in_g, ln_in_b, w_in, ssd_conv_w, ssd_conv_b, ssd_dt_bias, ssd_a_log, ssd_d, ssd_norm_w, rwkv_mu, rwkv_w0, rwkv_w2, rwkv_a0, rwkv_a2, rwkv_g2, rwkv_k_k, rwkv_k_a, rwkv_r_k, rwkv_ln_w, rwkv_ln_b, mlstm_conv_w, mlstm_conv_b, mlstm_i_bias, mlstm_f_bias, mlstm_norm_w, proj_ssd, proj_rwkv, proj_mlstm, w_out, ln1_g, ln1_b, ffn_w_up, ffn_conv_w, ffn_conv_b, ffn_w_down, ln2_g, ln2_b):
    bsz, seq, d = x.shape
    n_tok = bsz * seq
    tb = TIME_BLOCK
    t_i = jnp.arange(tb)[:, None]
    t_j = jnp.arange(tb)[None, :]
    tri = (t_i >= t_j).astype(BF16)

    h, hb = _ln_in(x.reshape(n_tok, d), ln_in_g, ln_in_b)
    for l in range(DEPTH):
        proj = _matmul(hb, _pack_w_in(w_in[l]), tm=min(1024, n_tok), tn=2048, out_dtype=F32,
                       name="in_proj")
        y_ssd = _ssd_mixer(proj, bsz, seq, ssd_conv_w[l], ssd_conv_b[l], ssd_dt_bias[l],
                           ssd_a_log[l], ssd_d[l], ssd_norm_w[l], tri)
        y_rwkv = _rwkv_mixer(proj, bsz, seq, rwkv_mu[l], rwkv_w0[l], rwkv_w2[l], rwkv_a0[l],
                             rwkv_a2[l], rwkv_g2[l], rwkv_k_k[l], rwkv_k_a[l], rwkv_r_k[l],
                             rwkv_ln_w[l], rwkv_ln_b[l])
        y_mlstm = _mlstm_mixer(proj, bsz, seq, mlstm_conv_w[l], mlstm_conv_b[l], mlstm_i_bias[l],
                               mlstm_f_bias[l], mlstm_norm_w[l], tri)
        merged = _gated_merge(
            (y_ssd, y_rwkv, y_mlstm),
            (proj_ssd[l].astype(BF16), proj_rwkv[l].astype(BF16), proj_mlstm[l].astype(BF16)), proj)
        h, hb = _matmul_res_ln(merged, w_out[l].astype(BF16), h, ln1_g[l], ln1_b[l],
                               tm=min(512, n_tok), row_split=2)

        wg = _pad_cols(ffn_w_up[l][:, :FFN_D], FFN_PAD).astype(BF16)
        wu = _pad_cols(ffn_w_up[l][:, FFN_D:], FFN_PAD).astype(BF16)
        cwg = _pad_cols(ffn_conv_w[l][:, :FFN_D], FFN_PAD)
        cwu = _pad_cols(ffn_conv_w[l][:, FFN_D:], FFN_PAD)
        cbg = _pad_cols(_row2(ffn_conv_b[l][:FFN_D]), FFN_PAD)
        cbu = _pad_cols(_row2(ffn_conv_b[l][FFN_D:]), FFN_PAD)
        act = _ffn_up(hb, wg, wu, cwg, cbg, cwu, cbu, seq, tm=min(1024, seq))
        wd = jnp.pad(ffn_w_down[l], ((0, FFN_PAD - FFN_D), (0, 0))).astype(BF16)
        h, hb = _matmul_res_ln(act, wd, h, ln2_g[l], ln2_b[l], tm=min(256, n_tok), row_split=1)
    return h.reshape(bsz, seq, d)
```

```python
import functools
import math

import jax
import jax.numpy as jnp
from jax import lax
from jax.experimental import pallas as pl
from jax.experimental.pallas import tpu as pltpu

F32 = jnp.float32
BF16 = jnp.bfloat16

D_MODEL = 2048
DEPTH = 2
ALPHA = (2.0 * DEPTH) ** 0.25
LN_EPS = 1e-5

SSD_HEADS = 32
SSD_HEAD_DIM = 64
SSD_GROUPS = 4
SSD_STATE = 128
SSD_GN = SSD_GROUPS * SSD_STATE
SSD_EPS = 1e-5
SSD_GROUP_W = D_MODEL // SSD_GROUPS
SSD_HPG = SSD_HEADS // SSD_GROUPS

RWKV_HEAD_DIM = 64
RWKV_PAIRS = D_MODEL // 128
RWKV_LORA_W = 96
RWKV_LORA_A = 96
RWKV_LORA_G = 256
RWKV_LN_EPS = 64e-5
RWKV_CHUNK = 64
RWKV_SUB = 16

ML_HEADS = 8
ML_DK = 128
ML_DV = 256
ML_EPS = 1e-6
ML_GROUP = 8

FFN_D = 5504
FFN_PAD = 5632
FFN_TN = 512

LANES = 128
HALO = 8
TIME_BLOCK = 128

C_Z = 0
C_X = 2048
C_R = 4096
C_K = 6144
C_V = 8192
C_QK = 10240
C_MV = 12288
C_MO = 14336
C_GS = 16384
C_GR = 18432
C_GM = 20480
C_BC = 22528
C_LORA = 23552
C_SMALL = 24064
N_PROJ = 24576

VMEM_LIMIT = 56 * 1024 * 1024


def _dot(a, b):
    return jnp.dot(a, b, preferred_element_type=F32)


def _dot_nt(a, b):
    return lax.dot_general(a, b, (((1,), (1,)), ((), ())), preferred_element_type=F32)


def _dot_tn(a, b):
    return lax.dot_general(a, b, (((0,), (0,)), ((), ())), preferred_element_type=F32)


def _split(x, n):
    parts, r = [], x
    for i in range(n):
        p = r.astype(BF16)
        parts.append(p)
        if i + 1 < n:
            r = r - p.astype(F32)
    return parts


def _dotx_l(x, w_exact, n, dot=_dot):
    acc = None
    for p in _split(x, n):
        t = dot(p, w_exact)
        acc = t if acc is None else acc + t
    return acc


def _dotx_r(w_exact, x, n, dot=_dot):
    acc = None
    for p in _split(x, n):
        t = dot(w_exact, p)
        acc = t if acc is None else acc + t
    return acc


def _mm(a, b, passes, dot=_dot):
    if passes == 1:
        return dot(a.astype(BF16), b.astype(BF16))
    ah, al = _split(a, 2)
    bh, bl = _split(b, 2)
    return dot(ah, bh) + (dot(ah, bl) + dot(al, bh))


def _layer_norm_rows(x, g, b):
    mu = jnp.mean(x, axis=-1, keepdims=True)
    xc = x - mu
    var = jnp.mean(xc * xc, axis=-1, keepdims=True)
    return xc * lax.rsqrt(var + LN_EPS) * g + b


def _cparams(n_axes):
    return pltpu.CompilerParams(dimension_semantics=("arbitrary",) * n_axes,
                                vmem_limit_bytes=VMEM_LIMIT)


def _row2(v):
    return v.reshape(1, -1)


def _ln_in_body(x_ref, g_ref, b_ref, h_ref, hb_ref):
    h = _layer_norm_rows(x_ref[...], g_ref[...], b_ref[...])
    h_ref[...] = h
    hb_ref[...] = h.astype(BF16)


def _ln_in(x2d, g, b, tm=512):
    n, d = x2d.shape
    tm = min(tm, n)
    return pl.pallas_call(
        _ln_in_body,
        grid=(n // tm,),
        in_specs=[pl.BlockSpec((tm, d), lambda i: (i, 0)),
                  pl.BlockSpec((1, d), lambda i: (0, 0)),
                  pl.BlockSpec((1, d), lambda i: (0, 0))],
        out_specs=[pl.BlockSpec((tm, d), lambda i: (i, 0)),
                   pl.BlockSpec((tm, d), lambda i: (i, 0))],
        out_shape=[jax.ShapeDtypeStruct((n, d), F32), jax.ShapeDtypeStruct((n, d), BF16)],
        compiler_params=_cparams(1),
        name="ln_in",
    )(x2d, _row2(g), _row2(b))


def _mm_body(a_ref, w_ref, o_ref):
    o_ref[...] = _dot(a_ref[...], w_ref[...]).astype(o_ref.dtype)


def _matmul(a, w, tm, tn, out_dtype, name):
    m, k = a.shape
    _, n = w.shape
    return pl.pallas_call(
        _mm_body,
        grid=(n // tn, m // tm),
        in_specs=[pl.BlockSpec((tm, k), lambda j, i: (i, 0)),
                  pl.BlockSpec((k, tn), lambda j, i: (0, j))],
        out_specs=pl.BlockSpec((tm, tn), lambda j, i: (i, j)),
        out_shape=jax.ShapeDtypeStruct((m, n), out_dtype),
        compiler_params=_cparams(2),
        name=name,
    )(a, w)


CONV_HALO = 16


def _conv_shift_matrix(rows):
    t = jnp.arange(rows)[:, None]
    c = jnp.arange(3 * (rows + CONV_HALO))[None, :]
    k, j = c // (rows + CONV_HALO), c % (rows + CONV_HALO)
    return (j == CONV_HALO + t - 3 + k).astype(BF16)


def _conv4_silu(buf, src_ref, w_ref, b_ref, shift_ref, rows):
    x = src_ref[...]
    buf[CONV_HALO:CONV_HALO + rows, :] = x
    hist = buf[...]
    taps = jnp.concatenate([(hist * w_ref[k:k + 1, :]).astype(BF16) for k in range(3)], axis=0)
    acc = b_ref[...] + w_ref[3:4, :] * x + _dot(shift_ref[...], taps)
    buf[0:CONV_HALO, :] = buf[rows:rows + CONV_HALO, :]
    return jax.nn.silu(acc)


def _ssd_body(z_ref, x_ref, bc_ref, sm_ref, cwx_ref, cbx_ref, cwbc_ref, cbbc_ref, dtb_ref,
              alog_ref, dsk_ref, nw_ref, e_ref, tri_ref, shift_ref, o_ref, xbuf, bcbuf, st_ref):
    rows = x_ref.shape[0]
    gw = SSD_GROUP_W
    groups = range(SSD_GROUPS)

    @pl.when(pl.program_id(1) == 0)
    def _():
        xbuf[0:CONV_HALO, :] = jnp.zeros((CONV_HALO, xbuf.shape[1]), F32)
        bcbuf[0:CONV_HALO, :] = jnp.zeros((CONV_HALO, bcbuf.shape[1]), F32)
        st_ref[...] = jnp.zeros(st_ref.shape, F32)

    xs_all = _conv4_silu(xbuf, x_ref, cwx_ref, cbx_ref, shift_ref, rows)
    bc = _conv4_silu(bcbuf, bc_ref, cwbc_ref, cbbc_ref, shift_ref, rows)

    dt = jax.nn.softplus(sm_ref[...] + dtb_ref[...])
    adt = dt * (-jnp.exp(alog_ref[...]))
    a_cs = _dotx_r(tri_ref[...], adt, 3)
    a_cs_t = a_cs.T

    t_row = lax.broadcasted_iota(jnp.int32, (rows, rows), 0)
    t_col = lax.broadcasted_iota(jnp.int32, (rows, rows), 1)
    causal = t_row >= t_col
    lane = lax.broadcasted_iota(jnp.int32, (rows, LANES), 1)

    xs = [xs_all[:, g * gw:(g + 1) * gw] for g in groups]
    bm = [bc[:, g * SSD_STATE:(g + 1) * SSD_STATE].astype(BF16) for g in groups]
    cm = [bc[:, SSD_GN + g * SSD_STATE:SSD_GN + (g + 1) * SSD_STATE].astype(BF16) for g in groups]
    a_exp = [_dotx_l(a_cs, e_ref[g], 3) for g in groups]
    dt_exp = [_dotx_l(dt, e_ref[g], 3) for g in groups]
    scores = [_dot_nt(cm[g], bm[g]) for g in groups]
    st = [st_ref[g] for g in groups]
    y_off = [_dot(cm[g], st[g].astype(BF16)) for g in groups]
    xdt = [xs[g] * dt_exp[g] for g in groups]

    y_diag = []
    for g in groups:
        parts = []
        for pr in range(SSD_HPG // 2):
            xp = xdt[g][:, pr * LANES:(pr + 1) * LANES]
            acc = None
            for s in range(2):
                hh = g * SSD_HPG + 2 * pr + s
                seg = a_cs[:, hh:hh + 1] - a_cs_t[hh:hh + 1, :]
                m = jnp.where(causal, jnp.exp(seg), 0.0) * scores[g]
                head_lanes = (lane < SSD_HEAD_DIM) if s == 0 else (lane >= SSD_HEAD_DIM)
                xm = jnp.where(head_lanes, xp, 0.0)
                t = _dot(m.astype(BF16), xm.astype(BF16))
                acc = t if acc is None else acc + t
            parts.append(acc)
        y_diag.append(jnp.concatenate(parts, axis=1))

    for g in groups:
        cols = slice(g * gw, (g + 1) * gw)
        a_last = a_exp[g][rows - 1:rows, :]
        st_ref[g] = jnp.exp(a_last) * st[g] + _dot_tn(
            bm[g], (xdt[g] * jnp.exp(a_last - a_exp[g])).astype(BF16))
        y = y_diag[g] + y_off[g] * jnp.exp(a_exp[g]) + xs[g] * dsk_ref[:, cols]
        y = y * jax.nn.silu(z_ref[:, cols])
        ms = jnp.mean(y * y, axis=-1, keepdims=True)
        o_ref[:, cols] = (y * lax.rsqrt(ms + SSD_EPS) * nw_ref[:, cols]).astype(o_ref.dtype)


def _ssd_mixer(proj, bsz, seq, conv_w, conv_b, dt_bias, a_log, d_skip, norm_w, tri):
    tb = TIME_BLOCK
    nc = seq // tb
    gw = SSD_GROUP_W
    d = D_MODEL
    bcw = 2 * SSD_GN
    cwx, cwbc = conv_w[:, :d], conv_w[:, d:]
    cbx, cbbc = _row2(conv_b[:d]), _row2(conv_b[d:])
    head_lanes = lambda v: _row2(jnp.pad(v, (0, LANES - SSD_HEADS)))
    grp = jnp.arange(SSD_GROUPS)[:, None, None]
    src = jnp.arange(LANES)[None, :, None]
    dst = jnp.arange(gw)[None, None, :]
    expand = (src == grp * SSD_HPG + dst // SSD_HEAD_DIM).astype(BF16)

    row = lambda b, c: b * nc + c
    full = lambda shape: pl.BlockSpec(shape, lambda b, c: (0,) * len(shape))
    in_specs = [
        pl.BlockSpec((tb, d), lambda b, c: (row(b, c), C_Z // d)),
        pl.BlockSpec((tb, d), lambda b, c: (row(b, c), C_X // d)),
        pl.BlockSpec((tb, bcw), lambda b, c: (row(b, c), C_BC // bcw)),
        pl.BlockSpec((tb, LANES), lambda b, c: (row(b, c), C_SMALL // LANES)),
        full((4, d)), full((1, d)), full((4, bcw)), full((1, bcw)),
        full((1, LANES)), full((1, LANES)), full((1, d)), full((1, d)),
        full((SSD_GROUPS, LANES, gw)), full((tb, tb)), full((tb, 3 * (tb + CONV_HALO))),
    ]
    return pl.pallas_call(
        _ssd_body,
        grid=(bsz, nc),
        in_specs=in_specs,
        out_specs=pl.BlockSpec((tb, d), lambda b, c: (row(b, c), 0)),
        out_shape=jax.ShapeDtypeStruct((bsz * seq, d), BF16),
        scratch_shapes=[pltpu.VMEM((tb + CONV_HALO, d), F32),
                        pltpu.VMEM((tb + CONV_HALO, bcw), F32),
                        pltpu.VMEM((SSD_GROUPS, SSD_STATE, gw), F32)],
        compiler_params=_cparams(2),
        name="ssd_mixer",
    )(proj, proj, proj, proj, cwx, cbx, cwbc, cbbc, head_lanes(dt_bias), head_lanes(a_log),
      _row2(jnp.repeat(d_skip, SSD_HEAD_DIM)), _row2(norm_w), expand, tri, _conv_shift_matrix(tb))


def _mlstm_body(q_ref, k_ref, v_ref, og_ref, sm_ref, sel_ref, cwq_ref, cbq_ref, cwk_ref, cbk_ref,
                gb_ref, nw_ref, tri_ref, shift_ref, o_ref, qbuf, kbuf, ct_ref, n_ref, m_ref):
    rows = q_ref.shape[0]

    @pl.when(pl.program_id(2) == 0)
    def _():
        qbuf[0:CONV_HALO, :] = jnp.zeros((CONV_HALO, qbuf.shape[1]), F32)
        kbuf[0:CONV_HALO, :] = jnp.zeros((CONV_HALO, kbuf.shape[1]), F32)
        ct_ref[...] = jnp.zeros(ct_ref.shape, F32)
        n_ref[...] = jnp.zeros(n_ref.shape, F32)
        m_ref[...] = jnp.zeros(m_ref.shape, F32)

    heads = range(q_ref.shape[1] // ML_DK)
    q_all = _conv4_silu(qbuf, q_ref, cwq_ref, cbq_ref, shift_ref, rows) * (ML_DK ** -0.5)
    k_all = _conv4_silu(kbuf, k_ref, cwk_ref, cbk_ref, shift_ref, rows)
    qf = [q_all[:, i * ML_DK:(i + 1) * ML_DK] for i in heads]
    q = [qf[i].astype(BF16) for i in heads]
    k = [k_all[:, i * ML_DK:(i + 1) * ML_DK] for i in heads]
    kb = [k[i].astype(BF16) for i in heads]
    v = [v_ref[:, i * ML_DV:(i + 1) * ML_DV] for i in heads]

    pre = _dotx_l(sm_ref[...], sel_ref[...], 3) + gb_ref[...]
    log_f = jax.nn.log_sigmoid(pre)
    bcum_all = _dotx_r(tri_ref[...], log_f, 3)
    lane = lax.broadcasted_iota(jnp.int32, (rows, LANES), 1)
    gates_t = jnp.where(lane % 2 == 0, pre, bcum_all).T
    t_row = lax.broadcasted_iota(jnp.int32, (rows, rows), 0)
    t_col = lax.broadcasted_iota(jnp.int32, (rows, rows), 1)
    causal = t_row >= t_col

    li = [pre[:, 2 * i:2 * i + 1] for i in heads]
    bcum = [bcum_all[:, 2 * i + 1:2 * i + 2] for i in heads]
    log_d = [jnp.where(causal, bcum[i] - gates_t[2 * i + 1:2 * i + 2, :] + gates_t[2 * i:2 * i + 1, :],
                       -jnp.inf) for i in heads]
    qk = [_dot_nt(q[i], kb[i]) for i in heads]
    ct = [ct_ref[i] for i in heads]
    inter = [_dot(q[i], ct[i].astype(BF16)) for i in heads]
    m_prev = [m_ref[i, 0:1, 0:1] for i in heads]
    m_inter = [bcum[i] + m_prev[i] for i in heads]
    m_t = [jnp.maximum(jnp.max(log_d[i], axis=-1, keepdims=True), m_inter[i]) for i in heads]
    w = [jnp.exp(log_d[i] - m_t[i]) * qk[i] for i in heads]
    s_inter = [jnp.exp(m_inter[i] - m_t[i]) for i in heads]
    nvec = [n_ref[i, 0:1, :] for i in heads]
    num = [_dot(w[i].astype(BF16), v[i].astype(BF16)) + s_inter[i] * inter[i] for i in heads]
    den = [jnp.sum(w[i], axis=-1, keepdims=True)
           + s_inter[i] * jnp.sum(qf[i] * nvec[i], axis=-1, keepdims=True)
           for i in heads]
    h = [num[i] / jnp.maximum(jnp.abs(den[i]), jnp.exp(-m_t[i])) for i in heads]

    b_tot = [bcum[i][rows - 1:rows, :] for i in heads]
    log_w = [b_tot[i] - bcum[i] + li[i] for i in heads]
    m_new = [jnp.maximum(b_tot[i] + m_prev[i], jnp.max(log_w[i], axis=0, keepdims=True))
             for i in heads]
    w_end = [jnp.exp(log_w[i] - m_new[i]) for i in heads]
    s_old = [jnp.exp(b_tot[i] + m_prev[i] - m_new[i]) for i in heads]
    ct_new = [s_old[i] * ct[i] + _dot_tn(kb[i], (w_end[i] * v[i]).astype(BF16)) for i in heads]
    for i in heads:
        ct_ref[i] = ct_new[i]
        n_ref[i] = jnp.broadcast_to(
            s_old[i] * nvec[i] + jnp.sum(w_end[i] * k[i], axis=0, keepdims=True), n_ref.shape[1:])
        m_ref[i] = jnp.broadcast_to(m_new[i], m_ref.shape[1:])
        hn = h[i] * lax.rsqrt(jnp.mean(h[i] * h[i], axis=-1, keepdims=True) + ML_EPS)
        cols = slice(i * ML_DV, (i + 1) * ML_DV)
        o_ref[:, cols] = (jax.nn.sigmoid(og_ref[:, cols]) * hn * nw_ref[:, cols]).astype(o_ref.dtype)


def _mlstm_mixer(proj, bsz, seq, conv_w, conv_b, i_bias, f_bias, norm_w, tri):
    tb = TIME_BLOCK
    nc = seq // tb
    hg = ML_GROUP
    n_groups = ML_HEADS // hg
    qw, vw = hg * ML_DK, hg * ML_DV
    small_i = SSD_HEADS
    small_f = SSD_HEADS + ML_HEADS
    src = jnp.arange(LANES)[None, :, None]
    dst = jnp.arange(LANES)[None, None, :]
    grp = jnp.arange(n_groups)[:, None, None]
    head = grp * hg + dst // 2
    sel = ((dst < 2 * hg) & (((dst % 2 == 0) & (src == small_i + head))
                             | ((dst % 2 == 1) & (src == small_f + head)))).astype(BF16)
    gate_bias = jnp.stack([i_bias.reshape(n_groups, hg), f_bias.reshape(n_groups, hg)], axis=-1)
    gate_bias = jnp.pad(gate_bias.reshape(n_groups, 1, 2 * hg), ((0, 0), (0, 0), (0, LANES - 2 * hg)))
    conv_b2 = _row2(conv_b)
    kq = n_groups
    row = lambda b, h, c: b * nc + c
    in_specs = [
        pl.BlockSpec((tb, qw), lambda b, h, c: (row(b, h, c), C_QK // qw + h)),
        pl.BlockSpec((tb, qw), lambda b, h, c: (row(b, h, c), C_QK // qw + kq + h)),
        pl.BlockSpec((tb, vw), lambda b, h, c: (row(b, h, c), C_MV // vw + h)),
        pl.BlockSpec((tb, vw), lambda b, h, c: (row(b, h, c), C_MO // vw + h)),
        pl.BlockSpec((tb, LANES), lambda b, h, c: (row(b, h, c), C_SMALL // LANES)),
        pl.BlockSpec((None, LANES, LANES), lambda b, h, c: (h, 0, 0)),
        pl.BlockSpec((4, qw), lambda b, h, c: (0, h)),
        pl.BlockSpec((1, qw), lambda b, h, c: (0, h)),
        pl.BlockSpec((4, qw), lambda b, h, c: (0, kq + h)),
        pl.BlockSpec((1, qw), lambda b, h, c: (0, kq + h)),
        pl.BlockSpec((None, 1, LANES), lambda b, h, c: (h, 0, 0)),
        pl.BlockSpec((1, vw), lambda b, h, c: (0, h)),
        pl.BlockSpec((tb, tb), lambda b, h, c: (0, 0)),
        pl.BlockSpec((tb, 3 * (tb + CONV_HALO)), lambda b, h, c: (0, 0)),
    ]
    return pl.pallas_call(
        _mlstm_body,
        grid=(bsz, n_groups, nc),
        in_specs=in_specs,
        out_specs=pl.BlockSpec((tb, vw), lambda b, h, c: (row(b, h, c), h)),
        out_shape=jax.ShapeDtypeStruct((bsz * seq, D_MODEL), BF16),
        scratch_shapes=[pltpu.VMEM((tb + CONV_HALO, qw), F32),
                        pltpu.VMEM((tb + CONV_HALO, qw), F32),
                        pltpu.VMEM((hg, ML_DK, ML_DV), F32),
                        pltpu.VMEM((hg, HALO, LANES), F32),
                        pltpu.VMEM((hg, HALO, LANES), F32)],
        compiler_params=_cparams(3),
        name="mlstm_mixer",
    )(proj, proj, proj, proj, proj, sel, conv_w, conv_b2, conv_w, conv_b2, gate_bias,
      _row2(norm_w), tri, _conv_shift_matrix(tb))


def _stack_heads(x):
    lane = lax.broadcasted_iota(jnp.int32, x.shape, 1)
    return jnp.concatenate([jnp.where(lane < RWKV_HEAD_DIM, x, 0.0),
                            jnp.where(lane >= RWKV_HEAD_DIM, x, 0.0)], axis=0)


def _shift_lerp(buf, src_ref, mu_ref, rows):
    buf[HALO:HALO + rows, :] = src_ref[...]
    cur = buf[HALO:HALO + rows, :]
    prev = buf[HALO - 1:HALO - 1 + rows, :]
    buf[0:HALO, :] = buf[rows:rows + HALO, :]
    return cur + (prev - cur) * mu_ref[...]


RWKV_GROUP = 8
RWKV_TIME_BLOCK = 256


def _rwkv_body(r_ref, k_ref, v_ref, lo_ref, mur_ref, muk_ref, muv_ref, mulo_ref, w0_ref, w2_ref,
               a0_ref, a2_ref, g2_ref, kk_ref, ka_ref, rk_ref, lnw_ref, lnb_ref, tri_ref,
               ones_ref, o_ref, rbuf, kbuf, vbuf, lobuf, z_ref, y_buf):
    rows = r_ref.shape[0]
    n_pairs = r_ref.shape[1] // LANES
    ch = RWKV_CHUNK
    hd = RWKV_HEAD_DIM

    @pl.when(pl.program_id(2) == 0)
    def _():
        for buf in (rbuf, kbuf, vbuf, lobuf):
            buf[0:HALO, :] = jnp.zeros((HALO, buf.shape[1]), F32)
        z_ref[...] = jnp.zeros(z_ref.shape, F32)

    r = _shift_lerp(rbuf, r_ref, mur_ref, rows)
    k = _shift_lerp(kbuf, k_ref, muk_ref, rows)
    v = _shift_lerp(vbuf, v_ref, muv_ref, rows)
    lo = _shift_lerp(lobuf, lo_ref, mulo_ref, rows)
    w_lo, a_lo, g_lo = lo[:, 0:LANES], lo[:, LANES:2 * LANES], lo[:, 2 * LANES:]

    log_w = -jax.nn.softplus(-(w0_ref[...] + _mm(jnp.tanh(w_lo), w2_ref[...], 3))) - 0.5
    lw = -jnp.exp(log_w)
    a = jax.nn.sigmoid(a0_ref[...] + _mm(a_lo, a2_ref[...], 3))
    gate = _mm(jax.nn.sigmoid(g_lo), g2_ref[...], 1)

    head_sum = ones_ref[...]

    def head_total(x):
        return jnp.concatenate([_dotx_l(x[:, p * LANES:(p + 1) * LANES], head_sum, 2)
                                for p in range(n_pairs)], axis=1)

    kk = k * kk_ref[...]
    kk = kk * lax.rsqrt(jnp.maximum(head_total(kk * kk), 1e-24))
    kh = k * (1.0 + (a - 1.0) * ka_ref[...])
    kb = kk * a
    cum = _dotx_r(tri_ref[...], lw, 3)

    c_row = lax.broadcasted_iota(jnp.int32, (ch, LANES), 0)
    c_col = lax.broadcasted_iota(jnp.int32, (ch, LANES), 1) % hd
    strict = c_col < c_row
    incl = c_col <= c_row
    same_sub = (c_col // RWKV_SUB) == (c_row // RWKV_SUB)
    eye_pair = (c_col == c_row).astype(F32)
    s_row = lax.broadcasted_iota(jnp.int32, (LANES, LANES), 0)
    s_col = lax.broadcasted_iota(jnp.int32, (LANES, LANES), 1)
    same_head = (s_row // hd) == (s_col // hd)
    eye_full = (s_row == s_col).astype(F32)

    n_chunks = rows // ch
    cat = jnp.concatenate
    mm = functools.partial(_mm, passes=1)
    a_hat, r_hat, k_til, b_til, k_end, b_end, vp, decay_end = [], [], [], [], [], [], [], []
    for j in range(n_chunks):
        sl = slice(j * ch, (j + 1) * ch)
        cm, lwj, kkj, khj, kbj, rj, vj = cum[sl], lw[sl], kk[sl], kh[sl], kb[sl], r[sl], v[sl]
        cm_end = cm[ch - 1:ch, :]
        e_neg = jnp.exp(-cm)
        e_end = jnp.exp(cm_end - cm)
        wide = (-kkj * jnp.exp(cm - lwj), rj * jnp.exp(cm), khj * e_neg, kbj * e_neg,
                khj * e_end, kbj * e_end, vj, jnp.exp(cm_end))
        for pi in range(n_pairs):
            ls = slice(pi * LANES, (pi + 1) * LANES)
            for dst, src in zip((a_hat, r_hat, k_til, b_til, k_end, b_end, vp, decay_end), wide):
                dst.append(src[:, ls])
    units = range(n_chunks * n_pairs)
    each = lambda fn: [fn(u) for u in units]

    big = each(lambda u: mm(cat([a_hat[u], r_hat[u]], axis=0),
                            cat([_stack_heads(k_til[u]), _stack_heads(b_til[u])], axis=0),
                            dot=_dot_nt))
    l_ak = each(lambda u: jnp.where(strict, big[u][0:ch, 0:LANES], 0.0))
    l_ab = each(lambda u: jnp.where(strict, big[u][0:ch, LANES:], 0.0))
    m_rk = each(lambda u: jnp.where(incl, big[u][ch:, 0:LANES], 0.0))
    m_rb = each(lambda u: jnp.where(incl, big[u][ch:, LANES:], 0.0))

    l_d = each(lambda u: jnp.where(same_sub, l_ab[u], 0.0))
    l_o = each(lambda u: l_ab[u] - l_d[u])
    x = each(lambda u: eye_pair + l_d[u])
    p = each(lambda u: mm(l_d[u], _stack_heads(l_d[u])))
    for _ in range(2):
        both = each(lambda u: mm(cat([p[u], x[u]], axis=0), _stack_heads(p[u])))
        x = each(lambda u: x[u] + both[u][ch:])
        p = each(lambda u: both[u][:ch])
    t_d = each(lambda u: x[u] + mm(x[u], _stack_heads(p[u])))
    nn = each(lambda u: mm(t_d[u], _stack_heads(l_o[u])))
    n2 = each(lambda u: mm(nn[u], _stack_heads(nn[u])))
    mx = each(lambda u: eye_pair + nn[u])
    mx = each(lambda u: mx[u] + mm(mx[u], _stack_heads(n2[u])))
    t = each(lambda u: mm(mx[u], _stack_heads(t_d[u])))

    lmv = each(lambda u: mm(cat([l_ak[u], m_rk[u]], axis=0), _stack_heads(vp[u])))
    wu = each(lambda u: mm(t[u], cat([_stack_heads(a_hat[u]), _stack_heads(lmv[u][:ch])], axis=1)))
    w_a = each(lambda u: wu[u][:, 0:LANES])
    u_v = each(lambda u: wu[u][:, LANES:])
    mw = each(lambda u: mm(m_rb[u], cat([_stack_heads(w_a[u]), _stack_heads(u_v[u])], axis=1)))
    q_eff = each(lambda u: r_hat[u] + mw[u][:, 0:LANES])
    y_loc = each(lambda u: lmv[u][ch:] + mw[u][:, LANES:])
    a_eff = each(lambda u: jnp.where(same_head, mm(b_end[u], w_a[u], dot=_dot_tn), 0.0)
                 + eye_full * decay_end[u])
    z_loc = each(lambda u: jnp.where(
        same_head, mm(cat([k_end[u], b_end[u]], axis=0), cat([vp[u], u_v[u]], axis=0),
                      dot=_dot_tn), 0.0))

    z = [z_ref[pi] for pi in range(n_pairs)]
    for j in range(n_chunks):
        sl = slice(j * ch, (j + 1) * ch)
        us = [j * n_pairs + pi for pi in range(n_pairs)]
        qz = [mm(cat([q_eff[u], a_eff[u]], axis=0), z[pi]) for pi, u in enumerate(us)]
        for pi, u in enumerate(us):
            y_buf[sl, pi * LANES:(pi + 1) * LANES] = qz[pi][:ch] + y_loc[u]
        z = [qz[pi][ch:] + z_loc[u] for pi, u in enumerate(us)]
    for pi in range(n_pairs):
        z_ref[pi] = z[pi]

    y = y_buf[...]
    mean = head_total(y) * (1.0 / hd)
    yc = y - mean
    var = head_total(yc * yc) * (1.0 / hd)
    yn = yc * lax.rsqrt(var + RWKV_LN_EPS) * lnw_ref[...] + lnb_ref[...]
    bonus = head_total(r * kh * rk_ref[...]) * v
    o_ref[...] = ((yn + bonus) * gate).astype(o_ref.dtype)


def _rwkv_mixer(proj, bsz, seq, mu, w0, w2, a0, a2, g2, k_k, k_a, r_k, ln_w, ln_b):
    tb = RWKV_TIME_BLOCK
    nc = seq // tb
    d = D_MODEL
    lora_w = 2 * LANES + RWKV_LORA_G
    mur, muk, muv = _row2(mu[0:d]), _row2(mu[d:2 * d]), _row2(mu[2 * d:3 * d])
    o = 3 * d
    pad_l = lambda x, n: jnp.pad(x, (0, n - x.shape[0]))
    mulo = _row2(jnp.concatenate([pad_l(mu[o:o + RWKV_LORA_W], LANES),
                                  pad_l(mu[o + RWKV_LORA_W:o + RWKV_LORA_W + RWKV_LORA_A], LANES),
                                  mu[o + RWKV_LORA_W + RWKV_LORA_A:]]))
    w2p = jnp.pad(w2, ((0, LANES - RWKV_LORA_W), (0, 0)))
    a2p = jnp.pad(a2, ((0, LANES - RWKV_LORA_A), (0, 0)))
    ch = RWKV_CHUNK
    t_i = jnp.arange(tb)[:, None]
    t_j = jnp.arange(tb)[None, :]
    tri = ((t_i >= t_j) & (t_i // ch == t_j // ch)).astype(BF16)
    l_i = jnp.arange(LANES)[:, None]
    l_j = jnp.arange(LANES)[None, :]
    head_sum = (l_i // RWKV_HEAD_DIM == l_j // RWKV_HEAD_DIM).astype(BF16)

    gw = RWKV_GROUP * LANES
    row = lambda b, p, c: b * nc + c
    col = lambda base: (lambda b, p, c: (row(b, p, c), base // gw + p))
    vec = pl.BlockSpec((1, gw), lambda b, p, c: (0, p))
    in_specs = [
        pl.BlockSpec((tb, gw), col(C_R)),
        pl.BlockSpec((tb, gw), col(C_K)),
        pl.BlockSpec((tb, gw), col(C_V)),
        pl.BlockSpec((tb, lora_w), lambda b, p, c: (row(b, p, c), C_LORA // lora_w)),
        vec, vec, vec,
        pl.BlockSpec((1, lora_w), lambda b, p, c: (0, 0)),
        vec,
        pl.BlockSpec((LANES, gw), lambda b, p, c: (0, p)),
        vec,
        pl.BlockSpec((LANES, gw), lambda b, p, c: (0, p)),
        pl.BlockSpec((RWKV_LORA_G, gw), lambda b, p, c: (0, p)),
        vec, vec, vec, vec, vec,
        pl.BlockSpec((tb, tb), lambda b, p, c: (0, 0)),
        pl.BlockSpec((LANES, LANES), lambda b, p, c: (0, 0)),
    ]
    return pl.pallas_call(
        _rwkv_body,
        grid=(bsz, RWKV_PAIRS // RWKV_GROUP, nc),
        in_specs=in_specs,
        out_specs=pl.BlockSpec((tb, gw), lambda b, p, c: (row(b, p, c), p)),
        out_shape=jax.ShapeDtypeStruct((bsz * seq, D_MODEL), BF16),
        scratch_shapes=[pltpu.VMEM((tb + HALO, gw), F32),
                        pltpu.VMEM((tb + HALO, gw), F32),
                        pltpu.VMEM((tb + HALO, gw), F32),
                        pltpu.VMEM((tb + HALO, lora_w), F32),
                        pltpu.VMEM((RWKV_GROUP, LANES, LANES), F32),
                        pltpu.VMEM((tb, gw), F32)],
        compiler_params=_cparams(3),
        name="rwkv_mixer",
    )(proj, proj, proj, proj, mur, muk, muv, mulo, _row2(w0), w2p, _row2(a0), a2p, g2,
      _row2(k_k), _row2(k_a), _row2(r_k.reshape(-1)), _row2(ln_w), _row2(ln_b), tri, head_sum)


def _gated_merge_body(ys_ref, yr_ref, ym_ref, ps_ref, pr_ref, pm_ref, gs_ref, gr_ref, gm_ref, o_ref):
    acc = jax.nn.sigmoid(gs_ref[...]) * _dot(ys_ref[...], ps_ref[...])
    acc = acc + jax.nn.sigmoid(gr_ref[...]) * _dot(yr_ref[...], pr_ref[...])
    acc = acc + jax.nn.sigmoid(gm_ref[...]) * _dot(ym_ref[...], pm_ref[...])
    o_ref[...] = acc.astype(o_ref.dtype)


def _gated_merge(ys, ps, proj, tm=512, tn=1024):
    m, k = ys[0].shape
    tm = min(tm, m)
    n = ps[0].shape[1]
    y_spec = pl.BlockSpec((tm, k), lambda j, i: (i, 0))
    p_spec = pl.BlockSpec((k, tn), lambda j, i: (0, j), pipeline_mode=pl.Buffered(1))
    gate_spec = lambda col: pl.BlockSpec((tm, tn), lambda j, i: (i, col // tn + j))
    return pl.pallas_call(
        _gated_merge_body,
        grid=(n // tn, m // tm),
        in_specs=[y_spec] * 3 + [p_spec] * 3 + [gate_spec(C_GS), gate_spec(C_GR), gate_spec(C_GM)],
        out_specs=pl.BlockSpec((tm, tn), lambda j, i: (i, j)),
        out_shape=jax.ShapeDtypeStruct((m, n), BF16),
        compiler_params=_cparams(2),
        name="gated_merge",
    )(*ys, *ps, proj, proj, proj)


def _mm_res_ln_body(a_ref, w_ref, h_ref, g_ref, b_ref, o_ref, ob_ref, *, row_split):
    n = a_ref.shape[0] // row_split
    subs = [slice(s * n, (s + 1) * n) for s in range(row_split)]
    acc = [_dot(a_ref[rows, :], w_ref[...]) for rows in subs]
    for rows, mix in zip(subs, acc):
        h = _layer_norm_rows(ALPHA * h_ref[rows, :] + mix, g_ref[...], b_ref[...])
        o_ref[rows, :] = h
        ob_ref[rows, :] = h.astype(BF16)


def _matmul_res_ln(a, w, h, g, b, tm, row_split):
    m, k = a.shape
    n = w.shape[1]
    const = lambda shape: pl.BlockSpec(shape, lambda i: (0, 0), pipeline_mode=pl.Buffered(1))
    return pl.pallas_call(
        functools.partial(_mm_res_ln_body, row_split=row_split),
        grid=(m // tm,),
        in_specs=[pl.BlockSpec((tm, k), lambda i: (i, 0)),
                  const((k, n)),
                  pl.BlockSpec((tm, n), lambda i: (i, 0)),
                  const((1, n)), const((1, n))],
        out_specs=[pl.BlockSpec((tm, n), lambda i: (i, 0)),
                   pl.BlockSpec((tm, n), lambda i: (i, 0))],
        out_shape=[jax.ShapeDtypeStruct((m, n), F32), jax.ShapeDtypeStruct((m, n), BF16)],
        compiler_params=_cparams(1),
        name="matmul_res_ln",
    )(a, w, h, _row2(g), _row2(b))


FFN_HALO = 16


def _ffn_up_body(h_ref, hp_ref, wg_ref, wua_ref, wub_ref, cwg_ref, cbg_ref, cwu_ref, cbu_ref, o_ref,
                 gbuf, ubuf, wg_s, wu_s, *, blocks_per_seq):
    rows = h_ref.shape[0]
    tn = o_ref.shape[1]
    first = (pl.program_id(1) % blocks_per_seq) == 0
    keep = jnp.where(first, 0.0, 1.0)

    @pl.when(pl.program_id(1) == 0)
    def _():
        col = pl.program_id(0) * tn + lax.broadcasted_iota(jnp.int32, wg_s.shape, 1)
        off = FFN_D % tn
        up = jnp.concatenate([wua_ref[:, off:], wub_ref[:, :off]], axis=1)
        wg_s[...] = jnp.where(col < FFN_D, wg_ref[...], 0.0).astype(BF16)
        wu_s[...] = jnp.where(col < FFN_D, up, 0.0).astype(BF16)

    def conv(buf, cw_ref, cb_ref):
        acc = cb_ref[...] + cw_ref[2:3, :] * buf[FFN_HALO:FFN_HALO + rows, :]
        acc = acc + cw_ref[1:2, :] * buf[FFN_HALO - 1:FFN_HALO - 1 + rows, :]
        return acc + cw_ref[0:1, :] * buf[FFN_HALO - 2:FFN_HALO - 2 + rows, :]

    for buf, w_ref in ((gbuf, wg_s), (ubuf, wu_s)):
        buf[0:FFN_HALO, :] = _dot(hp_ref[...], w_ref[...]) * keep
        buf[FFN_HALO:FFN_HALO + rows, :] = _dot(h_ref[...], w_ref[...])
    gate = conv(gbuf, cwg_ref, cbg_ref)
    up = conv(ubuf, cwu_ref, cbu_ref)
    gelu = 0.5 * gate * (1.0 + lax.erf(gate * (2.0 ** -0.5)))
    o_ref[...] = (gelu * up).astype(o_ref.dtype)


def _ffn_up(hb, w_up, layer, cwg, cbg, cwu, cbu, seq, tm=1024):
    m, k = hb.shape
    n = FFN_PAD
    tn = FFN_TN
    halo_blocks = tm // FFN_HALO
    up0 = FFN_D // tn
    body = functools.partial(_ffn_up_body, blocks_per_seq=seq // tm)
    w_spec = lambda first_block: pl.BlockSpec(
        (None, k, tn), lambda j, i: (layer, 0, first_block + j), pipeline_mode=pl.Buffered(1))
    return pl.pallas_call(
        body,
        grid=(n // tn, m // tm),
        in_specs=[pl.BlockSpec((tm, k), lambda j, i: (i, 0)),
                  pl.BlockSpec((FFN_HALO, k), lambda j, i: (jnp.maximum(i * halo_blocks - 1, 0), 0)),
                  w_spec(0), w_spec(up0), w_spec(up0 + 1),
                  pl.BlockSpec((3, tn), lambda j, i: (0, j)),
                  pl.BlockSpec((1, tn), lambda j, i: (0, j)),
                  pl.BlockSpec((3, tn), lambda j, i: (0, j)),
                  pl.BlockSpec((1, tn), lambda j, i: (0, j))],
        out_specs=pl.BlockSpec((tm, tn), lambda j, i: (i, j)),
        out_shape=jax.ShapeDtypeStruct((m, n), BF16),
        scratch_shapes=[pltpu.VMEM((tm + FFN_HALO, tn), F32),
                        pltpu.VMEM((tm + FFN_HALO, tn), F32),
                        pltpu.VMEM((k, tn), BF16),
                        pltpu.VMEM((k, tn), BF16)],
        compiler_params=_cparams(2),
        name="ffn_up",
    )(hb, hb, w_up, w_up, w_up, cwg, cbg, cwu, cbu)


def _pack_w_in(w):
    d = D_MODEL
    o = 0

    def take(n):
        nonlocal o
        s = w[:, o:o + n]
        o += n
        return s

    z = take(d)
    x = take(d)
    bc = take(2 * SSD_GN)
    dt = take(SSD_HEADS)
    r, k, v = take(d), take(d), take(d)
    w_lo, a_lo, g_lo = take(RWKV_LORA_W), take(RWKV_LORA_A), take(RWKV_LORA_G)
    qk, mv, mo = take(d), take(d), take(d)
    gi, gf = take(ML_HEADS), take(ML_HEADS)
    gs, gr, gm = take(d), take(d), take(d)
    zeros = lambda n: jnp.zeros((d, n), w.dtype)
    small = jnp.concatenate([dt, gi, gf, zeros(LANES - SSD_HEADS - 2 * ML_HEADS)], axis=1)
    lora = jnp.concatenate([w_lo, zeros(LANES - RWKV_LORA_W), a_lo, zeros(LANES - RWKV_LORA_A), g_lo],
                           axis=1)
    packed = jnp.concatenate([z, x, r, k, v, qk, mv, mo, gs, gr, gm, bc, lora, small], axis=1)
    packed = jnp.concatenate([packed, zeros(N_PROJ - packed.shape[1])], axis=1)
    return packed.astype(BF16)


def _pad_cols(x, n):
    return jnp.pad(x, ((0, 0), (0, n - x.shape[1])))


def kernel(x, ln_in_g, ln_in_b, w_in, ssd_conv_w, ssd_conv_b, ssd_dt_bias, ssd_a_log, ssd_d, ssd_norm_w, rwkv_mu, rwkv_w0, rwkv_w2, rwkv_a0, rwkv_a2, rwkv_g2, rwkv_k_k, rwkv_k_a, rwkv_r_k, rwkv_ln_w, rwkv_ln_b, mlstm_conv_w, mlstm_conv_b, mlstm_i_bias, mlstm_f_bias, mlstm_norm_w, proj_ssd, proj_rwkv, proj_mlstm, w_out, ln1_g, ln1_b, ffn_w_up, ffn_conv_w, ffn_conv_b, ffn_w_down, ln2_g, ln2_b):
    bsz, seq, d = x.shape
    n_tok = bsz * seq
    tb = TIME_BLOCK
    t_i = jnp.arange(tb)[:, None]
    t_j = jnp.arange(tb)[None, :]
    tri = (t_i >= t_j).astype(BF16)

    h, hb = _ln_in(x.reshape(n_tok, d), ln_in_g, ln_in_b)
    for l in range(DEPTH):
        proj = _matmul(hb, _pack_w_in(w_in[l]), tm=min(1024, n_tok), tn=2048, out_dtype=F32,
                       name="in_proj")
        y_ssd = _ssd_mixer(proj, bsz, seq, ssd_conv_w[l], ssd_conv_b[l], ssd_dt_bias[l],
                           ssd_a_log[l], ssd_d[l], ssd_norm_w[l], tri)
        y_rwkv = _rwkv_mixer(proj, bsz, seq, rwkv_mu[l], rwkv_w0[l], rwkv_w2[l], rwkv_a0[l],
                             rwkv_a2[l], rwkv_g2[l], rwkv_k_k[l], rwkv_k_a[l], rwkv_r_k[l],
                             rwkv_ln_w[l], rwkv_ln_b[l])
        y_mlstm = _mlstm_mixer(proj, bsz, seq, mlstm_conv_w[l], mlstm_conv_b[l], mlstm_i_bias[l],
                               mlstm_f_bias[l], mlstm_norm_w[l], tri)
        merged = _gated_merge(
            (y_ssd, y_rwkv, y_mlstm),
            (proj_ssd[l].astype(BF16), proj_rwkv[l].astype(BF16), proj_mlstm[l].astype(BF16)), proj)
        h, hb = _matmul_res_ln(merged, w_out[l].astype(BF16), h, ln1_g[l], ln1_b[l],
                               tm=min(512, n_tok), row_split=2)

        cwg = _pad_cols(ffn_conv_w[l][:, :FFN_D], FFN_PAD)
        cwu = _pad_cols(ffn_conv_w[l][:, FFN_D:], FFN_PAD)
        cbg = _pad_cols(_row2(ffn_conv_b[l][:FFN_D]), FFN_PAD)
        cbu = _pad_cols(_row2(ffn_conv_b[l][FFN_D:]), FFN_PAD)
        act = _ffn_up(hb, ffn_w_up, l, cwg, cbg, cwu, cbu, seq, tm=min(1024, seq))
        wd = jnp.pad(ffn_w_down[l], ((0, FFN_PAD - FFN_D), (0, 0))).astype(BF16)
        h, hb = _matmul_res_ln(act, wd, h, ln2_g[l], ln2_b[l], tm=min(256, n_tok), row_split=1)
    return h.reshape(bsz, seq, d)
```

```python
import functools
import math

import jax
import jax.numpy as jnp
from jax import lax
from jax.experimental import pallas as pl
from jax.experimental.pallas import tpu as pltpu

F32 = jnp.float32
BF16 = jnp.bfloat16

D_MODEL = 2048
DEPTH = 2
ALPHA = (2.0 * DEPTH) ** 0.25
LN_EPS = 1e-5

SSD_HEADS = 32
SSD_HEAD_DIM = 64
SSD_GROUPS = 4
SSD_STATE = 128
SSD_GN = SSD_GROUPS * SSD_STATE
SSD_EPS = 1e-5
SSD_GROUP_W = D_MODEL // SSD_GROUPS
SSD_HPG = SSD_HEADS // SSD_GROUPS

RWKV_HEAD_DIM = 64
RWKV_PAIRS = D_MODEL // 128
RWKV_LORA_W = 96
RWKV_LORA_A = 96
RWKV_LORA_G = 256
RWKV_LN_EPS = 64e-5
RWKV_CHUNK = 64
RWKV_SUB = 16

ML_HEADS = 8
ML_DK = 128
ML_DV = 256
ML_EPS = 1e-6
ML_GROUP = 8

FFN_D = 5504
FFN_PAD = 5632
FFN_TN = 512

LANES = 128
HALO = 8
TIME_BLOCK = 128

C_Z = 0
C_X = 2048
C_R = 4096
C_K = 6144
C_V = 8192
C_QK = 10240
C_MV = 12288
C_MO = 14336
C_GS = 16384
C_GR = 18432
C_GM = 20480
C_BC = 22528
C_LORA = 23552
C_SMALL = 24064
N_PROJ = 24576

VMEM_LIMIT = 56 * 1024 * 1024


def _dot(a, b):
    return jnp.dot(a, b, preferred_element_type=F32)


def _dot_nt(a, b):
    return lax.dot_general(a, b, (((1,), (1,)), ((), ())), preferred_element_type=F32)


def _dot_tn(a, b):
    return lax.dot_general(a, b, (((0,), (0,)), ((), ())), preferred_element_type=F32)


def _split(x, n):
    parts, r = [], x
    for i in range(n):
        p = r.astype(BF16)
        parts.append(p)
        if i + 1 < n:
            r = r - p.astype(F32)
    return parts


def _dotx_l(x, w_exact, n, dot=_dot):
    acc = None
    for p in _split(x, n):
        t = dot(p, w_exact)
        acc = t if acc is None else acc + t
    return acc


def _dotx_r(w_exact, x, n, dot=_dot):
    acc = None
    for p in _split(x, n):
        t = dot(w_exact, p)
        acc = t if acc is None else acc + t
    return acc


def _mm(a, b, passes, dot=_dot):
    if passes == 1:
        return dot(a.astype(BF16), b.astype(BF16))
    ah, al = _split(a, 2)
    bh, bl = _split(b, 2)
    return dot(ah, bh) + (dot(ah, bl) + dot(al, bh))


def _layer_norm_rows(x, g, b):
    mu = jnp.mean(x, axis=-1, keepdims=True)
    xc = x - mu
    var = jnp.mean(xc * xc, axis=-1, keepdims=True)
    return xc * lax.rsqrt(var + LN_EPS) * g + b


def _cparams(n_axes):
    return pltpu.CompilerParams(dimension_semantics=("arbitrary",) * n_axes,
                                vmem_limit_bytes=VMEM_LIMIT)


def _row2(v):
    return v.reshape(1, -1)


def _ln_in_body(x_ref, g_ref, b_ref, h_ref, hb_ref):
    h = _layer_norm_rows(x_ref[...], g_ref[...], b_ref[...])
    h_ref[...] = h
    hb_ref[...] = h.astype(BF16)


def _ln_in(x2d, g, b, tm=512):
    n, d = x2d.shape
    tm = min(tm, n)
    return pl.pallas_call(
        _ln_in_body,
        grid=(n // tm,),
        in_specs=[pl.BlockSpec((tm, d), lambda i: (i, 0)),
                  pl.BlockSpec((1, d), lambda i: (0, 0)),
                  pl.BlockSpec((1, d), lambda i: (0, 0))],
        out_specs=[pl.BlockSpec((tm, d), lambda i: (i, 0)),
                   pl.BlockSpec((tm, d), lambda i: (i, 0))],
        out_shape=[jax.ShapeDtypeStruct((n, d), F32), jax.ShapeDtypeStruct((n, d), BF16)],
        compiler_params=_cparams(1),
        name="ln_in",
    )(x2d, _row2(g), _row2(b))


def _mm_body(a_ref, w_ref, o_ref):
    o_ref[...] = _dot(a_ref[...], w_ref[...]).astype(o_ref.dtype)


def _matmul(a, w, tm, tn, out_dtype, name):
    m, k = a.shape
    _, n = w.shape
    return pl.pallas_call(
        _mm_body,
        grid=(n // tn, m // tm),
        in_specs=[pl.BlockSpec((tm, k), lambda j, i: (i, 0)),
                  pl.BlockSpec((k, tn), lambda j, i: (0, j))],
        out_specs=pl.BlockSpec((tm, tn), lambda j, i: (i, j)),
        out_shape=jax.ShapeDtypeStruct((m, n), out_dtype),
        compiler_params=_cparams(2),
        name=name,
    )(a, w)


CONV_HALO = 16


def _conv_shift_matrix(rows):
    t = jnp.arange(rows)[:, None]
    c = jnp.arange(3 * (rows + CONV_HALO))[None, :]
    k, j = c // (rows + CONV_HALO), c % (rows + CONV_HALO)
    return (j == CONV_HALO + t - 3 + k).astype(BF16)


def _conv4_silu(buf, src_ref, w_ref, b_ref, shift_ref, rows):
    x = src_ref[...]
    buf[CONV_HALO:CONV_HALO + rows, :] = x
    hist = buf[...]
    taps = jnp.concatenate([(hist * w_ref[k:k + 1, :]).astype(BF16) for k in range(3)], axis=0)
    acc = b_ref[...] + w_ref[3:4, :] * x + _dot(shift_ref[...], taps)
    buf[0:CONV_HALO, :] = buf[rows:rows + CONV_HALO, :]
    return jax.nn.silu(acc)


def _ssd_body(z_ref, x_ref, bc_ref, sm_ref, cwx_ref, cbx_ref, cwbc_ref, cbbc_ref, dtb_ref,
              alog_ref, dsk_ref, nw_ref, e_ref, tri_ref, shift_ref, o_ref, xbuf, bcbuf, st_ref):
    rows = x_ref.shape[0]
    gw = SSD_GROUP_W
    groups = range(SSD_GROUPS)

    @pl.when(pl.program_id(1) == 0)
    def _():
        xbuf[0:CONV_HALO, :] = jnp.zeros((CONV_HALO, xbuf.shape[1]), F32)
        bcbuf[0:CONV_HALO, :] = jnp.zeros((CONV_HALO, bcbuf.shape[1]), F32)
        st_ref[...] = jnp.zeros(st_ref.shape, F32)

    xs_all = _conv4_silu(xbuf, x_ref, cwx_ref, cbx_ref, shift_ref, rows)
    bc = _conv4_silu(bcbuf, bc_ref, cwbc_ref, cbbc_ref, shift_ref, rows)

    dt = jax.nn.softplus(sm_ref[...] + dtb_ref[...])
    adt = dt * (-jnp.exp(alog_ref[...]))
    a_cs = _dotx_r(tri_ref[...], adt, 3)
    a_cs_t = a_cs.T

    t_row = lax.broadcasted_iota(jnp.int32, (rows, rows), 0)
    t_col = lax.broadcasted_iota(jnp.int32, (rows, rows), 1)
    causal = t_row >= t_col
    lane = lax.broadcasted_iota(jnp.int32, (rows, LANES), 1)

    xs = [xs_all[:, g * gw:(g + 1) * gw] for g in groups]
    bm = [bc[:, g * SSD_STATE:(g + 1) * SSD_STATE].astype(BF16) for g in groups]
    cm = [bc[:, SSD_GN + g * SSD_STATE:SSD_GN + (g + 1) * SSD_STATE].astype(BF16) for g in groups]
    a_exp = [_dotx_l(a_cs, e_ref[g], 3) for g in groups]
    dt_exp = [_dotx_l(dt, e_ref[g], 3) for g in groups]
    scores = [_dot_nt(cm[g], bm[g]) for g in groups]
    st = [st_ref[g] for g in groups]
    y_off = [_dot(cm[g], st[g].astype(BF16)) for g in groups]
    xdt = [xs[g] * dt_exp[g] for g in groups]

    y_diag = []
    for g in groups:
        parts = []
        for pr in range(SSD_HPG // 2):
            xp = xdt[g][:, pr * LANES:(pr + 1) * LANES]
            acc = None
            for s in range(2):
                hh = g * SSD_HPG + 2 * pr + s
                seg = a_cs[:, hh:hh + 1] - a_cs_t[hh:hh + 1, :]
                m = jnp.where(causal, jnp.exp(seg), 0.0) * scores[g]
                head_lanes = (lane < SSD_HEAD_DIM) if s == 0 else (lane >= SSD_HEAD_DIM)
                xm = jnp.where(head_lanes, xp, 0.0)
                t = _dot(m.astype(BF16), xm.astype(BF16))
                acc = t if acc is None else acc + t
            parts.append(acc)
        y_diag.append(jnp.concatenate(parts, axis=1))

    for g in groups:
        cols = slice(g * gw, (g + 1) * gw)
        a_last = a_exp[g][rows - 1:rows, :]
        st_ref[g] = jnp.exp(a_last) * st[g] + _dot_tn(
            bm[g], (xdt[g] * jnp.exp(a_last - a_exp[g])).astype(BF16))
        y = y_diag[g] + y_off[g] * jnp.exp(a_exp[g]) + xs[g] * dsk_ref[:, cols]
        y = y * jax.nn.silu(z_ref[:, cols])
        ms = jnp.mean(y * y, axis=-1, keepdims=True)
        o_ref[:, cols] = (y * lax.rsqrt(ms + SSD_EPS) * nw_ref[:, cols]).astype(o_ref.dtype)


def _ssd_mixer(proj, bsz, seq, conv_w, conv_b, dt_bias, a_log, d_skip, norm_w, tri):
    tb = TIME_BLOCK
    nc = seq // tb
    gw = SSD_GROUP_W
    d = D_MODEL
    bcw = 2 * SSD_GN
    cwx, cwbc = conv_w[:, :d], conv_w[:, d:]
    cbx, cbbc = _row2(conv_b[:d]), _row2(conv_b[d:])
    head_lanes = lambda v: _row2(jnp.pad(v, (0, LANES - SSD_HEADS)))
    grp = jnp.arange(SSD_GROUPS)[:, None, None]
    src = jnp.arange(LANES)[None, :, None]
    dst = jnp.arange(gw)[None, None, :]
    expand = (src == grp * SSD_HPG + dst // SSD_HEAD_DIM).astype(BF16)

    row = lambda b, c: b * nc + c
    full = lambda shape: pl.BlockSpec(shape, lambda b, c: (0,) * len(shape))
    in_specs = [
        pl.BlockSpec((tb, d), lambda b, c: (row(b, c), C_Z // d)),
        pl.BlockSpec((tb, d), lambda b, c: (row(b, c), C_X // d)),
        pl.BlockSpec((tb, bcw), lambda b, c: (row(b, c), C_BC // bcw)),
        pl.BlockSpec((tb, LANES), lambda b, c: (row(b, c), C_SMALL // LANES)),
        full((4, d)), full((1, d)), full((4, bcw)), full((1, bcw)),
        full((1, LANES)), full((1, LANES)), full((1, d)), full((1, d)),
        full((SSD_GROUPS, LANES, gw)), full((tb, tb)), full((tb, 3 * (tb + CONV_HALO))),
    ]
    return pl.pallas_call(
        _ssd_body,
        grid=(bsz, nc),
        in_specs=in_specs,
        out_specs=pl.BlockSpec((tb, d), lambda b, c: (row(b, c), 0)),
        out_shape=jax.ShapeDtypeStruct((bsz * seq, d), BF16),
        scratch_shapes=[pltpu.VMEM((tb + CONV_HALO, d), F32),
                        pltpu.VMEM((tb + CONV_HALO, bcw), F32),
                        pltpu.VMEM((SSD_GROUPS, SSD_STATE, gw), F32)],
        compiler_params=_cparams(2),
        name="ssd_mixer",
    )(proj, proj, proj, proj, cwx, cbx, cwbc, cbbc, head_lanes(dt_bias), head_lanes(a_log),
      _row2(jnp.repeat(d_skip, SSD_HEAD_DIM)), _row2(norm_w), expand, tri, _conv_shift_matrix(tb))


def _mlstm_body(q_ref, k_ref, v_ref, og_ref, sm_ref, sel_ref, cwq_ref, cbq_ref, cwk_ref, cbk_ref,
                gb_ref, nw_ref, tri_ref, shift_ref, o_ref, qbuf, kbuf, ct_ref, n_ref, m_ref):
    rows = q_ref.shape[0]

    @pl.when(pl.program_id(2) == 0)
    def _():
        qbuf[0:CONV_HALO, :] = jnp.zeros((CONV_HALO, qbuf.shape[1]), F32)
        kbuf[0:CONV_HALO, :] = jnp.zeros((CONV_HALO, kbuf.shape[1]), F32)
        ct_ref[...] = jnp.zeros(ct_ref.shape, F32)
        n_ref[...] = jnp.zeros(n_ref.shape, F32)
        m_ref[...] = jnp.zeros(m_ref.shape, F32)

    heads = range(q_ref.shape[1] // ML_DK)
    q_all = _conv4_silu(qbuf, q_ref, cwq_ref, cbq_ref, shift_ref, rows) * (ML_DK ** -0.5)
    k_all = _conv4_silu(kbuf, k_ref, cwk_ref, cbk_ref, shift_ref, rows)
    qf = [q_all[:, i * ML_DK:(i + 1) * ML_DK] for i in heads]
    q = [qf[i].astype(BF16) for i in heads]
    k = [k_all[:, i * ML_DK:(i + 1) * ML_DK] for i in heads]
    kb = [k[i].astype(BF16) for i in heads]
    v = [v_ref[:, i * ML_DV:(i + 1) * ML_DV] for i in heads]

    pre = _dotx_l(sm_ref[...], sel_ref[...], 3) + gb_ref[...]
    log_f = jax.nn.log_sigmoid(pre)
    bcum_all = _dotx_r(tri_ref[...], log_f, 3)
    lane = lax.broadcasted_iota(jnp.int32, (rows, LANES), 1)
    gates_t = jnp.where(lane % 2 == 0, pre, bcum_all).T
    t_row = lax.broadcasted_iota(jnp.int32, (rows, rows), 0)
    t_col = lax.broadcasted_iota(jnp.int32, (rows, rows), 1)
    causal = t_row >= t_col

    li = [pre[:, 2 * i:2 * i + 1] for i in heads]
    bcum = [bcum_all[:, 2 * i + 1:2 * i + 2] for i in heads]
    log_d = [jnp.where(causal, bcum[i] - gates_t[2 * i + 1:2 * i + 2, :] + gates_t[2 * i:2 * i + 1, :],
                       -jnp.inf) for i in heads]
    qk = [_dot_nt(q[i], kb[i]) for i in heads]
    ct = [ct_ref[i] for i in heads]
    inter = [_dot(q[i], ct[i].astype(BF16)) for i in heads]
    m_prev = [m_ref[i, 0:1, 0:1] for i in heads]
    m_inter = [bcum[i] + m_prev[i] for i in heads]
    m_t = [jnp.maximum(jnp.max(log_d[i], axis=-1, keepdims=True), m_inter[i]) for i in heads]
    w = [jnp.exp(log_d[i] - m_t[i]) * qk[i] for i in heads]
    s_inter = [jnp.exp(m_inter[i] - m_t[i]) for i in heads]
    nvec = [n_ref[i, 0:1, :] for i in heads]
    num = [_dot(w[i].astype(BF16), v[i].astype(BF16)) + s_inter[i] * inter[i] for i in heads]
    den = [jnp.sum(w[i], axis=-1, keepdims=True)
           + s_inter[i] * jnp.sum(qf[i] * nvec[i], axis=-1, keepdims=True)
           for i in heads]
    h = [num[i] / jnp.maximum(jnp.abs(den[i]), jnp.exp(-m_t[i])) for i in heads]

    b_tot = [bcum[i][rows - 1:rows, :] for i in heads]
    log_w = [b_tot[i] - bcum[i] + li[i] for i in heads]
    m_new = [jnp.maximum(b_tot[i] + m_prev[i], jnp.max(log_w[i], axis=0, keepdims=True))
             for i in heads]
    w_end = [jnp.exp(log_w[i] - m_new[i]) for i in heads]
    s_old = [jnp.exp(b_tot[i] + m_prev[i] - m_new[i]) for i in heads]
    ct_new = [s_old[i] * ct[i] + _dot_tn(kb[i], (w_end[i] * v[i]).astype(BF16)) for i in heads]
    for i in heads:
        ct_ref[i] = ct_new[i]
        n_ref[i] = jnp.broadcast_to(
            s_old[i] * nvec[i] + jnp.sum(w_end[i] * k[i], axis=0, keepdims=True), n_ref.shape[1:])
        m_ref[i] = jnp.broadcast_to(m_new[i], m_ref.shape[1:])
        hn = h[i] * lax.rsqrt(jnp.mean(h[i] * h[i], axis=-1, keepdims=True) + ML_EPS)
        cols = slice(i * ML_DV, (i + 1) * ML_DV)
        o_ref[:, cols] = (jax.nn.sigmoid(og_ref[:, cols]) * hn * nw_ref[:, cols]).astype(o_ref.dtype)


def _mlstm_mixer(proj, bsz, seq, conv_w, conv_b, i_bias, f_bias, norm_w, tri):
    tb = TIME_BLOCK
    nc = seq // tb
    hg = ML_GROUP
    n_groups = ML_HEADS // hg
    qw, vw = hg * ML_DK, hg * ML_DV
    small_i = SSD_HEADS
    small_f = SSD_HEADS + ML_HEADS
    src = jnp.arange(LANES)[None, :, None]
    dst = jnp.arange(LANES)[None, None, :]
    grp = jnp.arange(n_groups)[:, None, None]
    head = grp * hg + dst // 2
    sel = ((dst < 2 * hg) & (((dst % 2 == 0) & (src == small_i + head))
                             | ((dst % 2 == 1) & (src == small_f + head)))).astype(BF16)
    gate_bias = jnp.stack([i_bias.reshape(n_groups, hg), f_bias.reshape(n_groups, hg)], axis=-1)
    gate_bias = jnp.pad(gate_bias.reshape(n_groups, 1, 2 * hg), ((0, 0), (0, 0), (0, LANES - 2 * hg)))
    conv_b2 = _row2(conv_b)
    kq = n_groups
    row = lambda b, h, c: b * nc + c
    in_specs = [
        pl.BlockSpec((tb, qw), lambda b, h, c: (row(b, h, c), C_QK // qw + h)),
        pl.BlockSpec((tb, qw), lambda b, h, c: (row(b, h, c), C_QK // qw + kq + h)),
        pl.BlockSpec((tb, vw), lambda b, h, c: (row(b, h, c), C_MV // vw + h)),
        pl.BlockSpec((tb, vw), lambda b, h, c: (row(b, h, c), C_MO // vw + h)),
        pl.BlockSpec((tb, LANES), lambda b, h, c: (row(b, h, c), C_SMALL // LANES)),
        pl.BlockSpec((None, LANES, LANES), lambda b, h, c: (h, 0, 0)),
        pl.BlockSpec((4, qw), lambda b, h, c: (0, h)),
        pl.BlockSpec((1, qw), lambda b, h, c: (0, h)),
        pl.BlockSpec((4, qw), lambda b, h, c: (0, kq + h)),
        pl.BlockSpec((1, qw), lambda b, h, c: (0, kq + h)),
        pl.BlockSpec((None, 1, LANES), lambda b, h, c: (h, 0, 0)),
        pl.BlockSpec((1, vw), lambda b, h, c: (0, h)),
        pl.BlockSpec((tb, tb), lambda b, h, c: (0, 0)),
        pl.BlockSpec((tb, 3 * (tb + CONV_HALO)), lambda b, h, c: (0, 0)),
    ]
    return pl.pallas_call(
        _mlstm_body,
        grid=(bsz, n_groups, nc),
        in_specs=in_specs,
        out_specs=pl.BlockSpec((tb, vw), lambda b, h, c: (row(b, h, c), h)),
        out_shape=jax.ShapeDtypeStruct((bsz * seq, D_MODEL), BF16),
        scratch_shapes=[pltpu.VMEM((tb + CONV_HALO, qw), F32),
                        pltpu.VMEM((tb + CONV_HALO, qw), F32),
                        pltpu.VMEM((hg, ML_DK, ML_DV), F32),
                        pltpu.VMEM((hg, HALO, LANES), F32),
                        pltpu.VMEM((hg, HALO, LANES), F32)],
        compiler_params=_cparams(3),
        name="mlstm_mixer",
    )(proj, proj, proj, proj, proj, sel, conv_w, conv_b2, conv_w, conv_b2, gate_bias,
      _row2(norm_w), tri, _conv_shift_matrix(tb))


def _stack_heads(x):
    lane = lax.broadcasted_iota(jnp.int32, x.shape, 1)
    return jnp.concatenate([jnp.where(lane < RWKV_HEAD_DIM, x, 0.0),
                            jnp.where(lane >= RWKV_HEAD_DIM, x, 0.0)], axis=0)


def _shift_lerp(buf, src_ref, mu_ref, rows):
    buf[HALO:HALO + rows, :] = src_ref[...]
    cur = buf[HALO:HALO + rows, :]
    prev = buf[HALO - 1:HALO - 1 + rows, :]
    buf[0:HALO, :] = buf[rows:rows + HALO, :]
    return cur + (prev - cur) * mu_ref[...]


RWKV_GROUP = 8
RWKV_TIME_BLOCK = 256


def _rwkv_body(r_ref, k_ref, v_ref, lo_ref, mur_ref, muk_ref, muv_ref, mulo_ref, w0_ref, w2_ref,
               a0_ref, a2_ref, g2_ref, kk_ref, ka_ref, rk_ref, lnw_ref, lnb_ref, tri_ref,
               ones_ref, o_ref, rbuf, kbuf, vbuf, lobuf, z_ref, y_buf):
    rows = r_ref.shape[0]
    n_pairs = r_ref.shape[1] // LANES
    ch = RWKV_CHUNK
    hd = RWKV_HEAD_DIM

    @pl.when(pl.program_id(2) == 0)
    def _():
        for buf in (rbuf, kbuf, vbuf, lobuf):
            buf[0:HALO, :] = jnp.zeros((HALO, buf.shape[1]), F32)
        z_ref[...] = jnp.zeros(z_ref.shape, F32)

    r = _shift_lerp(rbuf, r_ref, mur_ref, rows)
    k = _shift_lerp(kbuf, k_ref, muk_ref, rows)
    v = _shift_lerp(vbuf, v_ref, muv_ref, rows)
    lo = _shift_lerp(lobuf, lo_ref, mulo_ref, rows)
    w_lo, a_lo, g_lo = lo[:, 0:LANES], lo[:, LANES:2 * LANES], lo[:, 2 * LANES:]

    log_w = -jax.nn.softplus(-(w0_ref[...] + _mm(jnp.tanh(w_lo), w2_ref[...], 3))) - 0.5
    lw = -jnp.exp(log_w)
    a = jax.nn.sigmoid(a0_ref[...] + _mm(a_lo, a2_ref[...], 3))
    gate = _mm(jax.nn.sigmoid(g_lo), g2_ref[...], 1)

    head_sum = ones_ref[...]

    def head_total(x):
        return jnp.concatenate([_dotx_l(x[:, p * LANES:(p + 1) * LANES], head_sum, 2)
                                for p in range(n_pairs)], axis=1)

    kk = k * kk_ref[...]
    kk = kk * lax.rsqrt(jnp.maximum(head_total(kk * kk), 1e-24))
    kh = k * (1.0 + (a - 1.0) * ka_ref[...])
    kb = kk * a
    cum = _dotx_r(tri_ref[...], lw, 3)

    c_row = lax.broadcasted_iota(jnp.int32, (ch, LANES), 0)
    c_col = lax.broadcasted_iota(jnp.int32, (ch, LANES), 1) % hd
    strict = c_col < c_row
    incl = c_col <= c_row
    same_sub = (c_col // RWKV_SUB) == (c_row // RWKV_SUB)
    eye_pair = (c_col == c_row).astype(F32)
    s_row = lax.broadcasted_iota(jnp.int32, (LANES, LANES), 0)
    s_col = lax.broadcasted_iota(jnp.int32, (LANES, LANES), 1)
    same_head = (s_row // hd) == (s_col // hd)
    eye_full = (s_row == s_col).astype(F32)

    n_chunks = rows // ch
    cat = jnp.concatenate
    mm = functools.partial(_mm, passes=1)
    a_hat, r_hat, k_til, b_til, k_end, b_end, vp, decay_end = [], [], [], [], [], [], [], []
    for j in range(n_chunks):
        sl = slice(j * ch, (j + 1) * ch)
        cm, lwj, kkj, khj, kbj, rj, vj = cum[sl], lw[sl], kk[sl], kh[sl], kb[sl], r[sl], v[sl]
        cm_end = cm[ch - 1:ch, :]
        e_neg = jnp.exp(-cm)
        e_end = jnp.exp(cm_end - cm)
        wide = (-kkj * jnp.exp(cm - lwj), rj * jnp.exp(cm), khj * e_neg, kbj * e_neg,
                khj * e_end, kbj * e_end, vj, jnp.exp(cm_end))
        for pi in range(n_pairs):
            ls = slice(pi * LANES, (pi + 1) * LANES)
            for dst, src in zip((a_hat, r_hat, k_til, b_til, k_end, b_end, vp, decay_end), wide):
                dst.append(src[:, ls])
    units = range(n_chunks * n_pairs)
    each = lambda fn: [fn(u) for u in units]

    big = each(lambda u: mm(cat([a_hat[u], r_hat[u]], axis=0),
                            cat([_stack_heads(k_til[u]), _stack_heads(b_til[u])], axis=0),
                            dot=_dot_nt))
    l_ak = each(lambda u: jnp.where(strict, big[u][0:ch, 0:LANES], 0.0))
    l_ab = each(lambda u: jnp.where(strict, big[u][0:ch, LANES:], 0.0))
    m_rk = each(lambda u: jnp.where(incl, big[u][ch:, 0:LANES], 0.0))
    m_rb = each(lambda u: jnp.where(incl, big[u][ch:, LANES:], 0.0))

    l_d = each(lambda u: jnp.where(same_sub, l_ab[u], 0.0))
    l_o = each(lambda u: l_ab[u] - l_d[u])
    x = each(lambda u: eye_pair + l_d[u])
    p = each(lambda u: mm(l_d[u], _stack_heads(l_d[u])))
    for _ in range(2):
        both = each(lambda u: mm(cat([p[u], x[u]], axis=0), _stack_heads(p[u])))
        x = each(lambda u: x[u] + both[u][ch:])
        p = each(lambda u: both[u][:ch])
    t_d = each(lambda u: x[u] + mm(x[u], _stack_heads(p[u])))
    nn = each(lambda u: mm(t_d[u], _stack_heads(l_o[u])))
    n2 = each(lambda u: mm(nn[u], _stack_heads(nn[u])))
    mx = each(lambda u: eye_pair + nn[u])
    mx = each(lambda u: mx[u] + mm(mx[u], _stack_heads(n2[u])))
    t = each(lambda u: mm(mx[u], _stack_heads(t_d[u])))

    lmv = each(lambda u: mm(cat([l_ak[u], m_rk[u]], axis=0), _stack_heads(vp[u])))
    wu = each(lambda u: mm(t[u], cat([_stack_heads(a_hat[u]), _stack_heads(lmv[u][:ch])], axis=1)))
    w_a = each(lambda u: wu[u][:, 0:LANES])
    u_v = each(lambda u: wu[u][:, LANES:])
    mw = each(lambda u: mm(m_rb[u], cat([_stack_heads(w_a[u]), _stack_heads(u_v[u])], axis=1)))
    q_eff = each(lambda u: r_hat[u] + mw[u][:, 0:LANES])
    y_loc = each(lambda u: lmv[u][ch:] + mw[u][:, LANES:])
    a_eff = each(lambda u: jnp.where(same_head, mm(b_end[u], w_a[u], dot=_dot_tn), 0.0)
                 + eye_full * decay_end[u])
    z_loc = each(lambda u: jnp.where(
        same_head, mm(cat([k_end[u], b_end[u]], axis=0), cat([vp[u], u_v[u]], axis=0),
                      dot=_dot_tn), 0.0))

    z = [z_ref[pi] for pi in range(n_pairs)]
    for j in range(n_chunks):
        sl = slice(j * ch, (j + 1) * ch)
        us = [j * n_pairs + pi for pi in range(n_pairs)]
        qz = [mm(cat([q_eff[u], a_eff[u]], axis=0), z[pi]) for pi, u in enumerate(us)]
        for pi, u in enumerate(us):
            y_buf[sl, pi * LANES:(pi + 1) * LANES] = qz[pi][:ch] + y_loc[u]
        z = [qz[pi][ch:] + z_loc[u] for pi, u in enumerate(us)]
    for pi in range(n_pairs):
        z_ref[pi] = z[pi]

    y = y_buf[...]
    mean = head_total(y) * (1.0 / hd)
    yc = y - mean
    var = head_total(yc * yc) * (1.0 / hd)
    yn = yc * lax.rsqrt(var + RWKV_LN_EPS) * lnw_ref[...] + lnb_ref[...]
    bonus = head_total(r * kh * rk_ref[...]) * v
    o_ref[...] = ((yn + bonus) * gate).astype(o_ref.dtype)


def _rwkv_mixer(proj, bsz, seq, mu, w0, w2, a0, a2, g2, k_k, k_a, r_k, ln_w, ln_b):
    tb = RWKV_TIME_BLOCK
    nc = seq // tb
    d = D_MODEL
    lora_w = 2 * LANES + RWKV_LORA_G
    mur, muk, muv = _row2(mu[0:d]), _row2(mu[d:2 * d]), _row2(mu[2 * d:3 * d])
    o = 3 * d
    pad_l = lambda x, n: jnp.pad(x, (0, n - x.shape[0]))
    mulo = _row2(jnp.concatenate([pad_l(mu[o:o + RWKV_LORA_W], LANES),
                                  pad_l(mu[o + RWKV_LORA_W:o + RWKV_LORA_W + RWKV_LORA_A], LANES),
                                  mu[o + RWKV_LORA_W + RWKV_LORA_A:]]))
    w2p = jnp.pad(w2, ((0, LANES - RWKV_LORA_W), (0, 0)))
    a2p = jnp.pad(a2, ((0, LANES - RWKV_LORA_A), (0, 0)))
    ch = RWKV_CHUNK
    t_i = jnp.arange(tb)[:, None]
    t_j = jnp.arange(tb)[None, :]
    tri = ((t_i >= t_j) & (t_i // ch == t_j // ch)).astype(BF16)
    l_i = jnp.arange(LANES)[:, None]
    l_j = jnp.arange(LANES)[None, :]
    head_sum = (l_i // RWKV_HEAD_DIM == l_j // RWKV_HEAD_DIM).astype(BF16)

    gw = RWKV_GROUP * LANES
    row = lambda b, p, c: b * nc + c
    col = lambda base: (lambda b, p, c: (row(b, p, c), base // gw + p))
    vec = pl.BlockSpec((1, gw), lambda b, p, c: (0, p))
    in_specs = [
        pl.BlockSpec((tb, gw), col(C_R)),
        pl.BlockSpec((tb, gw), col(C_K)),
        pl.BlockSpec((tb, gw), col(C_V)),
        pl.BlockSpec((tb, lora_w), lambda b, p, c: (row(b, p, c), C_LORA // lora_w)),
        vec, vec, vec,
        pl.BlockSpec((1, lora_w), lambda b, p, c: (0, 0)),
        vec,
        pl.BlockSpec((LANES, gw), lambda b, p, c: (0, p)),
        vec,
        pl.BlockSpec((LANES, gw), lambda b, p, c: (0, p)),
        pl.BlockSpec((RWKV_LORA_G, gw), lambda b, p, c: (0, p)),
        vec, vec, vec, vec, vec,
        pl.BlockSpec((tb, tb), lambda b, p, c: (0, 0)),
        pl.BlockSpec((LANES, LANES), lambda b, p, c: (0, 0)),
    ]
    return pl.pallas_call(
        _rwkv_body,
        grid=(bsz, RWKV_PAIRS // RWKV_GROUP, nc),
        in_specs=in_specs,
        out_specs=pl.BlockSpec((tb, gw), lambda b, p, c: (row(b, p, c), p)),
        out_shape=jax.ShapeDtypeStruct((bsz * seq, D_MODEL), BF16),
        scratch_shapes=[pltpu.VMEM((tb + HALO, gw), F32),
                        pltpu.VMEM((tb + HALO, gw), F32),
                        pltpu.VMEM((tb + HALO, gw), F32),
                        pltpu.VMEM((tb + HALO, lora_w), F32),
                        pltpu.VMEM((RWKV_GROUP, LANES, LANES), F32),
                        pltpu.VMEM((tb, gw), F32)],
        compiler_params=_cparams(3),
        name="rwkv_mixer",
    )(proj, proj, proj, proj, mur, muk, muv, mulo, _row2(w0), w2p, _row2(a0), a2p, g2,
      _row2(k_k), _row2(k_a), _row2(r_k.reshape(-1)), _row2(ln_w), _row2(ln_b), tri, head_sum)


def _gated_merge_body(ys_ref, yr_ref, ym_ref, ps_ref, pr_ref, pm_ref, gs_ref, gr_ref, gm_ref, o_ref):
    acc = jax.nn.sigmoid(gs_ref[...]) * _dot(ys_ref[...], ps_ref[...])
    acc = acc + jax.nn.sigmoid(gr_ref[...]) * _dot(yr_ref[...], pr_ref[...])
    acc = acc + jax.nn.sigmoid(gm_ref[...]) * _dot(ym_ref[...], pm_ref[...])
    o_ref[...] = acc.astype(o_ref.dtype)


def _gated_merge(ys, ps, proj, tm=512, tn=1024):
    m, k = ys[0].shape
    tm = min(tm, m)
    n = ps[0].shape[1]
    y_spec = pl.BlockSpec((tm, k), lambda j, i: (i, 0))
    p_spec = pl.BlockSpec((k, tn), lambda j, i: (0, j), pipeline_mode=pl.Buffered(1))
    gate_spec = lambda col: pl.BlockSpec((tm, tn), lambda j, i: (i, col // tn + j))
    return pl.pallas_call(
        _gated_merge_body,
        grid=(n // tn, m // tm),
        in_specs=[y_spec] * 3 + [p_spec] * 3 + [gate_spec(C_GS), gate_spec(C_GR), gate_spec(C_GM)],
        out_specs=pl.BlockSpec((tm, tn), lambda j, i: (i, j)),
        out_shape=jax.ShapeDtypeStruct((m, n), BF16),
        compiler_params=_cparams(2),
        name="gated_merge",
    )(*ys, *ps, proj, proj, proj)


def _mm_res_ln_body(a_ref, w_ref, h_ref, g_ref, b_ref, o_ref, ob_ref, *, row_split):
    n = a_ref.shape[0] // row_split
    subs = [slice(s * n, (s + 1) * n) for s in range(row_split)]
    acc = [_dot(a_ref[rows, :], w_ref[...]) for rows in subs]
    for rows, mix in zip(subs, acc):
        h = _layer_norm_rows(ALPHA * h_ref[rows, :] + mix, g_ref[...], b_ref[...])
        o_ref[rows, :] = h
        ob_ref[rows, :] = h.astype(BF16)


def _matmul_res_ln(a, w, h, g, b, tm, row_split):
    m, k = a.shape
    n = w.shape[1]
    const = lambda shape: pl.BlockSpec(shape, lambda i: (0, 0), pipeline_mode=pl.Buffered(1))
    return pl.pallas_call(
        functools.partial(_mm_res_ln_body, row_split=row_split),
        grid=(m // tm,),
        in_specs=[pl.BlockSpec((tm, k), lambda i: (i, 0)),
                  const((k, n)),
                  pl.BlockSpec((tm, n), lambda i: (i, 0)),
                  const((1, n)), const((1, n))],
        out_specs=[pl.BlockSpec((tm, n), lambda i: (i, 0)),
                   pl.BlockSpec((tm, n), lambda i: (i, 0))],
        out_shape=[jax.ShapeDtypeStruct((m, n), F32), jax.ShapeDtypeStruct((m, n), BF16)],
        compiler_params=_cparams(1),
        name="matmul_res_ln",
    )(a, w, h, _row2(g), _row2(b))


FFN_HALO = 16


def _ffn_up_body(h_ref, hp_ref, wg_ref, wua_ref, wub_ref, cwg_ref, cbg_ref, cwu_ref, cbu_ref, o_ref,
                 gbuf, ubuf, wg_s, wu_s, *, blocks_per_seq):
    rows = h_ref.shape[0]
    tn = o_ref.shape[1]
    first = (pl.program_id(1) % blocks_per_seq) == 0
    keep = jnp.where(first, 0.0, 1.0)

    @pl.when(pl.program_id(1) == 0)
    def _():
        col = pl.program_id(0) * tn + lax.broadcasted_iota(jnp.int32, wg_s.shape, 1)
        off = FFN_D % tn
        up = jnp.concatenate([wua_ref[:, off:], wub_ref[:, :off]], axis=1)
        wg_s[...] = jnp.where(col < FFN_D, wg_ref[...], 0.0).astype(BF16)
        wu_s[...] = jnp.where(col < FFN_D, up, 0.0).astype(BF16)

    def conv(buf, cw_ref, cb_ref):
        acc = cb_ref[...] + cw_ref[2:3, :] * buf[FFN_HALO:FFN_HALO + rows, :]
        acc = acc + cw_ref[1:2, :] * buf[FFN_HALO - 1:FFN_HALO - 1 + rows, :]
        return acc + cw_ref[0:1, :] * buf[FFN_HALO - 2:FFN_HALO - 2 + rows, :]

    for buf, w_ref in ((gbuf, wg_s), (ubuf, wu_s)):
        buf[0:FFN_HALO, :] = _dot(hp_ref[...], w_ref[...]) * keep
        buf[FFN_HALO:FFN_HALO + rows, :] = _dot(h_ref[...], w_ref[...])
    gate = conv(gbuf, cwg_ref, cbg_ref)
    up = conv(ubuf, cwu_ref, cbu_ref)
    gelu = 0.5 * gate * (1.0 + lax.erf(gate * (2.0 ** -0.5)))
    o_ref[...] = (gelu * up).astype(o_ref.dtype)


def _ffn_up(hb, w_up, layer, cwg, cbg, cwu, cbu, seq, tm=1024):
    m, k = hb.shape
    n = FFN_PAD
    tn = FFN_TN
    halo_blocks = tm // FFN_HALO
    up0 = FFN_D // tn
    body = functools.partial(_ffn_up_body, blocks_per_seq=seq // tm)
    w_spec = lambda first_block: pl.BlockSpec((None, k, tn), lambda j, i: (layer, 0, first_block + j))
    return pl.pallas_call(
        body,
        grid=(n // tn, m // tm),
        in_specs=[pl.BlockSpec((tm, k), lambda j, i: (i, 0)),
                  pl.BlockSpec((FFN_HALO, k), lambda j, i: (jnp.maximum(i * halo_blocks - 1, 0), 0)),
                  w_spec(0), w_spec(up0), w_spec(up0 + 1),
                  pl.BlockSpec((3, tn), lambda j, i: (0, j)),
                  pl.BlockSpec((1, tn), lambda j, i: (0, j)),
                  pl.BlockSpec((3, tn), lambda j, i: (0, j)),
                  pl.BlockSpec((1, tn), lambda j, i: (0, j))],
        out_specs=pl.BlockSpec((tm, tn), lambda j, i: (i, j)),
        out_shape=jax.ShapeDtypeStruct((m, n), BF16),
        scratch_shapes=[pltpu.VMEM((tm + FFN_HALO, tn), F32),
                        pltpu.VMEM((tm + FFN_HALO, tn), F32),
                        pltpu.VMEM((k, tn), BF16),
                        pltpu.VMEM((k, tn), BF16)],
        compiler_params=_cparams(2),
        name="ffn_up",
    )(hb, hb, w_up, w_up, w_up, cwg, cbg, cwu, cbu)


def _pack_segments():
    d = D_MODEL
    order = (("z", d), ("x", d), ("bc", 2 * SSD_GN), ("dt", SSD_HEADS), ("r", d), ("k", d), ("v", d),
             ("w_lo", RWKV_LORA_W), ("a_lo", RWKV_LORA_A), ("g_lo", RWKV_LORA_G), ("qk", d),
             ("mv", d), ("mo", d), ("gi", ML_HEADS), ("gf", ML_HEADS), ("gs", d), ("gr", d), ("gm", d))
    dst = dict(z=C_Z, x=C_X, r=C_R, k=C_K, v=C_V, qk=C_QK, mv=C_MV, mo=C_MO, gs=C_GS, gr=C_GR,
               gm=C_GM, bc=C_BC, w_lo=C_LORA, a_lo=C_LORA + LANES, g_lo=C_LORA + 2 * LANES,
               dt=C_SMALL, gi=C_SMALL + SSD_HEADS, gf=C_SMALL + SSD_HEADS + ML_HEADS)
    segments, src = [], 0
    for name, width in order:
        segments.append((src, width, dst[name]))
        src += width
    return tuple(segments), src


def _pack_body(w_ref, o_ref, *, segments):
    o_ref[...] = jnp.zeros(o_ref.shape, o_ref.dtype)
    for src, width, dst in segments:
        lo = src // LANES * LANES
        hi = min(-(-(src + width) // LANES) * LANES, w_ref.shape[1])
        piece = w_ref[:, lo:hi][:, src - lo:src - lo + width]
        o_ref[:, dst:dst + width] = piece.astype(o_ref.dtype)


def _pack_w_in(w_in, layer, rows=64):
    segments, d_in = _pack_segments()
    d = w_in.shape[1]
    return pl.pallas_call(
        functools.partial(_pack_body, segments=segments),
        grid=(d // rows,),
        in_specs=[pl.BlockSpec((None, rows, d_in), lambda i: (layer, i, 0))],
        out_specs=pl.BlockSpec((rows, N_PROJ), lambda i: (i, 0)),
        out_shape=jax.ShapeDtypeStruct((d, N_PROJ), BF16),
        compiler_params=_cparams(1),
        name="pack_w_in",
    )(w_in)


def _to_bf16_body(w_ref, o_ref, *, valid_rows):
    x = w_ref[...]
    if valid_rows is not None:
        row = pl.program_id(0) * o_ref.shape[0] + lax.broadcasted_iota(jnp.int32, o_ref.shape, 0)
        x = jnp.where(row < valid_rows, x, 0.0)
    o_ref[...] = x.astype(o_ref.dtype)


def _to_bf16(w_stack, layer, out_rows=None, rows_per_step=512):
    _, rows, cols = w_stack.shape
    out_rows = rows if out_rows is None else out_rows
    tr = rows_per_step
    last = (rows - 1) // tr
    return pl.pallas_call(
        functools.partial(_to_bf16_body, valid_rows=None if out_rows == rows else rows),
        grid=(out_rows // tr,),
        in_specs=[pl.BlockSpec((None, tr, cols), lambda i: (layer, jnp.minimum(i, last), 0))],
        out_specs=pl.BlockSpec((tr, cols), lambda i: (i, 0)),
        out_shape=jax.ShapeDtypeStruct((out_rows, cols), BF16),
        compiler_params=_cparams(1),
        name="to_bf16",
    )(w_stack)


def _pad_cols(x, n):
    return jnp.pad(x, ((0, 0), (0, n - x.shape[1])))


def kernel(x, ln_in_g, ln_in_b, w_in, ssd_conv_w, ssd_conv_b, ssd_dt_bias, ssd_a_log, ssd_d, ssd_norm_w, rwkv_mu, rwkv_w0, rwkv_w2, rwkv_a0, rwkv_a2, rwkv_g2, rwkv_k_k, rwkv_k_a, rwkv_r_k, rwkv_ln_w, rwkv_ln_b, mlstm_conv_w, mlstm_conv_b, mlstm_i_bias, mlstm_f_bias, mlstm_norm_w, proj_ssd, proj_rwkv, proj_mlstm, w_out, ln1_g, ln1_b, ffn_w_up, ffn_conv_w, ffn_conv_b, ffn_w_down, ln2_g, ln2_b):
    bsz, seq, d = x.shape
    n_tok = bsz * seq
    tb = TIME_BLOCK
    t_i = jnp.arange(tb)[:, None]
    t_j = jnp.arange(tb)[None, :]
    tri = (t_i >= t_j).astype(BF16)

    h, hb = _ln_in(x.reshape(n_tok, d), ln_in_g, ln_in_b)
    for l in range(DEPTH):
        proj = _matmul(hb, _pack_w_in(w_in, l), tm=min(1024, n_tok), tn=2048, out_dtype=F32,
                       name="in_proj")
        y_ssd = _ssd_mixer(proj, bsz, seq, ssd_conv_w[l], ssd_conv_b[l], ssd_dt_bias[l],
                           ssd_a_log[l], ssd_d[l], ssd_norm_w[l], tri)
        y_rwkv = _rwkv_mixer(proj, bsz, seq, rwkv_mu[l], rwkv_w0[l], rwkv_w2[l], rwkv_a0[l],
                             rwkv_a2[l], rwkv_g2[l], rwkv_k_k[l], rwkv_k_a[l], rwkv_r_k[l],
                             rwkv_ln_w[l], rwkv_ln_b[l])
        y_mlstm = _mlstm_mixer(proj, bsz, seq, mlstm_conv_w[l], mlstm_conv_b[l], mlstm_i_bias[l],
                               mlstm_f_bias[l], mlstm_norm_w[l], tri)
        merged = _gated_merge(
            (y_ssd, y_rwkv, y_mlstm),
            (_to_bf16(proj_ssd, l), _to_bf16(proj_rwkv, l), _to_bf16(proj_mlstm, l)), proj)
        h, hb = _matmul_res_ln(merged, _to_bf16(w_out, l), h, ln1_g[l], ln1_b[l],
                               tm=min(512, n_tok), row_split=2)

        cwg = _pad_cols(ffn_conv_w[l][:, :FFN_D], FFN_PAD)
        cwu = _pad_cols(ffn_conv_w[l][:, FFN_D:], FFN_PAD)
        cbg = _pad_cols(_row2(ffn_conv_b[l][:FFN_D]), FFN_PAD)
        cbu = _pad_cols(_row2(ffn_conv_b[l][FFN_D:]), FFN_PAD)
        act = _ffn_up(hb, ffn_w_up, l, cwg, cbg, cwu, cbu, seq, tm=min(1024, seq))
        wd = _to_bf16(ffn_w_down, l, out_rows=FFN_PAD)
        h, hb = _matmul_res_ln(act, wd, h, ln2_g[l], ln2_b[l], tm=min(256, n_tok), row_split=1)
    return h.reshape(bsz, seq, d)
```

```python
import functools
import math

import jax
import jax.numpy as jnp
from jax import lax
from jax.experimental import pallas as pl
from jax.experimental.pallas import tpu as pltpu

F32 = jnp.float32
BF16 = jnp.bfloat16

D_MODEL = 2048
DEPTH = 2
ALPHA = (2.0 * DEPTH) ** 0.25
LN_EPS = 1e-5

SSD_HEADS = 32
SSD_HEAD_DIM = 64
SSD_GROUPS = 4
SSD_STATE = 128
SSD_GN = SSD_GROUPS * SSD_STATE
SSD_EPS = 1e-5
SSD_GROUP_W = D_MODEL // SSD_GROUPS
SSD_HPG = SSD_HEADS // SSD_GROUPS

RWKV_HEAD_DIM = 64
RWKV_PAIRS = D_MODEL // 128
RWKV_LORA_W = 96
RWKV_LORA_A = 96
RWKV_LORA_G = 256
RWKV_LN_EPS = 64e-5
RWKV_CHUNK = 64
RWKV_SUB = 16

ML_HEADS = 8
ML_DK = 128
ML_DV = 256
ML_EPS = 1e-6
ML_GROUP = 8

FFN_D = 5504
FFN_PAD = 5632
FFN_TN = 512

LANES = 128
HALO = 8
TIME_BLOCK = 128

C_Z = 0
C_X = 2048
C_R = 4096
C_K = 6144
C_V = 8192
C_QK = 10240
C_MV = 12288
C_MO = 14336
C_GS = 16384
C_GR = 18432
C_GM = 20480
C_BC = 22528
C_LORA = 23552
C_SMALL = 24064
N_PROJ = 24576

VMEM_LIMIT = 56 * 1024 * 1024


def _dot(a, b):
    return jnp.dot(a, b, preferred_element_type=F32)


def _dot_nt(a, b):
    return lax.dot_general(a, b, (((1,), (1,)), ((), ())), preferred_element_type=F32)


def _dot_tn(a, b):
    return lax.dot_general(a, b, (((0,), (0,)), ((), ())), preferred_element_type=F32)


def _split(x, n):
    parts, r = [], x
    for i in range(n):
        p = r.astype(BF16)
        parts.append(p)
        if i + 1 < n:
            r = r - p.astype(F32)
    return parts


def _dotx_l(x, w_exact, n, dot=_dot):
    acc = None
    for p in _split(x, n):
        t = dot(p, w_exact)
        acc = t if acc is None else acc + t
    return acc


def _dotx_r(w_exact, x, n, dot=_dot):
    acc = None
    for p in _split(x, n):
        t = dot(w_exact, p)
        acc = t if acc is None else acc + t
    return acc


def _mm(a, b, passes, dot=_dot):
    if passes == 1:
        return dot(a.astype(BF16), b.astype(BF16))
    ah, al = _split(a, 2)
    bh, bl = _split(b, 2)
    return dot(ah, bh) + (dot(ah, bl) + dot(al, bh))


def _layer_norm_rows(x, g, b):
    mu = jnp.mean(x, axis=-1, keepdims=True)
    xc = x - mu
    var = jnp.mean(xc * xc, axis=-1, keepdims=True)
    return xc * lax.rsqrt(var + LN_EPS) * g + b


def _cparams(n_axes):
    return pltpu.CompilerParams(dimension_semantics=("arbitrary",) * n_axes,
                                vmem_limit_bytes=VMEM_LIMIT)


def _row2(v):
    return v.reshape(1, -1)


def _ln_in_body(x_ref, g_ref, b_ref, h_ref, hb_ref):
    h = _layer_norm_rows(x_ref[...], g_ref[...], b_ref[...])
    h_ref[...] = h
    hb_ref[...] = h.astype(BF16)


def _ln_in(x2d, g, b, tm=512):
    n, d = x2d.shape
    tm = min(tm, n)
    return pl.pallas_call(
        _ln_in_body,
        grid=(n // tm,),
        in_specs=[pl.BlockSpec((tm, d), lambda i: (i, 0)),
                  pl.BlockSpec((1, d), lambda i: (0, 0)),
                  pl.BlockSpec((1, d), lambda i: (0, 0))],
        out_specs=[pl.BlockSpec((tm, d), lambda i: (i, 0)),
                   pl.BlockSpec((tm, d), lambda i: (i, 0))],
        out_shape=[jax.ShapeDtypeStruct((n, d), F32), jax.ShapeDtypeStruct((n, d), BF16)],
        compiler_params=_cparams(1),
        name="ln_in",
    )(x2d, _row2(g), _row2(b))


def _mm_body(a_ref, w_ref, o_ref):
    o_ref[...] = _dot(a_ref[...], w_ref[...]).astype(o_ref.dtype)


def _matmul(a, w, tm, tn, out_dtype, name):
    m, k = a.shape
    _, n = w.shape
    return pl.pallas_call(
        _mm_body,
        grid=(n // tn, m // tm),
        in_specs=[pl.BlockSpec((tm, k), lambda j, i: (i, 0)),
                  pl.BlockSpec((k, tn), lambda j, i: (0, j))],
        out_specs=pl.BlockSpec((tm, tn), lambda j, i: (i, j)),
        out_shape=jax.ShapeDtypeStruct((m, n), out_dtype),
        compiler_params=_cparams(2),
        name=name,
    )(a, w)


CONV_HALO = 16


def _conv_shift_matrix(rows):
    t = jnp.arange(rows)[:, None]
    c = jnp.arange(3 * (rows + CONV_HALO))[None, :]
    k, j = c // (rows + CONV_HALO), c % (rows + CONV_HALO)
    return (j == CONV_HALO + t - 3 + k).astype(BF16)


def _conv4_silu(buf, src_ref, w_ref, b_ref, shift_ref, rows):
    x = src_ref[...]
    buf[CONV_HALO:CONV_HALO + rows, :] = x
    hist = buf[...]
    taps = jnp.concatenate([(hist * w_ref[k:k + 1, :]).astype(BF16) for k in range(3)], axis=0)
    acc = b_ref[...] + w_ref[3:4, :] * x + _dot(shift_ref[...], taps)
    buf[0:CONV_HALO, :] = buf[rows:rows + CONV_HALO, :]
    return jax.nn.silu(acc)


def _ssd_body(z_ref, x_ref, bc_ref, sm_ref, cwx_ref, cbx_ref, cwbc_ref, cbbc_ref, dtb_ref,
              alog_ref, dsk_ref, nw_ref, e_ref, tri_ref, shift_ref, o_ref, xbuf, bcbuf, st_ref):
    rows = x_ref.shape[0]
    gw = SSD_GROUP_W
    groups = range(SSD_GROUPS)

    @pl.when(pl.program_id(1) == 0)
    def _():
        xbuf[0:CONV_HALO, :] = jnp.zeros((CONV_HALO, xbuf.shape[1]), F32)
        bcbuf[0:CONV_HALO, :] = jnp.zeros((CONV_HALO, bcbuf.shape[1]), F32)
        st_ref[...] = jnp.zeros(st_ref.shape, F32)

    xs_all = _conv4_silu(xbuf, x_ref, cwx_ref, cbx_ref, shift_ref, rows)
    bc = _conv4_silu(bcbuf, bc_ref, cwbc_ref, cbbc_ref, shift_ref, rows)

    dt = jax.nn.softplus(sm_ref[...] + dtb_ref[...])
    adt = dt * (-jnp.exp(alog_ref[...]))
    a_cs = _dotx_r(tri_ref[...], adt, 3)
    a_cs_t = a_cs.T

    t_row = lax.broadcasted_iota(jnp.int32, (rows, rows), 0)
    t_col = lax.broadcasted_iota(jnp.int32, (rows, rows), 1)
    causal = t_row >= t_col
    lane = lax.broadcasted_iota(jnp.int32, (rows, LANES), 1)

    xs = [xs_all[:, g * gw:(g + 1) * gw] for g in groups]
    bm = [bc[:, g * SSD_STATE:(g + 1) * SSD_STATE].astype(BF16) for g in groups]
    cm = [bc[:, SSD_GN + g * SSD_STATE:SSD_GN + (g + 1) * SSD_STATE].astype(BF16) for g in groups]
    a_exp = [_dotx_l(a_cs, e_ref[g], 3) for g in groups]
    dt_exp = [_dotx_l(dt, e_ref[g], 3) for g in groups]
    scores = [_dot_nt(cm[g], bm[g]) for g in groups]
    st = [st_ref[g] for g in groups]
    y_off = [_dot(cm[g], st[g].astype(BF16)) for g in groups]
    xdt = [xs[g] * dt_exp[g] for g in groups]

    y_diag = []
    for g in groups:
        parts = []
        for pr in range(SSD_HPG // 2):
            xp = xdt[g][:, pr * LANES:(pr + 1) * LANES]
            acc = None
            for s in range(2):
                hh = g * SSD_HPG + 2 * pr + s
                seg = a_cs[:, hh:hh + 1] - a_cs_t[hh:hh + 1, :]
                m = jnp.where(causal, jnp.exp(seg), 0.0) * scores[g]
                head_lanes = (lane < SSD_HEAD_DIM) if s == 0 else (lane >= SSD_HEAD_DIM)
                xm = jnp.where(head_lanes, xp, 0.0)
                t = _dot(m.astype(BF16), xm.astype(BF16))
                acc = t if acc is None else acc + t
            parts.append(acc)
        y_diag.append(jnp.concatenate(parts, axis=1))

    for g in groups:
        cols = slice(g * gw, (g + 1) * gw)
        a_last = a_exp[g][rows - 1:rows, :]
        st_ref[g] = jnp.exp(a_last) * st[g] + _dot_tn(
            bm[g], (xdt[g] * jnp.exp(a_last - a_exp[g])).astype(BF16))
        y = y_diag[g] + y_off[g] * jnp.exp(a_exp[g]) + xs[g] * dsk_ref[:, cols]
        y = y * jax.nn.silu(z_ref[:, cols])
        ms = jnp.mean(y * y, axis=-1, keepdims=True)
        o_ref[:, cols] = (y * lax.rsqrt(ms + SSD_EPS) * nw_ref[:, cols]).astype(o_ref.dtype)


def _ssd_mixer(proj, bsz, seq, conv_w, conv_b, dt_bias, a_log, d_skip, norm_w, tri):
    tb = TIME_BLOCK
    nc = seq // tb
    gw = SSD_GROUP_W
    d = D_MODEL
    bcw = 2 * SSD_GN
    cwx, cwbc = conv_w[:, :d], conv_w[:, d:]
    cbx, cbbc = _row2(conv_b[:d]), _row2(conv_b[d:])
    head_lanes = lambda v: _row2(jnp.pad(v, (0, LANES - SSD_HEADS)))
    grp = jnp.arange(SSD_GROUPS)[:, None, None]
    src = jnp.arange(LANES)[None, :, None]
    dst = jnp.arange(gw)[None, None, :]
    expand = (src == grp * SSD_HPG + dst // SSD_HEAD_DIM).astype(BF16)

    row = lambda b, c: b * nc + c
    full = lambda shape: pl.BlockSpec(shape, lambda b, c: (0,) * len(shape))
    in_specs = [
        pl.BlockSpec((tb, d), lambda b, c: (row(b, c), C_Z // d)),
        pl.BlockSpec((tb, d), lambda b, c: (row(b, c), C_X // d)),
        pl.BlockSpec((tb, bcw), lambda b, c: (row(b, c), C_BC // bcw)),
        pl.BlockSpec((tb, LANES), lambda b, c: (row(b, c), C_SMALL // LANES)),
        full((4, d)), full((1, d)), full((4, bcw)), full((1, bcw)),
        full((1, LANES)), full((1, LANES)), full((1, d)), full((1, d)),
        full((SSD_GROUPS, LANES, gw)), full((tb, tb)), full((tb, 3 * (tb + CONV_HALO))),
    ]
    return pl.pallas_call(
        _ssd_body,
        grid=(bsz, nc),
        in_specs=in_specs,
        out_specs=pl.BlockSpec((tb, d), lambda b, c: (row(b, c), 0)),
        out_shape=jax.ShapeDtypeStruct((bsz * seq, d), BF16),
        scratch_shapes=[pltpu.VMEM((tb + CONV_HALO, d), F32),
                        pltpu.VMEM((tb + CONV_HALO, bcw), F32),
                        pltpu.VMEM((SSD_GROUPS, SSD_STATE, gw), F32)],
        compiler_params=_cparams(2),
        name="ssd_mixer",
    )(proj, proj, proj, proj, cwx, cbx, cwbc, cbbc, head_lanes(dt_bias), head_lanes(a_log),
      _row2(jnp.repeat(d_skip, SSD_HEAD_DIM)), _row2(norm_w), expand, tri, _conv_shift_matrix(tb))


def _mlstm_body(q_ref, k_ref, v_ref, og_ref, sm_ref, sel_ref, cwq_ref, cbq_ref, cwk_ref, cbk_ref,
                gb_ref, nw_ref, tri_ref, shift_ref, o_ref, qbuf, kbuf, ct_ref, n_ref, m_ref):
    rows = q_ref.shape[0]

    @pl.when(pl.program_id(2) == 0)
    def _():
        qbuf[0:CONV_HALO, :] = jnp.zeros((CONV_HALO, qbuf.shape[1]), F32)
        kbuf[0:CONV_HALO, :] = jnp.zeros((CONV_HALO, kbuf.shape[1]), F32)
        ct_ref[...] = jnp.zeros(ct_ref.shape, F32)
        n_ref[...] = jnp.zeros(n_ref.shape, F32)
        m_ref[...] = jnp.zeros(m_ref.shape, F32)

    heads = range(q_ref.shape[1] // ML_DK)
    q_all = _conv4_silu(qbuf, q_ref, cwq_ref, cbq_ref, shift_ref, rows) * (ML_DK ** -0.5)
    k_all = _conv4_silu(kbuf, k_ref, cwk_ref, cbk_ref, shift_ref, rows)
    qf = [q_all[:, i * ML_DK:(i + 1) * ML_DK] for i in heads]
    q = [qf[i].astype(BF16) for i in heads]
    k = [k_all[:, i * ML_DK:(i + 1) * ML_DK] for i in heads]
    kb = [k[i].astype(BF16) for i in heads]
    v = [v_ref[:, i * ML_DV:(i + 1) * ML_DV] for i in heads]

    pre = _dotx_l(sm_ref[...], sel_ref[...], 3) + gb_ref[...]
    log_f = jax.nn.log_sigmoid(pre)
    bcum_all = _dotx_r(tri_ref[...], log_f, 3)
    lane = lax.broadcasted_iota(jnp.int32, (rows, LANES), 1)
    gates_t = jnp.where(lane % 2 == 0, pre, bcum_all).T
    t_row = lax.broadcasted_iota(jnp.int32, (rows, rows), 0)
    t_col = lax.broadcasted_iota(jnp.int32, (rows, rows), 1)
    causal = t_row >= t_col

    li = [pre[:, 2 * i:2 * i + 1] for i in heads]
    bcum = [bcum_all[:, 2 * i + 1:2 * i + 2] for i in heads]
    log_d = [jnp.where(causal, bcum[i] - gates_t[2 * i + 1:2 * i + 2, :] + gates_t[2 * i:2 * i + 1, :],
                       -jnp.inf) for i in heads]
    qk = [_dot_nt(q[i], kb[i]) for i in heads]
    ct = [ct_ref[i] for i in heads]
    inter = [_dot(q[i], ct[i].astype(BF16)) for i in heads]
    m_prev = [m_ref[i, 0:1, 0:1] for i in heads]
    m_inter = [bcum[i] + m_prev[i] for i in heads]
    m_t = [jnp.maximum(jnp.max(log_d[i], axis=-1, keepdims=True), m_inter[i]) for i in heads]
    w = [jnp.exp(log_d[i] - m_t[i]) * qk[i] for i in heads]
    s_inter = [jnp.exp(m_inter[i] - m_t[i]) for i in heads]
    nvec = [n_ref[i, 0:1, :] for i in heads]
    num = [_dot(w[i].astype(BF16), v[i].astype(BF16)) + s_inter[i] * inter[i] for i in heads]
    den = [jnp.sum(w[i], axis=-1, keepdims=True)
           + s_inter[i] * jnp.sum(qf[i] * nvec[i], axis=-1, keepdims=True)
           for i in heads]
    h = [num[i] / jnp.maximum(jnp.abs(den[i]), jnp.exp(-m_t[i])) for i in heads]

    b_tot = [bcum[i][rows - 1:rows, :] for i in heads]
    log_w = [b_tot[i] - bcum[i] + li[i] for i in heads]
    m_new = [jnp.maximum(b_tot[i] + m_prev[i], jnp.max(log_w[i], axis=0, keepdims=True))
             for i in heads]
    w_end = [jnp.exp(log_w[i] - m_new[i]) for i in heads]
    s_old = [jnp.exp(b_tot[i] + m_prev[i] - m_new[i]) for i in heads]
    ct_new = [s_old[i] * ct[i] + _dot_tn(kb[i], (w_end[i] * v[i]).astype(BF16)) for i in heads]
    for i in heads:
        ct_ref[i] = ct_new[i]
        n_ref[i] = jnp.broadcast_to(
            s_old[i] * nvec[i] + jnp.sum(w_end[i] * k[i], axis=0, keepdims=True), n_ref.shape[1:])
        m_ref[i] = jnp.broadcast_to(m_new[i], m_ref.shape[1:])
        hn = h[i] * lax.rsqrt(jnp.mean(h[i] * h[i], axis=-1, keepdims=True) + ML_EPS)
        cols = slice(i * ML_DV, (i + 1) * ML_DV)
        o_ref[:, cols] = (jax.nn.sigmoid(og_ref[:, cols]) * hn * nw_ref[:, cols]).astype(o_ref.dtype)


def _mlstm_mixer(proj, bsz, seq, conv_w, conv_b, i_bias, f_bias, norm_w, tri):
    tb = TIME_BLOCK
    nc = seq // tb
    hg = ML_GROUP
    n_groups = ML_HEADS // hg
    qw, vw = hg * ML_DK, hg * ML_DV
    small_i = SSD_HEADS
    small_f = SSD_HEADS + ML_HEADS
    src = jnp.arange(LANES)[None, :, None]
    dst = jnp.arange(LANES)[None, None, :]
    grp = jnp.arange(n_groups)[:, None, None]
    head = grp * hg + dst // 2
    sel = ((dst < 2 * hg) & (((dst % 2 == 0) & (src == small_i + head))
                             | ((dst % 2 == 1) & (src == small_f + head)))).astype(BF16)
    gate_bias = jnp.stack([i_bias.reshape(n_groups, hg), f_bias.reshape(n_groups, hg)], axis=-1)
    gate_bias = jnp.pad(gate_bias.reshape(n_groups, 1, 2 * hg), ((0, 0), (0, 0), (0, LANES - 2 * hg)))
    conv_b2 = _row2(conv_b)
    kq = n_groups
    row = lambda b, h, c: b * nc + c
    in_specs = [
        pl.BlockSpec((tb, qw), lambda b, h, c: (row(b, h, c), C_QK // qw + h)),
        pl.BlockSpec((tb, qw), lambda b, h, c: (row(b, h, c), C_QK // qw + kq + h)),
        pl.BlockSpec((tb, vw), lambda b, h, c: (row(b, h, c), C_MV // vw + h)),
        pl.BlockSpec((tb, vw), lambda b, h, c: (row(b, h, c), C_MO // vw + h)),
        pl.BlockSpec((tb, LANES), lambda b, h, c: (row(b, h, c), C_SMALL // LANES)),
        pl.BlockSpec((None, LANES, LANES), lambda b, h, c: (h, 0, 0)),
        pl.BlockSpec((4, qw), lambda b, h, c: (0, h)),
        pl.BlockSpec((1, qw), lambda b, h, c: (0, h)),
        pl.BlockSpec((4, qw), lambda b, h, c: (0, kq + h)),
        pl.BlockSpec((1, qw), lambda b, h, c: (0, kq + h)),
        pl.BlockSpec((None, 1, LANES), lambda b, h, c: (h, 0, 0)),
        pl.BlockSpec((1, vw), lambda b, h, c: (0, h)),
        pl.BlockSpec((tb, tb), lambda b, h, c: (0, 0)),
        pl.BlockSpec((tb, 3 * (tb + CONV_HALO)), lambda b, h, c: (0, 0)),
    ]
    return pl.pallas_call(
        _mlstm_body,
        grid=(bsz, n_groups, nc),
        in_specs=in_specs,
        out_specs=pl.BlockSpec((tb, vw), lambda b, h, c: (row(b, h, c), h)),
        out_shape=jax.ShapeDtypeStruct((bsz * seq, D_MODEL), BF16),
        scratch_shapes=[pltpu.VMEM((tb + CONV_HALO, qw), F32),
                        pltpu.VMEM((tb + CONV_HALO, qw), F32),
                        pltpu.VMEM((hg, ML_DK, ML_DV), F32),
                        pltpu.VMEM((hg, HALO, LANES), F32),
                        pltpu.VMEM((hg, HALO, LANES), F32)],
        compiler_params=_cparams(3),
        name="mlstm_mixer",
    )(proj, proj, proj, proj, proj, sel, conv_w, conv_b2, conv_w, conv_b2, gate_bias,
      _row2(norm_w), tri, _conv_shift_matrix(tb))


def _stack_heads(x):
    lane = lax.broadcasted_iota(jnp.int32, x.shape, 1)
    return jnp.concatenate([jnp.where(lane < RWKV_HEAD_DIM, x, 0.0),
                            jnp.where(lane >= RWKV_HEAD_DIM, x, 0.0)], axis=0)


def _shift_lerp(buf, src_ref, mu_ref, rows):
    buf[HALO:HALO + rows, :] = src_ref[...]
    cur = buf[HALO:HALO + rows, :]
    prev = buf[HALO - 1:HALO - 1 + rows, :]
    buf[0:HALO, :] = buf[rows:rows + HALO, :]
    return cur + (prev - cur) * mu_ref[...]


RWKV_GROUP = 8
RWKV_TIME_BLOCK = 256


def _rwkv_body(r_ref, k_ref, v_ref, lo_ref, mur_ref, muk_ref, muv_ref, mulo_ref, w0_ref, w2_ref,
               a0_ref, a2_ref, g2_ref, kk_ref, ka_ref, rk_ref, lnw_ref, lnb_ref, tri_ref,
               ones_ref, o_ref, rbuf, kbuf, vbuf, lobuf, z_ref, y_buf):
    rows = r_ref.shape[0]
    n_pairs = r_ref.shape[1] // LANES
    ch = RWKV_CHUNK
    hd = RWKV_HEAD_DIM

    @pl.when(pl.program_id(2) == 0)
    def _():
        for buf in (rbuf, kbuf, vbuf, lobuf):
            buf[0:HALO, :] = jnp.zeros((HALO, buf.shape[1]), F32)
        z_ref[...] = jnp.zeros(z_ref.shape, F32)

    r = _shift_lerp(rbuf, r_ref, mur_ref, rows)
    k = _shift_lerp(kbuf, k_ref, muk_ref, rows)
    v = _shift_lerp(vbuf, v_ref, muv_ref, rows)
    lo = _shift_lerp(lobuf, lo_ref, mulo_ref, rows)
    w_lo, a_lo, g_lo = lo[:, 0:LANES], lo[:, LANES:2 * LANES], lo[:, 2 * LANES:]

    log_w = -jax.nn.softplus(-(w0_ref[...] + _mm(jnp.tanh(w_lo), w2_ref[...], 3))) - 0.5
    lw = -jnp.exp(log_w)
    a = jax.nn.sigmoid(a0_ref[...] + _mm(a_lo, a2_ref[...], 3))
    gate = _mm(jax.nn.sigmoid(g_lo), g2_ref[...], 1)

    head_sum = ones_ref[...]

    def head_total(x):
        return jnp.concatenate([_dotx_l(x[:, p * LANES:(p + 1) * LANES], head_sum, 2)
                                for p in range(n_pairs)], axis=1)

    kk = k * kk_ref[...]
    kk = kk * lax.rsqrt(jnp.maximum(head_total(kk * kk), 1e-24))
    kh = k * (1.0 + (a - 1.0) * ka_ref[...])
    kb = kk * a
    cum = _dotx_r(tri_ref[...], lw, 3)

    c_row = lax.broadcasted_iota(jnp.int32, (ch, LANES), 0)
    c_col = lax.broadcasted_iota(jnp.int32, (ch, LANES), 1) % hd
    strict = c_col < c_row
    incl = c_col <= c_row
    same_sub = (c_col // RWKV_SUB) == (c_row // RWKV_SUB)
    eye_pair = (c_col == c_row).astype(F32)
    s_row = lax.broadcasted_iota(jnp.int32, (LANES, LANES), 0)
    s_col = lax.broadcasted_iota(jnp.int32, (LANES, LANES), 1)
    same_head = (s_row // hd) == (s_col // hd)
    eye_full = (s_row == s_col).astype(F32)

    n_chunks = rows // ch
    cat = jnp.concatenate
    mm = functools.partial(_mm, passes=1)
    a_hat, r_hat, k_til, b_til, k_end, b_end, vp, decay_end = [], [], [], [], [], [], [], []
    for j in range(n_chunks):
        sl = slice(j * ch, (j + 1) * ch)
        cm, lwj, kkj, khj, kbj, rj, vj = cum[sl], lw[sl], kk[sl], kh[sl], kb[sl], r[sl], v[sl]
        cm_end = cm[ch - 1:ch, :]
        e_neg = jnp.exp(-cm)
        e_end = jnp.exp(cm_end - cm)
        wide = (-kkj * jnp.exp(cm - lwj), rj * jnp.exp(cm), khj * e_neg, kbj * e_neg,
                khj * e_end, kbj * e_end, vj, jnp.exp(cm_end))
        for pi in range(n_pairs):
            ls = slice(pi * LANES, (pi + 1) * LANES)
            for dst, src in zip((a_hat, r_hat, k_til, b_til, k_end, b_end, vp, decay_end), wide):
                dst.append(src[:, ls])
    units = range(n_chunks * n_pairs)
    each = lambda fn: [fn(u) for u in units]

    big = each(lambda u: mm(cat([a_hat[u], r_hat[u]], axis=0),
                            cat([_stack_heads(k_til[u]), _stack_heads(b_til[u])], axis=0),
                            dot=_dot_nt))
    l_ak = each(lambda u: jnp.where(strict, big[u][0:ch, 0:LANES], 0.0))
    l_ab = each(lambda u: jnp.where(strict, big[u][0:ch, LANES:], 0.0))
    m_rk = each(lambda u: jnp.where(incl, big[u][ch:, 0:LANES], 0.0))
    m_rb = each(lambda u: jnp.where(incl, big[u][ch:, LANES:], 0.0))

    l_d = each(lambda u: jnp.where(same_sub, l_ab[u], 0.0))
    l_o = each(lambda u: l_ab[u] - l_d[u])
    x = each(lambda u: eye_pair + l_d[u])
    p = each(lambda u: mm(l_d[u], _stack_heads(l_d[u])))
    for _ in range(2):
        both = each(lambda u: mm(cat([p[u], x[u]], axis=0), _stack_heads(p[u])))
        x = each(lambda u: x[u] + both[u][ch:])
        p = each(lambda u: both[u][:ch])
    t_d = each(lambda u: x[u] + mm(x[u], _stack_heads(p[u])))
    nn = each(lambda u: mm(t_d[u], _stack_heads(l_o[u])))
    n2 = each(lambda u: mm(nn[u], _stack_heads(nn[u])))
    mx = each(lambda u: eye_pair + nn[u])
    mx = each(lambda u: mx[u] + mm(mx[u], _stack_heads(n2[u])))
    t = each(lambda u: mm(mx[u], _stack_heads(t_d[u])))

    lmv = each(lambda u: mm(cat([l_ak[u], m_rk[u]], axis=0), _stack_heads(vp[u])))
    wu = each(lambda u: mm(t[u], cat([_stack_heads(a_hat[u]), _stack_heads(lmv[u][:ch])], axis=1)))
    w_a = each(lambda u: wu[u][:, 0:LANES])
    u_v = each(lambda u: wu[u][:, LANES:])
    mw = each(lambda u: mm(m_rb[u], cat([_stack_heads(w_a[u]), _stack_heads(u_v[u])], axis=1)))
    q_eff = each(lambda u: r_hat[u] + mw[u][:, 0:LANES])
    y_loc = each(lambda u: lmv[u][ch:] + mw[u][:, LANES:])
    a_eff = each(lambda u: jnp.where(same_head, mm(b_end[u], w_a[u], dot=_dot_tn), 0.0)
                 + eye_full * decay_end[u])
    z_loc = each(lambda u: jnp.where(
        same_head, mm(cat([k_end[u], b_end[u]], axis=0), cat([vp[u], u_v[u]], axis=0),
                      dot=_dot_tn), 0.0))

    z = [z_ref[pi] for pi in range(n_pairs)]
    for j in range(n_chunks):
        sl = slice(j * ch, (j + 1) * ch)
        us = [j * n_pairs + pi for pi in range(n_pairs)]
        qz = [mm(cat([q_eff[u], a_eff[u]], axis=0), z[pi]) for pi, u in enumerate(us)]
        for pi, u in enumerate(us):
            y_buf[sl, pi * LANES:(pi + 1) * LANES] = qz[pi][:ch] + y_loc[u]
        z = [qz[pi][ch:] + z_loc[u] for pi, u in enumerate(us)]
    for pi in range(n_pairs):
        z_ref[pi] = z[pi]

    y = y_buf[...]
    mean = head_total(y) * (1.0 / hd)
    yc = y - mean
    var = head_total(yc * yc) * (1.0 / hd)
    yn = yc * lax.rsqrt(var + RWKV_LN_EPS) * lnw_ref[...] + lnb_ref[...]
    bonus = head_total(r * kh * rk_ref[...]) * v
    o_ref[...] = ((yn + bonus) * gate).astype(o_ref.dtype)


def _rwkv_mixer(proj, bsz, seq, mu, w0, w2, a0, a2, g2, k_k, k_a, r_k, ln_w, ln_b):
    tb = RWKV_TIME_BLOCK
    nc = seq // tb
    d = D_MODEL
    lora_w = 2 * LANES + RWKV_LORA_G
    mur, muk, muv = _row2(mu[0:d]), _row2(mu[d:2 * d]), _row2(mu[2 * d:3 * d])
    o = 3 * d
    pad_l = lambda x, n: jnp.pad(x, (0, n - x.shape[0]))
    mulo = _row2(jnp.concatenate([pad_l(mu[o:o + RWKV_LORA_W], LANES),
                                  pad_l(mu[o + RWKV_LORA_W:o + RWKV_LORA_W + RWKV_LORA_A], LANES),
                                  mu[o + RWKV_LORA_W + RWKV_LORA_A:]]))
    w2p = jnp.pad(w2, ((0, LANES - RWKV_LORA_W), (0, 0)))
    a2p = jnp.pad(a2, ((0, LANES - RWKV_LORA_A), (0, 0)))
    ch = RWKV_CHUNK
    t_i = jnp.arange(tb)[:, None]
    t_j = jnp.arange(tb)[None, :]
    tri = ((t_i >= t_j) & (t_i // ch == t_j // ch)).astype(BF16)
    l_i = jnp.arange(LANES)[:, None]
    l_j = jnp.arange(LANES)[None, :]
    head_sum = (l_i // RWKV_HEAD_DIM == l_j // RWKV_HEAD_DIM).astype(BF16)

    gw = RWKV_GROUP * LANES
    row = lambda b, p, c: b * nc + c
    col = lambda base: (lambda b, p, c: (row(b, p, c), base // gw + p))
    vec = pl.BlockSpec((1, gw), lambda b, p, c: (0, p))
    in_specs = [
        pl.BlockSpec((tb, gw), col(C_R)),
        pl.BlockSpec((tb, gw), col(C_K)),
        pl.BlockSpec((tb, gw), col(C_V)),
        pl.BlockSpec((tb, lora_w), lambda b, p, c: (row(b, p, c), C_LORA // lora_w)),
        vec, vec, vec,
        pl.BlockSpec((1, lora_w), lambda b, p, c: (0, 0)),
        vec,
        pl.BlockSpec((LANES, gw), lambda b, p, c: (0, p)),
        vec,
        pl.BlockSpec((LANES, gw), lambda b, p, c: (0, p)),
        pl.BlockSpec((RWKV_LORA_G, gw), lambda b, p, c: (0, p)),
        vec, vec, vec, vec, vec,
        pl.BlockSpec((tb, tb), lambda b, p, c: (0, 0)),
        pl.BlockSpec((LANES, LANES), lambda b, p, c: (0, 0)),
    ]
    return pl.pallas_call(
        _rwkv_body,
        grid=(bsz, RWKV_PAIRS // RWKV_GROUP, nc),
        in_specs=in_specs,
        out_specs=pl.BlockSpec((tb, gw), lambda b, p, c: (row(b, p, c), p)),
        out_shape=jax.ShapeDtypeStruct((bsz * seq, D_MODEL), BF16),
        scratch_shapes=[pltpu.VMEM((tb + HALO, gw), F32),
                        pltpu.VMEM((tb + HALO, gw), F32),
                        pltpu.VMEM((tb + HALO, gw), F32),
                        pltpu.VMEM((tb + HALO, lora_w), F32),
                        pltpu.VMEM((RWKV_GROUP, LANES, LANES), F32),
                        pltpu.VMEM((tb, gw), F32)],
        compiler_params=_cparams(3),
        name="rwkv_mixer",
    )(proj, proj, proj, proj, mur, muk, muv, mulo, _row2(w0), w2p, _row2(a0), a2p, g2,
      _row2(k_k), _row2(k_a), _row2(r_k.reshape(-1)), _row2(ln_w), _row2(ln_b), tri, head_sum)


def _gated_merge_body(ys_ref, yr_ref, ym_ref, ps_ref, pr_ref, pm_ref, gs_ref, gr_ref, gm_ref, o_ref):
    acc = jax.nn.sigmoid(gs_ref[...]) * _dot(ys_ref[...], ps_ref[...])
    acc = acc + jax.nn.sigmoid(gr_ref[...]) * _dot(yr_ref[...], pr_ref[...])
    acc = acc + jax.nn.sigmoid(gm_ref[...]) * _dot(ym_ref[...], pm_ref[...])
    o_ref[...] = acc.astype(o_ref.dtype)


def _gated_merge(ys, ps, proj, tm=512, tn=1024):
    m, k = ys[0].shape
    tm = min(tm, m)
    n = ps[0].shape[1]
    y_spec = pl.BlockSpec((tm, k), lambda j, i: (i, 0))
    p_spec = pl.BlockSpec((k, tn), lambda j, i: (0, j), pipeline_mode=pl.Buffered(1))
    gate_spec = lambda col: pl.BlockSpec((tm, tn), lambda j, i: (i, col // tn + j))
    return pl.pallas_call(
        _gated_merge_body,
        grid=(n // tn, m // tm),
        in_specs=[y_spec] * 3 + [p_spec] * 3 + [gate_spec(C_GS), gate_spec(C_GR), gate_spec(C_GM)],
        out_specs=pl.BlockSpec((tm, tn), lambda j, i: (i, j)),
        out_shape=jax.ShapeDtypeStruct((m, n), BF16),
        compiler_params=_cparams(2),
        name="gated_merge",
    )(*ys, *ps, proj, proj, proj)


def _mm_res_ln_body(a_ref, w_ref, h_ref, g_ref, b_ref, o_ref, ob_ref, *, row_split):
    n = a_ref.shape[0] // row_split
    subs = [slice(s * n, (s + 1) * n) for s in range(row_split)]
    acc = [_dot(a_ref[rows, :], w_ref[...]) for rows in subs]
    for rows, mix in zip(subs, acc):
        h = _layer_norm_rows(ALPHA * h_ref[rows, :] + mix, g_ref[...], b_ref[...])
        o_ref[rows, :] = h
        ob_ref[rows, :] = h.astype(BF16)


def _matmul_res_ln(a, w, h, g, b, tm, row_split):
    m, k = a.shape
    n = w.shape[1]
    const = lambda shape: pl.BlockSpec(shape, lambda i: (0, 0), pipeline_mode=pl.Buffered(1))
    return pl.pallas_call(
        functools.partial(_mm_res_ln_body, row_split=row_split),
        grid=(m // tm,),
        in_specs=[pl.BlockSpec((tm, k), lambda i: (i, 0)),
                  const((k, n)),
                  pl.BlockSpec((tm, n), lambda i: (i, 0)),
                  const((1, n)), const((1, n))],
        out_specs=[pl.BlockSpec((tm, n), lambda i: (i, 0)),
                   pl.BlockSpec((tm, n), lambda i: (i, 0))],
        out_shape=[jax.ShapeDtypeStruct((m, n), F32), jax.ShapeDtypeStruct((m, n), BF16)],
        compiler_params=_cparams(1),
        name="matmul_res_ln",
    )(a, w, h, _row2(g), _row2(b))


FFN_HALO = 16


def _ffn_up_body(h_ref, hp_ref, wg_ref, wua_ref, wub_ref, cwg_ref, cbg_ref, cwu_ref, cbu_ref, o_ref,
                 gbuf, ubuf, wg_s, wu_s, *, blocks_per_seq):
    rows = h_ref.shape[0]
    tn = o_ref.shape[1]
    first = (pl.program_id(1) % blocks_per_seq) == 0
    keep = jnp.where(first, 0.0, 1.0)

    @pl.when(pl.program_id(1) == 0)
    def _():
        col = pl.program_id(0) * tn + lax.broadcasted_iota(jnp.int32, wg_s.shape, 1)
        off = FFN_D % tn
        up = jnp.concatenate([wua_ref[:, off:], wub_ref[:, :off]], axis=1)
        wg_s[...] = jnp.where(col < FFN_D, wg_ref[...], 0.0).astype(BF16)
        wu_s[...] = jnp.where(col < FFN_D, up, 0.0).astype(BF16)

    def conv(buf, cw_ref, cb_ref):
        acc = cb_ref[...] + cw_ref[2:3, :] * buf[FFN_HALO:FFN_HALO + rows, :]
        acc = acc + cw_ref[1:2, :] * buf[FFN_HALO - 1:FFN_HALO - 1 + rows, :]
        return acc + cw_ref[0:1, :] * buf[FFN_HALO - 2:FFN_HALO - 2 + rows, :]

    for buf, w_ref in ((gbuf, wg_s), (ubuf, wu_s)):
        buf[0:FFN_HALO, :] = _dot(hp_ref[...], w_ref[...]) * keep
        buf[FFN_HALO:FFN_HALO + rows, :] = _dot(h_ref[...], w_ref[...])
    gate = conv(gbuf, cwg_ref, cbg_ref)
    up = conv(ubuf, cwu_ref, cbu_ref)
    gelu = 0.5 * gate * (1.0 + lax.erf(gate * (2.0 ** -0.5)))
    o_ref[...] = (gelu * up).astype(o_ref.dtype)


def _ffn_up(hb, w_up, layer, cwg, cbg, cwu, cbu, seq, tm=1024):
    m, k = hb.shape
    n = FFN_PAD
    tn = FFN_TN
    halo_blocks = tm // FFN_HALO
    up0 = FFN_D // tn
    body = functools.partial(_ffn_up_body, blocks_per_seq=seq // tm)
    w_spec = lambda first_block: pl.BlockSpec((None, k, tn), lambda j, i: (layer, 0, first_block + j))
    return pl.pallas_call(
        body,
        grid=(n // tn, m // tm),
        in_specs=[pl.BlockSpec((tm, k), lambda j, i: (i, 0)),
                  pl.BlockSpec((FFN_HALO, k), lambda j, i: (jnp.maximum(i * halo_blocks - 1, 0), 0)),
                  w_spec(0), w_spec(up0), w_spec(up0 + 1),
                  pl.BlockSpec((3, tn), lambda j, i: (0, j)),
                  pl.BlockSpec((1, tn), lambda j, i: (0, j)),
                  pl.BlockSpec((3, tn), lambda j, i: (0, j)),
                  pl.BlockSpec((1, tn), lambda j, i: (0, j))],
        out_specs=pl.BlockSpec((tm, tn), lambda j, i: (i, j)),
        out_shape=jax.ShapeDtypeStruct((m, n), BF16),
        scratch_shapes=[pltpu.VMEM((tm + FFN_HALO, tn), F32),
                        pltpu.VMEM((tm + FFN_HALO, tn), F32),
                        pltpu.VMEM((k, tn), BF16),
                        pltpu.VMEM((k, tn), BF16)],
        compiler_params=_cparams(2),
        name="ffn_up",
    )(hb, hb, w_up, w_up, w_up, cwg, cbg, cwu, cbu)


def _pack_segments():
    d = D_MODEL
    order = (("z", d), ("x", d), ("bc", 2 * SSD_GN), ("dt", SSD_HEADS), ("r", d), ("k", d), ("v", d),
             ("w_lo", RWKV_LORA_W), ("a_lo", RWKV_LORA_A), ("g_lo", RWKV_LORA_G), ("qk", d),
             ("mv", d), ("mo", d), ("gi", ML_HEADS), ("gf", ML_HEADS), ("gs", d), ("gr", d), ("gm", d))
    dst = dict(z=C_Z, x=C_X, r=C_R, k=C_K, v=C_V, qk=C_QK, mv=C_MV, mo=C_MO, gs=C_GS, gr=C_GR,
               gm=C_GM, bc=C_BC, w_lo=C_LORA, a_lo=C_LORA + LANES, g_lo=C_LORA + 2 * LANES,
               dt=C_SMALL, gi=C_SMALL + SSD_HEADS, gf=C_SMALL + SSD_HEADS + ML_HEADS)
    segments, src = [], 0
    for name, width in order:
        segments.append((src, width, dst[name]))
        src += width
    return tuple(segments), src


def _pack_body(wt_ref, o_ref, *, segments):
    group, start, pos = [], 0, 0

    def flush():
        nonlocal group, start
        rows = group[0] if len(group) == 1 else jnp.concatenate(group, axis=0)
        o_ref[:, start:pos] = rows.T.astype(o_ref.dtype)
        group, start = [], pos

    for src, width, dst in sorted(segments, key=lambda s: s[2]) + [(None, 0, N_PROJ)]:
        if dst > pos:
            group.append(jnp.zeros((dst - pos, LANES), F32))
            pos = dst
            if pos % LANES == 0:
                flush()
        if width:
            group.append(wt_ref[src:src + width, :])
            pos += width
            if pos % LANES == 0:
                flush()


def _pack_w_in(w_in, layer):
    segments, d_in = _pack_segments()
    d = w_in.shape[1]
    return pl.pallas_call(
        functools.partial(_pack_body, segments=segments),
        grid=(d // LANES,),
        in_specs=[pl.BlockSpec((None, d_in, LANES), lambda i: (layer, 0, i))],
        out_specs=pl.BlockSpec((LANES, N_PROJ), lambda i: (i, 0)),
        out_shape=jax.ShapeDtypeStruct((d, N_PROJ), BF16),
        compiler_params=_cparams(1),
        name="pack_w_in",
    )(jnp.swapaxes(w_in, 1, 2))


def _to_bf16_body(w_ref, o_ref, *, valid_rows):
    x = w_ref[...]
    if valid_rows is not None:
        row = pl.program_id(0) * o_ref.shape[0] + lax.broadcasted_iota(jnp.int32, o_ref.shape, 0)
        x = jnp.where(row < valid_rows, x, 0.0)
    o_ref[...] = x.astype(o_ref.dtype)


def _to_bf16(w_stack, layer, out_rows=None, rows_per_step=512):
    _, rows, cols = w_stack.shape
    out_rows = rows if out_rows is None else out_rows
    tr = rows_per_step
    last = (rows - 1) // tr
    return pl.pallas_call(
        functools.partial(_to_bf16_body, valid_rows=None if out_rows == rows else rows),
        grid=(out_rows // tr,),
        in_specs=[pl.BlockSpec((None, tr, cols), lambda i: (layer, jnp.minimum(i, last), 0))],
        out_specs=pl.BlockSpec((tr, cols), lambda i: (i, 0)),
        out_shape=jax.ShapeDtypeStruct((out_rows, cols), BF16),
        compiler_params=_cparams(1),
        name="to_bf16",
    )(w_stack)


def _pad_cols(x, n):
    return jnp.pad(x, ((0, 0), (0, n - x.shape[1])))


def kernel(x, ln_in_g, ln_in_b, w_in, ssd_conv_w, ssd_conv_b, ssd_dt_bias, ssd_a_log, ssd_d, ssd_norm_w, rwkv_mu, rwkv_w0, rwkv_w2, rwkv_a0, rwkv_a2, rwkv_g2, rwkv_k_k, rwkv_k_a, rwkv_r_k, rwkv_ln_w, rwkv_ln_b, mlstm_conv_w, mlstm_conv_b, mlstm_i_bias, mlstm_f_bias, mlstm_norm_w, proj_ssd, proj_rwkv, proj_mlstm, w_out, ln1_g, ln1_b, ffn_w_up, ffn_conv_w, ffn_conv_b, ffn_w_down, ln2_g, ln2_b):
    bsz, seq, d = x.shape
    n_tok = bsz * seq
    tb = TIME_BLOCK
    t_i = jnp.arange(tb)[:, None]
    t_j = jnp.arange(tb)[None, :]
    tri = (t_i >= t_j).astype(BF16)

    h, hb = _ln_in(x.reshape(n_tok, d), ln_in_g, ln_in_b)
    for l in range(DEPTH):
        proj = _matmul(hb, _pack_w_in(w_in, l), tm=min(1024, n_tok), tn=2048, out_dtype=F32,
                       name="in_proj")
        y_ssd = _ssd_mixer(proj, bsz, seq, ssd_conv_w[l], ssd_conv_b[l], ssd_dt_bias[l],
                           ssd_a_log[l], ssd_d[l], ssd_norm_w[l], tri)
        y_rwkv = _rwkv_mixer(proj, bsz, seq, rwkv_mu[l], rwkv_w0[l], rwkv_w2[l], rwkv_a0[l],
                             rwkv_a2[l], rwkv_g2[l], rwkv_k_k[l], rwkv_k_a[l], rwkv_r_k[l],
                             rwkv_ln_w[l], rwkv_ln_b[l])
        y_mlstm = _mlstm_mixer(proj, bsz, seq, mlstm_conv_w[l], mlstm_conv_b[l], mlstm_i_bias[l],
                               mlstm_f_bias[l], mlstm_norm_w[l], tri)
        merged = _gated_merge(
            (y_ssd, y_rwkv, y_mlstm),
            (_to_bf16(proj_ssd, l), _to_bf16(proj_rwkv, l), _to_bf16(proj_mlstm, l)), proj)
        h, hb = _matmul_res_ln(merged, _to_bf16(w_out, l), h, ln1_g[l], ln1_b[l],
                               tm=min(512, n_tok), row_split=2)

        cwg = _pad_cols(ffn_conv_w[l][:, :FFN_D], FFN_PAD)
        cwu = _pad_cols(ffn_conv_w[l][:, FFN_D:], FFN_PAD)
        cbg = _pad_cols(_row2(ffn_conv_b[l][:FFN_D]), FFN_PAD)
        cbu = _pad_cols(_row2(ffn_conv_b[l][FFN_D:]), FFN_PAD)
        act = _ffn_up(hb, ffn_w_up, l, cwg, cbg, cwu, cbu, seq, tm=min(1024, seq))
        wd = _to_bf16(ffn_w_down, l, out_rows=FFN_PAD)
        h, hb = _matmul_res_ln(act, wd, h, ln2_g[l], ln2_b[l], tm=min(256, n_tok), row_split=1)
    return h.reshape(bsz, seq, d)
```

```python
import functools

import jax
import jax.numpy as jnp
from jax import lax
from jax.experimental import pallas as pl
from jax.experimental.pallas import tpu as pltpu

F32 = jnp.float32
BF16 = jnp.bfloat16

D_MODEL = 2048
DEPTH = 2
ALPHA = (2.0 * DEPTH) ** 0.25
LN_EPS = 1e-5

SSD_HEADS = 32
SSD_HEAD_DIM = 64
SSD_GROUPS = 4
SSD_STATE = 128
SSD_GN = SSD_GROUPS * SSD_STATE
SSD_EPS = 1e-5
SSD_GROUP_W = D_MODEL // SSD_GROUPS
SSD_HPG = SSD_HEADS // SSD_GROUPS

RWKV_HEAD_DIM = 64
RWKV_PAIRS = D_MODEL // 128
RWKV_LORA_W = 96
RWKV_LORA_A = 96
RWKV_LORA_G = 256
RWKV_LN_EPS = 64e-5
RWKV_CHUNK = 64
RWKV_SUB = 16

ML_HEADS = 8
ML_DK = 128
ML_DV = 256
ML_EPS = 1e-6
ML_GROUP = 8

FFN_D = 5504
FFN_PAD = 5632
FFN_TN = 512

LANES = 128
HALO = 8
TIME_BLOCK = 128

C_Z = 0
C_X = 2048
C_R = 4096
C_K = 6144
C_V = 8192
C_QK = 10240
C_MV = 12288
C_MO = 14336
C_GS = 16384
C_GR = 18432
C_GM = 20480
C_BC = 22528
C_LORA = 23552
C_SMALL = 24064
N_PROJ = 24576

VMEM_LIMIT = 56 * 1024 * 1024

TM_LN_IN = 512
TM_IN_PROJ, TN_IN_PROJ = 1024, 2048
TM_GATED, TN_GATED = 512, 1024
TM_W_OUT = 512
TM_FFN_UP = 1024
TM_FFN_DOWN = 256


def _dot(a, b):
    return jnp.dot(a, b, preferred_element_type=F32)


def _dot_nt(a, b):
    return lax.dot_general(a, b, (((1,), (1,)), ((), ())), preferred_element_type=F32)


def _dot_tn(a, b):
    return lax.dot_general(a, b, (((0,), (0,)), ((), ())), preferred_element_type=F32)


def _split(x, n):
    parts, r = [], x
    for i in range(n):
        p = r.astype(BF16)
        parts.append(p)
        if i + 1 < n:
            r = r - p.astype(F32)
    return parts


def _dotx_l(x, w_exact, n, dot=_dot):
    acc = None
    for p in _split(x, n):
        t = dot(p, w_exact)
        acc = t if acc is None else acc + t
    return acc


def _dotx_r(w_exact, x, n, dot=_dot):
    acc = None
    for p in _split(x, n):
        t = dot(w_exact, p)
        acc = t if acc is None else acc + t
    return acc


def _mm(a, b, passes, dot=_dot):
    if passes == 1:
        return dot(a.astype(BF16), b.astype(BF16))
    ah, al = _split(a, 2)
    bh, bl = _split(b, 2)
    return dot(ah, bh) + (dot(ah, bl) + dot(al, bh))


def _layer_norm_rows(x, g, b):
    mu = jnp.mean(x, axis=-1, keepdims=True)
    xc = x - mu
    var = jnp.mean(xc * xc, axis=-1, keepdims=True)
    return xc * lax.rsqrt(var + LN_EPS) * g + b


def _cparams(n_axes):
    return pltpu.CompilerParams(dimension_semantics=("arbitrary",) * n_axes,
                                vmem_limit_bytes=VMEM_LIMIT)


def _row2(v):
    return v.reshape(1, -1)


def _ln_in_body(x_ref, g_ref, b_ref, h_ref, hb_ref):
    h = _layer_norm_rows(x_ref[...], g_ref[...], b_ref[...])
    h_ref[...] = h
    hb_ref[...] = h.astype(BF16)


def _ln_in(x2d, g, b, tm=TM_LN_IN):
    n, d = x2d.shape
    tm = min(tm, n)
    return pl.pallas_call(
        _ln_in_body,
        grid=(n // tm,),
        in_specs=[pl.BlockSpec((tm, d), lambda i: (i, 0)),
                  pl.BlockSpec((1, d), lambda i: (0, 0)),
                  pl.BlockSpec((1, d), lambda i: (0, 0))],
        out_specs=[pl.BlockSpec((tm, d), lambda i: (i, 0)),
                   pl.BlockSpec((tm, d), lambda i: (i, 0))],
        out_shape=[jax.ShapeDtypeStruct((n, d), F32), jax.ShapeDtypeStruct((n, d), BF16)],
        compiler_params=_cparams(1),
        name="ln_in",
    )(x2d, _row2(g), _row2(b))


def _mm_body(a_ref, w_ref, o_ref):
    o_ref[...] = _dot(a_ref[...], w_ref[...]).astype(o_ref.dtype)


def _matmul(a, w, tm, tn, out_dtype, name):
    m, k = a.shape
    _, n = w.shape
    return pl.pallas_call(
        _mm_body,
        grid=(n // tn, m // tm),
        in_specs=[pl.BlockSpec((tm, k), lambda j, i: (i, 0)),
                  pl.BlockSpec((k, tn), lambda j, i: (0, j))],
        out_specs=pl.BlockSpec((tm, tn), lambda j, i: (i, j)),
        out_shape=jax.ShapeDtypeStruct((m, n), out_dtype),
        compiler_params=_cparams(2),
        name=name,
    )(a, w)


CONV_HALO = 16


def _conv_shift_matrix(rows):
    t = jnp.arange(rows)[:, None]
    c = jnp.arange(3 * (rows + CONV_HALO))[None, :]
    k, j = c // (rows + CONV_HALO), c % (rows + CONV_HALO)
    return (j == CONV_HALO + t - 3 + k).astype(BF16)


def _conv4_silu(buf, src_ref, w_ref, b_ref, shift_ref, rows):
    x = src_ref[...]
    buf[CONV_HALO:CONV_HALO + rows, :] = x
    hist = buf[...]
    taps = jnp.concatenate([(hist * w_ref[k:k + 1, :]).astype(BF16) for k in range(3)], axis=0)
    acc = b_ref[...] + w_ref[3:4, :] * x + _dot(shift_ref[...], taps)
    buf[0:CONV_HALO, :] = buf[rows:rows + CONV_HALO, :]
    return jax.nn.silu(acc)


def _ssd_body(z_ref, x_ref, bc_ref, sm_ref, cwx_ref, cbx_ref, cwbc_ref, cbbc_ref, dtb_ref,
              alog_ref, dsk_ref, nw_ref, e_ref, tri_ref, shift_ref, o_ref, xbuf, bcbuf, st_ref):
    rows = x_ref.shape[0]
    gw = SSD_GROUP_W
    groups = range(SSD_GROUPS)

    @pl.when(pl.program_id(1) == 0)
    def _():
        xbuf[0:CONV_HALO, :] = jnp.zeros((CONV_HALO, xbuf.shape[1]), F32)
        bcbuf[0:CONV_HALO, :] = jnp.zeros((CONV_HALO, bcbuf.shape[1]), F32)
        st_ref[...] = jnp.zeros(st_ref.shape, F32)

    xs_all = _conv4_silu(xbuf, x_ref, cwx_ref, cbx_ref, shift_ref, rows)
    bc = _conv4_silu(bcbuf, bc_ref, cwbc_ref, cbbc_ref, shift_ref, rows)

    dt = jax.nn.softplus(sm_ref[...] + dtb_ref[...])
    adt = dt * (-jnp.exp(alog_ref[...]))
    a_cs = _dotx_r(tri_ref[...], adt, 3)
    a_cs_t = a_cs.T

    t_row = lax.broadcasted_iota(jnp.int32, (rows, rows), 0)
    t_col = lax.broadcasted_iota(jnp.int32, (rows, rows), 1)
    causal = t_row >= t_col
    lane = lax.broadcasted_iota(jnp.int32, (rows, LANES), 1)

    xs = [xs_all[:, g * gw:(g + 1) * gw] for g in groups]
    bm = [bc[:, g * SSD_STATE:(g + 1) * SSD_STATE].astype(BF16) for g in groups]
    cm = [bc[:, SSD_GN + g * SSD_STATE:SSD_GN + (g + 1) * SSD_STATE].astype(BF16) for g in groups]
    a_exp = [_dotx_l(a_cs, e_ref[g], 3) for g in groups]
    dt_exp = [_dotx_l(dt, e_ref[g], 3) for g in groups]
    scores = [_dot_nt(cm[g], bm[g]) for g in groups]
    st = [st_ref[g] for g in groups]
    y_off = [_dot(cm[g], st[g].astype(BF16)) for g in groups]
    xdt = [xs[g] * dt_exp[g] for g in groups]

    y_diag = []
    for g in groups:
        parts = []
        for pr in range(SSD_HPG // 2):
            xp = xdt[g][:, pr * LANES:(pr + 1) * LANES]
            acc = None
            for s in range(2):
                hh = g * SSD_HPG + 2 * pr + s
                seg = a_cs[:, hh:hh + 1] - a_cs_t[hh:hh + 1, :]
                m = jnp.where(causal, jnp.exp(seg), 0.0) * scores[g]
                head_lanes = (lane < SSD_HEAD_DIM) if s == 0 else (lane >= SSD_HEAD_DIM)
                xm = jnp.where(head_lanes, xp, 0.0)
                t = _dot(m.astype(BF16), xm.astype(BF16))
                acc = t if acc is None else acc + t
            parts.append(acc)
        y_diag.append(jnp.concatenate(parts, axis=1))

    for g in groups:
        cols = slice(g * gw, (g + 1) * gw)
        a_last = a_exp[g][rows - 1:rows, :]
        st_ref[g] = jnp.exp(a_last) * st[g] + _dot_tn(
            bm[g], (xdt[g] * jnp.exp(a_last - a_exp[g])).astype(BF16))
        y = y_diag[g] + y_off[g] * jnp.exp(a_exp[g]) + xs[g] * dsk_ref[:, cols]
        y = y * jax.nn.silu(z_ref[:, cols])
        ms = jnp.mean(y * y, axis=-1, keepdims=True)
        o_ref[:, cols] = (y * lax.rsqrt(ms + SSD_EPS) * nw_ref[:, cols]).astype(o_ref.dtype)


def _ssd_mixer(proj, bsz, seq, conv_w, conv_b, dt_bias, a_log, d_skip, norm_w, tri):
    tb = TIME_BLOCK
    nc = seq // tb
    gw = SSD_GROUP_W
    d = D_MODEL
    bcw = 2 * SSD_GN
    cwx, cwbc = conv_w[:, :d], conv_w[:, d:]
    cbx, cbbc = _row2(conv_b[:d]), _row2(conv_b[d:])
    head_lanes = lambda v: _row2(jnp.pad(v, (0, LANES - SSD_HEADS)))
    grp = jnp.arange(SSD_GROUPS)[:, None, None]
    src = jnp.arange(LANES)[None, :, None]
    dst = jnp.arange(gw)[None, None, :]
    expand = (src == grp * SSD_HPG + dst // SSD_HEAD_DIM).astype(BF16)

    row = lambda b, c: b * nc + c
    full = lambda shape: pl.BlockSpec(shape, lambda b, c: (0,) * len(shape))
    in_specs = [
        pl.BlockSpec((tb, d), lambda b, c: (row(b, c), C_Z // d)),
        pl.BlockSpec((tb, d), lambda b, c: (row(b, c), C_X // d)),
        pl.BlockSpec((tb, bcw), lambda b, c: (row(b, c), C_BC // bcw)),
        pl.BlockSpec((tb, LANES), lambda b, c: (row(b, c), C_SMALL // LANES)),
        full((4, d)), full((1, d)), full((4, bcw)), full((1, bcw)),
        full((1, LANES)), full((1, LANES)), full((1, d)), full((1, d)),
        full((SSD_GROUPS, LANES, gw)), full((tb, tb)), full((tb, 3 * (tb + CONV_HALO))),
    ]
    return pl.pallas_call(
        _ssd_body,
        grid=(bsz, nc),
        in_specs=in_specs,
        out_specs=pl.BlockSpec((tb, d), lambda b, c: (row(b, c), 0)),
        out_shape=jax.ShapeDtypeStruct((bsz * seq, d), BF16),
        scratch_shapes=[pltpu.VMEM((tb + CONV_HALO, d), F32),
                        pltpu.VMEM((tb + CONV_HALO, bcw), F32),
                        pltpu.VMEM((SSD_GROUPS, SSD_STATE, gw), F32)],
        compiler_params=_cparams(2),
        name="ssd_mixer",
    )(proj, proj, proj, proj, cwx, cbx, cwbc, cbbc, head_lanes(dt_bias), head_lanes(a_log),
      _row2(jnp.repeat(d_skip, SSD_HEAD_DIM)), _row2(norm_w), expand, tri, _conv_shift_matrix(tb))


def _mlstm_body(q_ref, k_ref, v_ref, og_ref, sm_ref, sel_ref, cwq_ref, cbq_ref, cwk_ref, cbk_ref,
                gb_ref, nw_ref, tri_ref, shift_ref, o_ref, qbuf, kbuf, ct_ref, n_ref, m_ref):
    rows = q_ref.shape[0]

    @pl.when(pl.program_id(2) == 0)
    def _():
        qbuf[0:CONV_HALO, :] = jnp.zeros((CONV_HALO, qbuf.shape[1]), F32)
        kbuf[0:CONV_HALO, :] = jnp.zeros((CONV_HALO, kbuf.shape[1]), F32)
        ct_ref[...] = jnp.zeros(ct_ref.shape, F32)
        n_ref[...] = jnp.zeros(n_ref.shape, F32)
        m_ref[...] = jnp.zeros(m_ref.shape, F32)

    heads = range(q_ref.shape[1] // ML_DK)
    q_all = _conv4_silu(qbuf, q_ref, cwq_ref, cbq_ref, shift_ref, rows) * (ML_DK ** -0.5)
    k_all = _conv4_silu(kbuf, k_ref, cwk_ref, cbk_ref, shift_ref, rows)
    qf = [q_all[:, i * ML_DK:(i + 1) * ML_DK] for i in heads]
    q = [qf[i].astype(BF16) for i in heads]
    k = [k_all[:, i * ML_DK:(i + 1) * ML_DK] for i in heads]
    kb = [k[i].astype(BF16) for i in heads]
    v = [v_ref[:, i * ML_DV:(i + 1) * ML_DV] for i in heads]

    pre = _dotx_l(sm_ref[...], sel_ref[...], 3) + gb_ref[...]
    log_f = jax.nn.log_sigmoid(pre)
    bcum_all = _dotx_r(tri_ref[...], log_f, 3)
    lane = lax.broadcasted_iota(jnp.int32, (rows, LANES), 1)
    gates_t = jnp.where(lane % 2 == 0, pre, bcum_all).T
    t_row = lax.broadcasted_iota(jnp.int32, (rows, rows), 0)
    t_col = lax.broadcasted_iota(jnp.int32, (rows, rows), 1)
    causal = t_row >= t_col

    li = [pre[:, 2 * i:2 * i + 1] for i in heads]
    bcum = [bcum_all[:, 2 * i + 1:2 * i + 2] for i in heads]
    log_d = [jnp.where(causal, bcum[i] - gates_t[2 * i + 1:2 * i + 2, :] + gates_t[2 * i:2 * i + 1, :],
                       -jnp.inf) for i in heads]
    qk = [_dot_nt(q[i], kb[i]) for i in heads]
    ct = [ct_ref[i] for i in heads]
    inter = [_dot(q[i], ct[i].astype(BF16)) for i in heads]
    m_prev = [m_ref[i, 0:1, 0:1] for i in heads]
    m_inter = [bcum[i] + m_prev[i] for i in heads]
    m_t = [jnp.maximum(jnp.max(log_d[i], axis=-1, keepdims=True), m_inter[i]) for i in heads]
    w = [jnp.exp(log_d[i] - m_t[i]) * qk[i] for i in heads]
    s_inter = [jnp.exp(m_inter[i] - m_t[i]) for i in heads]
    nvec = [n_ref[i, 0:1, :] for i in heads]
    num = [_dot(w[i].astype(BF16), v[i].astype(BF16)) + s_inter[i] * inter[i] for i in heads]
    den = [jnp.sum(w[i], axis=-1, keepdims=True)
           + s_inter[i] * jnp.sum(qf[i] * nvec[i], axis=-1, keepdims=True)
           for i in heads]
    h = [num[i] / jnp.maximum(jnp.abs(den[i]), jnp.exp(-m_t[i])) for i in heads]

    b_tot = [bcum[i][rows - 1:rows, :] for i in heads]
    log_w = [b_tot[i] - bcum[i] + li[i] for i in heads]
    m_new = [jnp.maximum(b_tot[i] + m_prev[i], jnp.max(log_w[i], axis=0, keepdims=True))
             for i in heads]
    w_end = [jnp.exp(log_w[i] - m_new[i]) for i in heads]
    s_old = [jnp.exp(b_tot[i] + m_prev[i] - m_new[i]) for i in heads]
    ct_new = [s_old[i] * ct[i] + _dot_tn(kb[i], (w_end[i] * v[i]).astype(BF16)) for i in heads]
    for i in heads:
        ct_ref[i] = ct_new[i]
        n_ref[i] = jnp.broadcast_to(
            s_old[i] * nvec[i] + jnp.sum(w_end[i] * k[i], axis=0, keepdims=True), n_ref.shape[1:])
        m_ref[i] = jnp.broadcast_to(m_new[i], m_ref.shape[1:])
        hn = h[i] * lax.rsqrt(jnp.mean(h[i] * h[i], axis=-1, keepdims=True) + ML_EPS)
        cols = slice(i * ML_DV, (i + 1) * ML_DV)
        o_ref[:, cols] = (jax.nn.sigmoid(og_ref[:, cols]) * hn * nw_ref[:, cols]).astype(o_ref.dtype)


def _mlstm_mixer(proj, bsz, seq, conv_w, conv_b, i_bias, f_bias, norm_w, tri):
    tb = TIME_BLOCK
    nc = seq // tb
    hg = ML_GROUP
    n_groups = ML_HEADS // hg
    qw, vw = hg * ML_DK, hg * ML_DV
    small_i = SSD_HEADS
    small_f = SSD_HEADS + ML_HEADS
    src = jnp.arange(LANES)[None, :, None]
    dst = jnp.arange(LANES)[None, None, :]
    grp = jnp.arange(n_groups)[:, None, None]
    head = grp * hg + dst // 2
    sel = ((dst < 2 * hg) & (((dst % 2 == 0) & (src == small_i + head))
                             | ((dst % 2 == 1) & (src == small_f + head)))).astype(BF16)
    gate_bias = jnp.stack([i_bias.reshape(n_groups, hg), f_bias.reshape(n_groups, hg)], axis=-1)
    gate_bias = jnp.pad(gate_bias.reshape(n_groups, 1, 2 * hg), ((0, 0), (0, 0), (0, LANES - 2 * hg)))
    conv_b2 = _row2(conv_b)
    kq = n_groups
    row = lambda b, h, c: b * nc + c
    in_specs = [
        pl.BlockSpec((tb, qw), lambda b, h, c: (row(b, h, c), C_QK // qw + h)),
        pl.BlockSpec((tb, qw), lambda b, h, c: (row(b, h, c), C_QK // qw + kq + h)),
        pl.BlockSpec((tb, vw), lambda b, h, c: (row(b, h, c), C_MV // vw + h)),
        pl.BlockSpec((tb, vw), lambda b, h, c: (row(b, h, c), C_MO // vw + h)),
        pl.BlockSpec((tb, LANES), lambda b, h, c: (row(b, h, c), C_SMALL // LANES)),
        pl.BlockSpec((None, LANES, LANES), lambda b, h, c: (h, 0, 0)),
        pl.BlockSpec((4, qw), lambda b, h, c: (0, h)),
        pl.BlockSpec((1, qw), lambda b, h, c: (0, h)),
        pl.BlockSpec((4, qw), lambda b, h, c: (0, kq + h)),
        pl.BlockSpec((1, qw), lambda b, h, c: (0, kq + h)),
        pl.BlockSpec((None, 1, LANES), lambda b, h, c: (h, 0, 0)),
        pl.BlockSpec((1, vw), lambda b, h, c: (0, h)),
        pl.BlockSpec((tb, tb), lambda b, h, c: (0, 0)),
        pl.BlockSpec((tb, 3 * (tb + CONV_HALO)), lambda b, h, c: (0, 0)),
    ]
    return pl.pallas_call(
        _mlstm_body,
        grid=(bsz, n_groups, nc),
        in_specs=in_specs,
        out_specs=pl.BlockSpec((tb, vw), lambda b, h, c: (row(b, h, c), h)),
        out_shape=jax.ShapeDtypeStruct((bsz * seq, D_MODEL), BF16),
        scratch_shapes=[pltpu.VMEM((tb + CONV_HALO, qw), F32),
                        pltpu.VMEM((tb + CONV_HALO, qw), F32),
                        pltpu.VMEM((hg, ML_DK, ML_DV), F32),
                        pltpu.VMEM((hg, HALO, LANES), F32),
                        pltpu.VMEM((hg, HALO, LANES), F32)],
        compiler_params=_cparams(3),
        name="mlstm_mixer",
    )(proj, proj, proj, proj, proj, sel, conv_w, conv_b2, conv_w, conv_b2, gate_bias,
      _row2(norm_w), tri, _conv_shift_matrix(tb))


def _stack_heads(x):
    lane = lax.broadcasted_iota(jnp.int32, x.shape, 1)
    return jnp.concatenate([jnp.where(lane < RWKV_HEAD_DIM, x, 0.0),
                            jnp.where(lane >= RWKV_HEAD_DIM, x, 0.0)], axis=0)


def _shift_lerp(buf, src_ref, mu_ref, rows):
    buf[HALO:HALO + rows, :] = src_ref[...]
    cur = buf[HALO:HALO + rows, :]
    prev = buf[HALO - 1:HALO - 1 + rows, :]
    buf[0:HALO, :] = buf[rows:rows + HALO, :]
    return cur + (prev - cur) * mu_ref[...]


RWKV_GROUP = 8
RWKV_PASSES = dict(scores=1, solve=1, values=1, state=1)
RWKV_TIME_BLOCK = 256


def _rwkv_body(r_ref, k_ref, v_ref, lo_ref, mur_ref, muk_ref, muv_ref, mulo_ref, w0_ref, w2_ref,
               a0_ref, a2_ref, g2_ref, kk_ref, ka_ref, rk_ref, lnw_ref, lnb_ref, tri_ref,
               ones_ref, o_ref, rbuf, kbuf, vbuf, lobuf, z_ref, y_buf):
    rows = r_ref.shape[0]
    n_pairs = r_ref.shape[1] // LANES
    ch = RWKV_CHUNK
    hd = RWKV_HEAD_DIM

    @pl.when(pl.program_id(2) == 0)
    def _():
        for buf in (rbuf, kbuf, vbuf, lobuf):
            buf[0:HALO, :] = jnp.zeros((HALO, buf.shape[1]), F32)
        z_ref[...] = jnp.zeros(z_ref.shape, F32)

    r = _shift_lerp(rbuf, r_ref, mur_ref, rows)
    k = _shift_lerp(kbuf, k_ref, muk_ref, rows)
    v = _shift_lerp(vbuf, v_ref, muv_ref, rows)
    lo = _shift_lerp(lobuf, lo_ref, mulo_ref, rows)
    w_lo, a_lo, g_lo = lo[:, 0:LANES], lo[:, LANES:2 * LANES], lo[:, 2 * LANES:]

    log_w = -jax.nn.softplus(-(w0_ref[...] + _mm(jnp.tanh(w_lo), w2_ref[...], 3))) - 0.5
    lw = -jnp.exp(log_w)
    a = jax.nn.sigmoid(a0_ref[...] + _mm(a_lo, a2_ref[...], 3))
    gate = _mm(jax.nn.sigmoid(g_lo), g2_ref[...], 1)

    head_sum = ones_ref[...]

    def head_total(x):
        return jnp.concatenate([_dotx_l(x[:, p * LANES:(p + 1) * LANES], head_sum, 2)
                                for p in range(n_pairs)], axis=1)

    kk = k * kk_ref[...]
    kk = kk * lax.rsqrt(jnp.maximum(head_total(kk * kk), 1e-24))
    kh = k * (1.0 + (a - 1.0) * ka_ref[...])
    kb = kk * a
    cum = _dotx_r(tri_ref[...], lw, 3)

    c_row = lax.broadcasted_iota(jnp.int32, (ch, LANES), 0)
    c_col = lax.broadcasted_iota(jnp.int32, (ch, LANES), 1) % hd
    strict = c_col < c_row
    incl = c_col <= c_row
    same_sub = (c_col // RWKV_SUB) == (c_row // RWKV_SUB)
    eye_pair = (c_col == c_row).astype(F32)
    s_row = lax.broadcasted_iota(jnp.int32, (LANES, LANES), 0)
    s_col = lax.broadcasted_iota(jnp.int32, (LANES, LANES), 1)
    same_head = (s_row // hd) == (s_col // hd)
    eye_full = (s_row == s_col).astype(F32)

    n_chunks = rows // ch
    cat = jnp.concatenate
    mm_scores = functools.partial(_mm, passes=RWKV_PASSES["scores"])
    mm_solve = functools.partial(_mm, passes=RWKV_PASSES["solve"])
    mm_values = functools.partial(_mm, passes=RWKV_PASSES["values"])
    mm_state = functools.partial(_mm, passes=RWKV_PASSES["state"])
    a_hat, r_hat, k_til, b_til, k_end, b_end, vp, decay_end = [], [], [], [], [], [], [], []
    for j in range(n_chunks):
        sl = slice(j * ch, (j + 1) * ch)
        cm, lwj, kkj, khj, kbj, rj, vj = cum[sl], lw[sl], kk[sl], kh[sl], kb[sl], r[sl], v[sl]
        cm_end = cm[ch - 1:ch, :]
        e_neg = jnp.exp(-cm)
        e_end = jnp.exp(cm_end - cm)
        wide = (-kkj * jnp.exp(cm - lwj), rj * jnp.exp(cm), khj * e_neg, kbj * e_neg,
                khj * e_end, kbj * e_end, vj, jnp.exp(cm_end))
        for pi in range(n_pairs):
            ls = slice(pi * LANES, (pi + 1) * LANES)
            for dst, src in zip((a_hat, r_hat, k_til, b_til, k_end, b_end, vp, decay_end), wide):
                dst.append(src[:, ls])
    units = range(n_chunks * n_pairs)
    each = lambda fn: [fn(u) for u in units]

    big = each(lambda u: mm_scores(cat([a_hat[u], r_hat[u]], axis=0),
                                   cat([_stack_heads(k_til[u]), _stack_heads(b_til[u])], axis=0),
                                   dot=_dot_nt))
    l_ak = each(lambda u: jnp.where(strict, big[u][0:ch, 0:LANES], 0.0))
    l_ab = each(lambda u: jnp.where(strict, big[u][0:ch, LANES:], 0.0))
    m_rk = each(lambda u: jnp.where(incl, big[u][ch:, 0:LANES], 0.0))
    m_rb = each(lambda u: jnp.where(incl, big[u][ch:, LANES:], 0.0))

    l_d = each(lambda u: jnp.where(same_sub, l_ab[u], 0.0))
    l_o = each(lambda u: l_ab[u] - l_d[u])
    x = each(lambda u: eye_pair + l_d[u])
    p = each(lambda u: mm_solve(l_d[u], _stack_heads(l_d[u])))
    for _ in range(2):
        both = each(lambda u: mm_solve(cat([p[u], x[u]], axis=0), _stack_heads(p[u])))
        x = each(lambda u: x[u] + both[u][ch:])
        p = each(lambda u: both[u][:ch])
    t_d = each(lambda u: x[u] + mm_solve(x[u], _stack_heads(p[u])))
    nn = each(lambda u: mm_solve(t_d[u], _stack_heads(l_o[u])))
    n2 = each(lambda u: mm_solve(nn[u], _stack_heads(nn[u])))
    mx = each(lambda u: eye_pair + nn[u])
    mx = each(lambda u: mx[u] + mm_solve(mx[u], _stack_heads(n2[u])))
    t = each(lambda u: mm_solve(mx[u], _stack_heads(t_d[u])))

    lmv = each(lambda u: mm_values(cat([l_ak[u], m_rk[u]], axis=0), _stack_heads(vp[u])))
    wu = each(lambda u: mm_values(
        t[u], cat([_stack_heads(a_hat[u]), _stack_heads(lmv[u][:ch])], axis=1)))
    w_a = each(lambda u: wu[u][:, 0:LANES])
    u_v = each(lambda u: wu[u][:, LANES:])
    mw = each(lambda u: mm_values(
        m_rb[u], cat([_stack_heads(w_a[u]), _stack_heads(u_v[u])], axis=1)))
    q_eff = each(lambda u: r_hat[u] + mw[u][:, 0:LANES])
    y_loc = each(lambda u: lmv[u][ch:] + mw[u][:, LANES:])
    a_eff = each(lambda u: jnp.where(same_head, mm_state(b_end[u], w_a[u], dot=_dot_tn), 0.0)
                 + eye_full * decay_end[u])
    z_loc = each(lambda u: jnp.where(
        same_head, mm_state(cat([k_end[u], b_end[u]], axis=0), cat([vp[u], u_v[u]], axis=0),
                            dot=_dot_tn), 0.0))

    z = [z_ref[pi] for pi in range(n_pairs)]
    for j in range(n_chunks):
        sl = slice(j * ch, (j + 1) * ch)
        us = [j * n_pairs + pi for pi in range(n_pairs)]
        qz = [mm_state(cat([q_eff[u], a_eff[u]], axis=0), z[pi]) for pi, u in enumerate(us)]
        for pi, u in enumerate(us):
            y_buf[sl, pi * LANES:(pi + 1) * LANES] = qz[pi][:ch] + y_loc[u]
        z = [qz[pi][ch:] + z_loc[u] for pi, u in enumerate(us)]
    for pi in range(n_pairs):
        z_ref[pi] = z[pi]

    y = y_buf[...]
    mean = head_total(y) * (1.0 / hd)
    yc = y - mean
    var = head_total(yc * yc) * (1.0 / hd)
    yn = yc * lax.rsqrt(var + RWKV_LN_EPS) * lnw_ref[...] + lnb_ref[...]
    bonus = head_total(r * kh * rk_ref[...]) * v
    o_ref[...] = ((yn + bonus) * gate).astype(o_ref.dtype)


def _rwkv_mixer(proj, bsz, seq, mu, w0, w2, a0, a2, g2, k_k, k_a, r_k, ln_w, ln_b):
    tb = RWKV_TIME_BLOCK
    nc = seq // tb
    d = D_MODEL
    lora_w = 2 * LANES + RWKV_LORA_G
    mur, muk, muv = _row2(mu[0:d]), _row2(mu[d:2 * d]), _row2(mu[2 * d:3 * d])
    o = 3 * d
    pad_l = lambda x, n: jnp.pad(x, (0, n - x.shape[0]))
    mulo = _row2(jnp.concatenate([pad_l(mu[o:o + RWKV_LORA_W], LANES),
                                  pad_l(mu[o + RWKV_LORA_W:o + RWKV_LORA_W + RWKV_LORA_A], LANES),
                                  mu[o + RWKV_LORA_W + RWKV_LORA_A:]]))
    w2p = jnp.pad(w2, ((0, LANES - RWKV_LORA_W), (0, 0)))
    a2p = jnp.pad(a2, ((0, LANES - RWKV_LORA_A), (0, 0)))
    ch = RWKV_CHUNK
    t_i = jnp.arange(tb)[:, None]
    t_j = jnp.arange(tb)[None, :]
    tri = ((t_i >= t_j) & (t_i // ch == t_j // ch)).astype(BF16)
    l_i = jnp.arange(LANES)[:, None]
    l_j = jnp.arange(LANES)[None, :]
    head_sum = (l_i // RWKV_HEAD_DIM == l_j // RWKV_HEAD_DIM).astype(BF16)

    gw = RWKV_GROUP * LANES
    row = lambda b, p, c: b * nc + c
    col = lambda base: (lambda b, p, c: (row(b, p, c), base // gw + p))
    vec = pl.BlockSpec((1, gw), lambda b, p, c: (0, p))
    in_specs = [
        pl.BlockSpec((tb, gw), col(C_R)),
        pl.BlockSpec((tb, gw), col(C_K)),
        pl.BlockSpec((tb, gw), col(C_V)),
        pl.BlockSpec((tb, lora_w), lambda b, p, c: (row(b, p, c), C_LORA // lora_w)),
        vec, vec, vec,
        pl.BlockSpec((1, lora_w), lambda b, p, c: (0, 0)),
        vec,
        pl.BlockSpec((LANES, gw), lambda b, p, c: (0, p)),
        vec,
        pl.BlockSpec((LANES, gw), lambda b, p, c: (0, p)),
        pl.BlockSpec((RWKV_LORA_G, gw), lambda b, p, c: (0, p)),
        vec, vec, vec, vec, vec,
        pl.BlockSpec((tb, tb), lambda b, p, c: (0, 0)),
        pl.BlockSpec((LANES, LANES), lambda b, p, c: (0, 0)),
    ]
    return pl.pallas_call(
        _rwkv_body,
        grid=(bsz, RWKV_PAIRS // RWKV_GROUP, nc),
        in_specs=in_specs,
        out_specs=pl.BlockSpec((tb, gw), lambda b, p, c: (row(b, p, c), p)),
        out_shape=jax.ShapeDtypeStruct((bsz * seq, D_MODEL), BF16),
        scratch_shapes=[pltpu.VMEM((tb + HALO, gw), F32),
                        pltpu.VMEM((tb + HALO, gw), F32),
                        pltpu.VMEM((tb + HALO, gw), F32),
                        pltpu.VMEM((tb + HALO, lora_w), F32),
                        pltpu.VMEM((RWKV_GROUP, LANES, LANES), F32),
                        pltpu.VMEM((tb, gw), F32)],
        compiler_params=_cparams(3),
        name="rwkv_mixer",
    )(proj, proj, proj, proj, mur, muk, muv, mulo, _row2(w0), w2p, _row2(a0), a2p, g2,
      _row2(k_k), _row2(k_a), _row2(r_k.reshape(-1)), _row2(ln_w), _row2(ln_b), tri, head_sum)


def _gated_merge_body(ys_ref, yr_ref, ym_ref, ps_ref, pr_ref, pm_ref, gs_ref, gr_ref, gm_ref, o_ref):
    acc = jax.nn.sigmoid(gs_ref[...]) * _dot(ys_ref[...], ps_ref[...])
    acc = acc + jax.nn.sigmoid(gr_ref[...]) * _dot(yr_ref[...], pr_ref[...])
    acc = acc + jax.nn.sigmoid(gm_ref[...]) * _dot(ym_ref[...], pm_ref[...])
    o_ref[...] = acc.astype(o_ref.dtype)


def _gated_merge(ys, ps, proj, tm=TM_GATED, tn=TN_GATED):
    m, k = ys[0].shape
    tm = min(tm, m)
    n = ps[0].shape[1]
    y_spec = pl.BlockSpec((tm, k), lambda j, i: (i, 0))
    p_spec = pl.BlockSpec((k, tn), lambda j, i: (0, j), pipeline_mode=pl.Buffered(1))
    gate_spec = lambda col: pl.BlockSpec((tm, tn), lambda j, i: (i, col // tn + j))
    return pl.pallas_call(
        _gated_merge_body,
        grid=(n // tn, m // tm),
        in_specs=[y_spec] * 3 + [p_spec] * 3 + [gate_spec(C_GS), gate_spec(C_GR), gate_spec(C_GM)],
        out_specs=pl.BlockSpec((tm, tn), lambda j, i: (i, j)),
        out_shape=jax.ShapeDtypeStruct((m, n), BF16),
        compiler_params=_cparams(2),
        name="gated_merge",
    )(*ys, *ps, proj, proj, proj)


def _mm_res_ln_body(a_ref, w_ref, h_ref, g_ref, b_ref, o_ref, ob_ref, *, row_split):
    n = a_ref.shape[0] // row_split
    subs = [slice(s * n, (s + 1) * n) for s in range(row_split)]
    acc = [_dot(a_ref[rows, :], w_ref[...]) for rows in subs]
    for rows, mix in zip(subs, acc):
        h = _layer_norm_rows(ALPHA * h_ref[rows, :] + mix, g_ref[...], b_ref[...])
        o_ref[rows, :] = h
        ob_ref[rows, :] = h.astype(BF16)


def _matmul_res_ln(a, w, h, g, b, tm, row_split):
    m, k = a.shape
    n = w.shape[1]
    const = lambda shape: pl.BlockSpec(shape, lambda i: (0, 0), pipeline_mode=pl.Buffered(1))
    return pl.pallas_call(
        functools.partial(_mm_res_ln_body, row_split=row_split),
        grid=(m // tm,),
        in_specs=[pl.BlockSpec((tm, k), lambda i: (i, 0)),
                  const((k, n)),
                  pl.BlockSpec((tm, n), lambda i: (i, 0)),
                  const((1, n)), const((1, n))],
        out_specs=[pl.BlockSpec((tm, n), lambda i: (i, 0)),
                   pl.BlockSpec((tm, n), lambda i: (i, 0))],
        out_shape=[jax.ShapeDtypeStruct((m, n), F32), jax.ShapeDtypeStruct((m, n), BF16)],
        compiler_params=_cparams(1),
        name="matmul_res_ln",
    )(a, w, h, _row2(g), _row2(b))


FFN_HALO = 16


def _ffn_up_body(h_ref, hp_ref, wg_ref, wua_ref, wub_ref, cwg_ref, cbg_ref, cwu_ref, cbu_ref, o_ref,
                 gbuf, ubuf, wg_s, wu_s, *, blocks_per_seq):
    rows = h_ref.shape[0]
    tn = o_ref.shape[1]
    first = (pl.program_id(1) % blocks_per_seq) == 0
    keep = jnp.where(first, 0.0, 1.0)

    @pl.when(pl.program_id(1) == 0)
    def _():
        col = pl.program_id(0) * tn + lax.broadcasted_iota(jnp.int32, wg_s.shape, 1)
        off = FFN_D % tn
        up = jnp.concatenate([wua_ref[:, off:], wub_ref[:, :off]], axis=1)
        wg_s[...] = jnp.where(col < FFN_D, wg_ref[...], 0.0).astype(BF16)
        wu_s[...] = jnp.where(col < FFN_D, up, 0.0).astype(BF16)

    def conv(buf, cw_ref, cb_ref):
        acc = cb_ref[...] + cw_ref[2:3, :] * buf[FFN_HALO:FFN_HALO + rows, :]
        acc = acc + cw_ref[1:2, :] * buf[FFN_HALO - 1:FFN_HALO - 1 + rows, :]
        return acc + cw_ref[0:1, :] * buf[FFN_HALO - 2:FFN_HALO - 2 + rows, :]

    for buf, w_ref in ((gbuf, wg_s), (ubuf, wu_s)):
        buf[0:FFN_HALO, :] = _dot(hp_ref[...], w_ref[...]) * keep
        buf[FFN_HALO:FFN_HALO + rows, :] = _dot(h_ref[...], w_ref[...])
    gate = conv(gbuf, cwg_ref, cbg_ref)
    up = conv(ubuf, cwu_ref, cbu_ref)
    gelu = 0.5 * gate * (1.0 + lax.erf(gate * (2.0 ** -0.5)))
    o_ref[...] = (gelu * up).astype(o_ref.dtype)


def _ffn_up(hb, w_up, layer, cwg, cbg, cwu, cbu, seq, tm):
    m, k = hb.shape
    n = FFN_PAD
    tn = FFN_TN
    halo_blocks = tm // FFN_HALO
    up0 = FFN_D // tn
    body = functools.partial(_ffn_up_body, blocks_per_seq=seq // tm)
    w_spec = lambda first_block: pl.BlockSpec((None, k, tn), lambda j, i: (layer, 0, first_block + j))
    return pl.pallas_call(
        body,
        grid=(n // tn, m // tm),
        in_specs=[pl.BlockSpec((tm, k), lambda j, i: (i, 0)),
                  pl.BlockSpec((FFN_HALO, k), lambda j, i: (jnp.maximum(i * halo_blocks - 1, 0), 0)),
                  w_spec(0), w_spec(up0), w_spec(up0 + 1),
                  pl.BlockSpec((3, tn), lambda j, i: (0, j)),
                  pl.BlockSpec((1, tn), lambda j, i: (0, j)),
                  pl.BlockSpec((3, tn), lambda j, i: (0, j)),
                  pl.BlockSpec((1, tn), lambda j, i: (0, j))],
        out_specs=pl.BlockSpec((tm, tn), lambda j, i: (i, j)),
        out_shape=jax.ShapeDtypeStruct((m, n), BF16),
        scratch_shapes=[pltpu.VMEM((tm + FFN_HALO, tn), F32),
                        pltpu.VMEM((tm + FFN_HALO, tn), F32),
                        pltpu.VMEM((k, tn), BF16),
                        pltpu.VMEM((k, tn), BF16)],
        compiler_params=_cparams(2),
        name="ffn_up",
    )(hb, hb, w_up, w_up, w_up, cwg, cbg, cwu, cbu)


def _pack_segments():
    d = D_MODEL
    order = (("z", d), ("x", d), ("bc", 2 * SSD_GN), ("dt", SSD_HEADS), ("r", d), ("k", d), ("v", d),
             ("w_lo", RWKV_LORA_W), ("a_lo", RWKV_LORA_A), ("g_lo", RWKV_LORA_G), ("qk", d),
             ("mv", d), ("mo", d), ("gi", ML_HEADS), ("gf", ML_HEADS), ("gs", d), ("gr", d), ("gm", d))
    dst = dict(z=C_Z, x=C_X, r=C_R, k=C_K, v=C_V, qk=C_QK, mv=C_MV, mo=C_MO, gs=C_GS, gr=C_GR,
               gm=C_GM, bc=C_BC, w_lo=C_LORA, a_lo=C_LORA + LANES, g_lo=C_LORA + 2 * LANES,
               dt=C_SMALL, gi=C_SMALL + SSD_HEADS, gf=C_SMALL + SSD_HEADS + ML_HEADS)
    segments, src = [], 0
    for name, width in order:
        segments.append((src, width, dst[name]))
        src += width
    return tuple(segments), src


def _pack_body(wt_ref, o_ref, *, segments):
    group, start, pos = [], 0, 0

    def flush():
        nonlocal group, start
        rows = group[0] if len(group) == 1 else jnp.concatenate(group, axis=0)
        o_ref[:, start:pos] = rows.T.astype(o_ref.dtype)
        group, start = [], pos

    for src, width, dst in sorted(segments, key=lambda s: s[2]) + [(None, 0, N_PROJ)]:
        if dst > pos:
            group.append(jnp.zeros((dst - pos, LANES), F32))
            pos = dst
            if pos % LANES == 0:
                flush()
        if width:
            group.append(wt_ref[src:src + width, :])
            pos += width
            if pos % LANES == 0:
                flush()


def _pack_w_in(w_in, layer):
    segments, d_in = _pack_segments()
    d = w_in.shape[1]
    return pl.pallas_call(
        functools.partial(_pack_body, segments=segments),
        grid=(d // LANES,),
        in_specs=[pl.BlockSpec((None, d_in, LANES), lambda i: (layer, 0, i))],
        out_specs=pl.BlockSpec((LANES, N_PROJ), lambda i: (i, 0)),
        out_shape=jax.ShapeDtypeStruct((d, N_PROJ), BF16),
        compiler_params=_cparams(1),
        name="pack_w_in",
    )(jnp.swapaxes(w_in, 1, 2))


def _to_bf16_body(w_ref, o_ref, *, valid_rows):
    x = w_ref[...]
    if valid_rows is not None:
        row = pl.program_id(0) * o_ref.shape[0] + lax.broadcasted_iota(jnp.int32, o_ref.shape, 0)
        x = jnp.where(row < valid_rows, x, 0.0)
    o_ref[...] = x.astype(o_ref.dtype)


def _to_bf16(w_stack, layer, out_rows=None, rows_per_step=512):
    _, rows, cols = w_stack.shape
    out_rows = rows if out_rows is None else out_rows
    tr = rows_per_step
    last = (rows - 1) // tr
    return pl.pallas_call(
        functools.partial(_to_bf16_body, valid_rows=None if out_rows == rows else rows),
        grid=(out_rows // tr,),
        in_specs=[pl.BlockSpec((None, tr, cols), lambda i: (layer, jnp.minimum(i, last), 0))],
        out_specs=pl.BlockSpec((tr, cols), lambda i: (i, 0)),
        out_shape=jax.ShapeDtypeStruct((out_rows, cols), BF16),
        compiler_params=_cparams(1),
        name="to_bf16",
    )(w_stack)


def _pad_cols(x, n):
    return jnp.pad(x, ((0, 0), (0, n - x.shape[1])))


def kernel(x, ln_in_g, ln_in_b, w_in, ssd_conv_w, ssd_conv_b, ssd_dt_bias, ssd_a_log, ssd_d, ssd_norm_w, rwkv_mu, rwkv_w0, rwkv_w2, rwkv_a0, rwkv_a2, rwkv_g2, rwkv_k_k, rwkv_k_a, rwkv_r_k, rwkv_ln_w, rwkv_ln_b, mlstm_conv_w, mlstm_conv_b, mlstm_i_bias, mlstm_f_bias, mlstm_norm_w, proj_ssd, proj_rwkv, proj_mlstm, w_out, ln1_g, ln1_b, ffn_w_up, ffn_conv_w, ffn_conv_b, ffn_w_down, ln2_g, ln2_b):
    bsz, seq, d = x.shape
    n_tok = bsz * seq
    tb = TIME_BLOCK
    t_i = jnp.arange(tb)[:, None]
    t_j = jnp.arange(tb)[None, :]
    tri = (t_i >= t_j).astype(BF16)

    h, hb = _ln_in(x.reshape(n_tok, d), ln_in_g, ln_in_b)
    for l in range(DEPTH):
        proj = _matmul(hb, _pack_w_in(w_in, l), tm=min(TM_IN_PROJ, n_tok), tn=TN_IN_PROJ,
                       out_dtype=F32, name="in_proj")
        y_ssd = _ssd_mixer(proj, bsz, seq, ssd_conv_w[l], ssd_conv_b[l], ssd_dt_bias[l],
                           ssd_a_log[l], ssd_d[l], ssd_norm_w[l], tri)
        y_rwkv = _rwkv_mixer(proj, bsz, seq, rwkv_mu[l], rwkv_w0[l], rwkv_w2[l], rwkv_a0[l],
                             rwkv_a2[l], rwkv_g2[l], rwkv_k_k[l], rwkv_k_a[l], rwkv_r_k[l],
                             rwkv_ln_w[l], rwkv_ln_b[l])
        y_mlstm = _mlstm_mixer(proj, bsz, seq, mlstm_conv_w[l], mlstm_conv_b[l], mlstm_i_bias[l],
                               mlstm_f_bias[l], mlstm_norm_w[l], tri)
        merged = _gated_merge(
            (y_ssd, y_rwkv, y_mlstm),
            (_to_bf16(proj_ssd, l), _to_bf16(proj_rwkv, l), _to_bf16(proj_mlstm, l)), proj)
        h, hb = _matmul_res_ln(merged, _to_bf16(w_out, l), h, ln1_g[l], ln1_b[l],
                               tm=min(TM_W_OUT, n_tok), row_split=4)

        cwg = _pad_cols(ffn_conv_w[l][:, :FFN_D], FFN_PAD)
        cwu = _pad_cols(ffn_conv_w[l][:, FFN_D:], FFN_PAD)
        cbg = _pad_cols(_row2(ffn_conv_b[l][:FFN_D]), FFN_PAD)
        cbu = _pad_cols(_row2(ffn_conv_b[l][FFN_D:]), FFN_PAD)
        act = _ffn_up(hb, ffn_w_up, l, cwg, cbg, cwu, cbu, seq, tm=min(TM_FFN_UP, seq))
        wd = _to_bf16(ffn_w_down, l, out_rows=FFN_PAD)
        h, hb = _matmul_res_ln(act, wd, h, ln2_g[l], ln2_b[l], tm=min(TM_FFN_DOWN, n_tok),
                               row_split=1)
    return h.reshape(bsz, seq, d)
```

```python
import functools

import jax
import jax.numpy as jnp
from jax import lax
from jax.experimental import pallas as pl
from jax.experimental.pallas import tpu as pltpu

F32 = jnp.float32
BF16 = jnp.bfloat16

D_MODEL = 2048
DEPTH = 2
ALPHA = (2.0 * DEPTH) ** 0.25
LN_EPS = 1e-5

SSD_HEADS = 32
SSD_HEAD_DIM = 64
SSD_GROUPS = 4
SSD_STATE = 128
SSD_GN = SSD_GROUPS * SSD_STATE
SSD_EPS = 1e-5
SSD_GROUP_W = D_MODEL // SSD_GROUPS
SSD_HPG = SSD_HEADS // SSD_GROUPS

RWKV_HEAD_DIM = 64
RWKV_PAIRS = D_MODEL // 128
RWKV_LORA_W = 96
RWKV_LORA_A = 96
RWKV_LORA_G = 256
RWKV_LN_EPS = 64e-5
RWKV_CHUNK = 64
RWKV_SUB = 16

ML_HEADS = 8
ML_DK = 128
ML_DV = 256
ML_EPS = 1e-6
ML_GROUP = 8

FFN_D = 5504
FFN_PAD = 5632
FFN_TN = 512

LANES = 128
HALO = 8
TIME_BLOCK = 128

C_Z = 0
C_X = 2048
C_R = 4096
C_K = 6144
C_V = 8192
C_QK = 10240
C_MV = 12288
C_MO = 14336
C_GS = 16384
C_GR = 18432
C_GM = 20480
C_BC = 22528
C_LORA = 23552
C_SMALL = 24064
N_PROJ = 24576

VMEM_LIMIT = 56 * 1024 * 1024

TM_LN_IN = 512
TM_IN_PROJ, TN_IN_PROJ = 1024, 2048
TM_GATED, TN_GATED = 512, 1024
TM_W_OUT = 512
TM_FFN_UP = 1024
TM_FFN_DOWN = 256


def _dot(a, b):
    return jnp.dot(a, b, preferred_element_type=F32)


def _dot_nt(a, b):
    return lax.dot_general(a, b, (((1,), (1,)), ((), ())), preferred_element_type=F32)


def _dot_tn(a, b):
    return lax.dot_general(a, b, (((0,), (0,)), ((), ())), preferred_element_type=F32)


def _split(x, n):
    parts, r = [], x
    for i in range(n):
        p = r.astype(BF16)
        parts.append(p)
        if i + 1 < n:
            r = r - p.astype(F32)
    return parts


def _dotx_l(x, w_exact, n, dot=_dot):
    acc = None
    for p in _split(x, n):
        t = dot(p, w_exact)
        acc = t if acc is None else acc + t
    return acc


def _dotx_r(w_exact, x, n, dot=_dot):
    acc = None
    for p in _split(x, n):
        t = dot(w_exact, p)
        acc = t if acc is None else acc + t
    return acc


def _mm(a, b, passes, dot=_dot):
    if passes == 1:
        return dot(a.astype(BF16), b.astype(BF16))
    ah, al = _split(a, 2)
    bh, bl = _split(b, 2)
    return dot(ah, bh) + (dot(ah, bl) + dot(al, bh))


def _layer_norm_rows(x, g, b):
    mu = jnp.mean(x, axis=-1, keepdims=True)
    xc = x - mu
    var = jnp.mean(xc * xc, axis=-1, keepdims=True)
    return xc * lax.rsqrt(var + LN_EPS) * g + b


def _cparams(n_axes):
    return pltpu.CompilerParams(dimension_semantics=("arbitrary",) * n_axes,
                                vmem_limit_bytes=VMEM_LIMIT)


def _row2(v):
    return v.reshape(1, -1)


def _ln_in_body(x_ref, g_ref, b_ref, h_ref, hb_ref):
    h = _layer_norm_rows(x_ref[...], g_ref[...], b_ref[...])
    h_ref[...] = h
    hb_ref[...] = h.astype(BF16)


def _ln_in(x2d, g, b, tm=TM_LN_IN):
    n, d = x2d.shape
    tm = min(tm, n)
    return pl.pallas_call(
        _ln_in_body,
        grid=(n // tm,),
        in_specs=[pl.BlockSpec((tm, d), lambda i: (i, 0)),
                  pl.BlockSpec((1, d), lambda i: (0, 0)),
                  pl.BlockSpec((1, d), lambda i: (0, 0))],
        out_specs=[pl.BlockSpec((tm, d), lambda i: (i, 0)),
                   pl.BlockSpec((tm, d), lambda i: (i, 0))],
        out_shape=[jax.ShapeDtypeStruct((n, d), F32), jax.ShapeDtypeStruct((n, d), BF16)],
        compiler_params=_cparams(1),
        name="ln_in",
    )(x2d, _row2(g), _row2(b))


def _mm_body(a_ref, w_ref, o_ref):
    o_ref[...] = _dot(a_ref[...], w_ref[...]).astype(o_ref.dtype)


def _matmul(a, w, tm, tn, out_dtype, name):
    m, k = a.shape
    _, n = w.shape
    return pl.pallas_call(
        _mm_body,
        grid=(n // tn, m // tm),
        in_specs=[pl.BlockSpec((tm, k), lambda j, i: (i, 0)),
                  pl.BlockSpec((k, tn), lambda j, i: (0, j))],
        out_specs=pl.BlockSpec((tm, tn), lambda j, i: (i, j)),
        out_shape=jax.ShapeDtypeStruct((m, n), out_dtype),
        compiler_params=_cparams(2),
        name=name,
    )(a, w)


CONV_HALO = 16


def _conv_shift_matrix(rows):
    t = jnp.arange(rows)[:, None]
    c = jnp.arange(3 * (rows + CONV_HALO))[None, :]
    k, j = c // (rows + CONV_HALO), c % (rows + CONV_HALO)
    return (j == CONV_HALO + t - 3 + k).astype(BF16)


def _conv4_silu(buf, src_ref, w_ref, b_ref, shift_ref, rows):
    x = src_ref[...]
    buf[CONV_HALO:CONV_HALO + rows, :] = x
    hist = buf[...]
    taps = jnp.concatenate([(hist * w_ref[k:k + 1, :]).astype(BF16) for k in range(3)], axis=0)
    acc = b_ref[...] + w_ref[3:4, :] * x + _dot(shift_ref[...], taps)
    buf[0:CONV_HALO, :] = buf[rows:rows + CONV_HALO, :]
    return jax.nn.silu(acc)


def _ssd_body(z_ref, x_ref, bc_ref, sm_ref, cwx_ref, cbx_ref, cwbc_ref, cbbc_ref, dtb_ref,
              alog_ref, dsk_ref, nw_ref, e_ref, tri_ref, shift_ref, o_ref, xbuf, bcbuf, st_ref):
    rows = x_ref.shape[0]
    gw = SSD_GROUP_W
    groups = range(SSD_GROUPS)

    @pl.when(pl.program_id(1) == 0)
    def _():
        xbuf[0:CONV_HALO, :] = jnp.zeros((CONV_HALO, xbuf.shape[1]), F32)
        bcbuf[0:CONV_HALO, :] = jnp.zeros((CONV_HALO, bcbuf.shape[1]), F32)
        st_ref[...] = jnp.zeros(st_ref.shape, F32)

    xs_all = _conv4_silu(xbuf, x_ref, cwx_ref, cbx_ref, shift_ref, rows)
    bc = _conv4_silu(bcbuf, bc_ref, cwbc_ref, cbbc_ref, shift_ref, rows)

    dt = jax.nn.softplus(sm_ref[...] + dtb_ref[...])
    adt = dt * (-jnp.exp(alog_ref[...]))
    a_cs = _dotx_r(tri_ref[...], adt, 3)
    a_cs_t = a_cs.T

    t_row = lax.broadcasted_iota(jnp.int32, (rows, rows), 0)
    t_col = lax.broadcasted_iota(jnp.int32, (rows, rows), 1)
    causal = t_row >= t_col
    lane = lax.broadcasted_iota(jnp.int32, (rows, LANES), 1)

    xs = [xs_all[:, g * gw:(g + 1) * gw] for g in groups]
    bm = [bc[:, g * SSD_STATE:(g + 1) * SSD_STATE].astype(BF16) for g in groups]
    cm = [bc[:, SSD_GN + g * SSD_STATE:SSD_GN + (g + 1) * SSD_STATE].astype(BF16) for g in groups]
    a_exp = [_dotx_l(a_cs, e_ref[g], 3) for g in groups]
    dt_exp = [_dotx_l(dt, e_ref[g], 3) for g in groups]
    scores = [_dot_nt(cm[g], bm[g]) for g in groups]
    st = [st_ref[g] for g in groups]
    y_off = [_dot(cm[g], st[g].astype(BF16)) for g in groups]
    xdt = [xs[g] * dt_exp[g] for g in groups]

    y_diag = []
    for g in groups:
        parts = []
        for pr in range(SSD_HPG // 2):
            xp = xdt[g][:, pr * LANES:(pr + 1) * LANES]
            acc = None
            for s in range(2):
                hh = g * SSD_HPG + 2 * pr + s
                seg = a_cs[:, hh:hh + 1] - a_cs_t[hh:hh + 1, :]
                m = jnp.where(causal, jnp.exp(seg), 0.0) * scores[g]
                head_lanes = (lane < SSD_HEAD_DIM) if s == 0 else (lane >= SSD_HEAD_DIM)
                xm = jnp.where(head_lanes, xp, 0.0)
                t = _dot(m.astype(BF16), xm.astype(BF16))
                acc = t if acc is None else acc + t
            parts.append(acc)
        y_diag.append(jnp.concatenate(parts, axis=1))

    for g in groups:
        cols = slice(g * gw, (g + 1) * gw)
        a_last = a_exp[g][rows - 1:rows, :]
        st_ref[g] = jnp.exp(a_last) * st[g] + _dot_tn(
            bm[g], (xdt[g] * jnp.exp(a_last - a_exp[g])).astype(BF16))
        y = y_diag[g] + y_off[g] * jnp.exp(a_exp[g]) + xs[g] * dsk_ref[:, cols]
        y = y * jax.nn.silu(z_ref[:, cols])
        ms = jnp.mean(y * y, axis=-1, keepdims=True)
        o_ref[:, cols] = (y * lax.rsqrt(ms + SSD_EPS) * nw_ref[:, cols]).astype(o_ref.dtype)


def _ssd_mixer(proj, bsz, seq, conv_w, conv_b, dt_bias, a_log, d_skip, norm_w, tri):
    tb = TIME_BLOCK
    nc = seq // tb
    gw = SSD_GROUP_W
    d = D_MODEL
    bcw = 2 * SSD_GN
    cwx, cwbc = conv_w[:, :d], conv_w[:, d:]
    cbx, cbbc = _row2(conv_b[:d]), _row2(conv_b[d:])
    head_lanes = lambda v: _row2(jnp.pad(v, (0, LANES - SSD_HEADS)))
    grp = jnp.arange(SSD_GROUPS)[:, None, None]
    src = jnp.arange(LANES)[None, :, None]
    dst = jnp.arange(gw)[None, None, :]
    expand = (src == grp * SSD_HPG + dst // SSD_HEAD_DIM).astype(BF16)

    row = lambda b, c: b * nc + c
    full = lambda shape: pl.BlockSpec(shape, lambda b, c: (0,) * len(shape))
    in_specs = [
        pl.BlockSpec((tb, d), lambda b, c: (row(b, c), C_Z // d)),
        pl.BlockSpec((tb, d), lambda b, c: (row(b, c), C_X // d)),
        pl.BlockSpec((tb, bcw), lambda b, c: (row(b, c), C_BC // bcw)),
        pl.BlockSpec((tb, LANES), lambda b, c: (row(b, c), C_SMALL // LANES)),
        full((4, d)), full((1, d)), full((4, bcw)), full((1, bcw)),
        full((1, LANES)), full((1, LANES)), full((1, d)), full((1, d)),
        full((SSD_GROUPS, LANES, gw)), full((tb, tb)), full((tb, 3 * (tb + CONV_HALO))),
    ]
    return pl.pallas_call(
        _ssd_body,
        grid=(bsz, nc),
        in_specs=in_specs,
        out_specs=pl.BlockSpec((tb, d), lambda b, c: (row(b, c), 0)),
        out_shape=jax.ShapeDtypeStruct((bsz * seq, d), BF16),
        scratch_shapes=[pltpu.VMEM((tb + CONV_HALO, d), F32),
                        pltpu.VMEM((tb + CONV_HALO, bcw), F32),
                        pltpu.VMEM((SSD_GROUPS, SSD_STATE, gw), F32)],
        compiler_params=_cparams(2),
        name="ssd_mixer",
    )(proj, proj, proj, proj, cwx, cbx, cwbc, cbbc, head_lanes(dt_bias), head_lanes(a_log),
      _row2(jnp.repeat(d_skip, SSD_HEAD_DIM)), _row2(norm_w), expand, tri, _conv_shift_matrix(tb))


def _mlstm_body(q_ref, k_ref, v_ref, og_ref, sm_ref, sel_ref, cwq_ref, cbq_ref, cwk_ref, cbk_ref,
                gb_ref, nw_ref, tri_ref, shift_ref, o_ref, qbuf, kbuf, ct_ref, n_ref, m_ref):
    rows = q_ref.shape[0]

    @pl.when(pl.program_id(2) == 0)
    def _():
        qbuf[0:CONV_HALO, :] = jnp.zeros((CONV_HALO, qbuf.shape[1]), F32)
        kbuf[0:CONV_HALO, :] = jnp.zeros((CONV_HALO, kbuf.shape[1]), F32)
        ct_ref[...] = jnp.zeros(ct_ref.shape, F32)
        n_ref[...] = jnp.zeros(n_ref.shape, F32)
        m_ref[...] = jnp.zeros(m_ref.shape, F32)

    heads = range(q_ref.shape[1] // ML_DK)
    q_all = _conv4_silu(qbuf, q_ref, cwq_ref, cbq_ref, shift_ref, rows) * (ML_DK ** -0.5)
    k_all = _conv4_silu(kbuf, k_ref, cwk_ref, cbk_ref, shift_ref, rows)
    qf = [q_all[:, i * ML_DK:(i + 1) * ML_DK] for i in heads]
    q = [qf[i].astype(BF16) for i in heads]
    k = [k_all[:, i * ML_DK:(i + 1) * ML_DK] for i in heads]
    kb = [k[i].astype(BF16) for i in heads]
    v = [v_ref[:, i * ML_DV:(i + 1) * ML_DV] for i in heads]

    pre = _dotx_l(sm_ref[...], sel_ref[...], 3) + gb_ref[...]
    log_f = jax.nn.log_sigmoid(pre)
    bcum_all = _dotx_r(tri_ref[...], log_f, 3)
    lane = lax.broadcasted_iota(jnp.int32, (rows, LANES), 1)
    gates_t = jnp.where(lane % 2 == 0, pre, bcum_all).T
    t_row = lax.broadcasted_iota(jnp.int32, (rows, rows), 0)
    t_col = lax.broadcasted_iota(jnp.int32, (rows, rows), 1)
    causal = t_row >= t_col

    li = [pre[:, 2 * i:2 * i + 1] for i in heads]
    bcum = [bcum_all[:, 2 * i + 1:2 * i + 2] for i in heads]
    log_d = [jnp.where(causal, bcum[i] - gates_t[2 * i + 1:2 * i + 2, :] + gates_t[2 * i:2 * i + 1, :],
                       -jnp.inf) for i in heads]
    qk = [_dot_nt(q[i], kb[i]) for i in heads]
    ct = [ct_ref[i] for i in heads]
    inter = [_dot(q[i], ct[i].astype(BF16)) for i in heads]
    m_prev = [m_ref[i, 0:1, 0:1] for i in heads]
    m_inter = [bcum[i] + m_prev[i] for i in heads]
    m_t = [jnp.maximum(jnp.max(log_d[i], axis=-1, keepdims=True), m_inter[i]) for i in heads]
    w = [jnp.exp(log_d[i] - m_t[i]) * qk[i] for i in heads]
    s_inter = [jnp.exp(m_inter[i] - m_t[i]) for i in heads]
    nvec = [n_ref[i, 0:1, :] for i in heads]
    num = [_dot(w[i].astype(BF16), v[i].astype(BF16)) + s_inter[i] * inter[i] for i in heads]
    den = [jnp.sum(w[i], axis=-1, keepdims=True)
           + s_inter[i] * jnp.sum(qf[i] * nvec[i], axis=-1, keepdims=True)
           for i in heads]
    h = [num[i] / jnp.maximum(jnp.abs(den[i]), jnp.exp(-m_t[i])) for i in heads]

    b_tot = [bcum[i][rows - 1:rows, :] for i in heads]
    log_w = [b_tot[i] - bcum[i] + li[i] for i in heads]
    m_new = [jnp.maximum(b_tot[i] + m_prev[i], jnp.max(log_w[i], axis=0, keepdims=True))
             for i in heads]
    w_end = [jnp.exp(log_w[i] - m_new[i]) for i in heads]
    s_old = [jnp.exp(b_tot[i] + m_prev[i] - m_new[i]) for i in heads]
    ct_new = [s_old[i] * ct[i] + _dot_tn(kb[i], (w_end[i] * v[i]).astype(BF16)) for i in heads]
    for i in heads:
        ct_ref[i] = ct_new[i]
        n_ref[i] = jnp.broadcast_to(
            s_old[i] * nvec[i] + jnp.sum(w_end[i] * k[i], axis=0, keepdims=True), n_ref.shape[1:])
        m_ref[i] = jnp.broadcast_to(m_new[i], m_ref.shape[1:])
        hn = h[i] * lax.rsqrt(jnp.mean(h[i] * h[i], axis=-1, keepdims=True) + ML_EPS)
        cols = slice(i * ML_DV, (i + 1) * ML_DV)
        o_ref[:, cols] = (jax.nn.sigmoid(og_ref[:, cols]) * hn * nw_ref[:, cols]).astype(o_ref.dtype)


def _mlstm_mixer(proj, bsz, seq, conv_w, conv_b, i_bias, f_bias, norm_w, tri):
    tb = TIME_BLOCK
    nc = seq // tb
    hg = ML_GROUP
    n_groups = ML_HEADS // hg
    qw, vw = hg * ML_DK, hg * ML_DV
    small_i = SSD_HEADS
    small_f = SSD_HEADS + ML_HEADS
    src = jnp.arange(LANES)[None, :, None]
    dst = jnp.arange(LANES)[None, None, :]
    grp = jnp.arange(n_groups)[:, None, None]
    head = grp * hg + dst // 2
    sel = ((dst < 2 * hg) & (((dst % 2 == 0) & (src == small_i + head))
                             | ((dst % 2 == 1) & (src == small_f + head)))).astype(BF16)
    gate_bias = jnp.stack([i_bias.reshape(n_groups, hg), f_bias.reshape(n_groups, hg)], axis=-1)
    gate_bias = jnp.pad(gate_bias.reshape(n_groups, 1, 2 * hg), ((0, 0), (0, 0), (0, LANES - 2 * hg)))
    conv_b2 = _row2(conv_b)
    kq = n_groups
    row = lambda b, h, c: b * nc + c
    in_specs = [
        pl.BlockSpec((tb, qw), lambda b, h, c: (row(b, h, c), C_QK // qw + h)),
        pl.BlockSpec((tb, qw), lambda b, h, c: (row(b, h, c), C_QK // qw + kq + h)),
        pl.BlockSpec((tb, vw), lambda b, h, c: (row(b, h, c), C_MV // vw + h)),
        pl.BlockSpec((tb, vw), lambda b, h, c: (row(b, h, c), C_MO // vw + h)),
        pl.BlockSpec((tb, LANES), lambda b, h, c: (row(b, h, c), C_SMALL // LANES)),
        pl.BlockSpec((None, LANES, LANES), lambda b, h, c: (h, 0, 0)),
        pl.BlockSpec((4, qw), lambda b, h, c: (0, h)),
        pl.BlockSpec((1, qw), lambda b, h, c: (0, h)),
        pl.BlockSpec((4, qw), lambda b, h, c: (0, kq + h)),
        pl.BlockSpec((1, qw), lambda b, h, c: (0, kq + h)),
        pl.BlockSpec((None, 1, LANES), lambda b, h, c: (h, 0, 0)),
        pl.BlockSpec((1, vw), lambda b, h, c: (0, h)),
        pl.BlockSpec((tb, tb), lambda b, h, c: (0, 0)),
        pl.BlockSpec((tb, 3 * (tb + CONV_HALO)), lambda b, h, c: (0, 0)),
    ]
    return pl.pallas_call(
        _mlstm_body,
        grid=(bsz, n_groups, nc),
        in_specs=in_specs,
        out_specs=pl.BlockSpec((tb, vw), lambda b, h, c: (row(b, h, c), h)),
        out_shape=jax.ShapeDtypeStruct((bsz * seq, D_MODEL), BF16),
        scratch_shapes=[pltpu.VMEM((tb + CONV_HALO, qw), F32),
                        pltpu.VMEM((tb + CONV_HALO, qw), F32),
                        pltpu.VMEM((hg, ML_DK, ML_DV), F32),
                        pltpu.VMEM((hg, HALO, LANES), F32),
                        pltpu.VMEM((hg, HALO, LANES), F32)],
        compiler_params=_cparams(3),
        name="mlstm_mixer",
    )(proj, proj, proj, proj, proj, sel, conv_w, conv_b2, conv_w, conv_b2, gate_bias,
      _row2(norm_w), tri, _conv_shift_matrix(tb))


def _stack_heads(x):
    lane = lax.broadcasted_iota(jnp.int32, x.shape, 1)
    return jnp.concatenate([jnp.where(lane < RWKV_HEAD_DIM, x, 0.0),
                            jnp.where(lane >= RWKV_HEAD_DIM, x, 0.0)], axis=0)


def _shift_lerp(buf, src_ref, mu_ref, rows):
    buf[HALO:HALO + rows, :] = src_ref[...]
    cur = buf[HALO:HALO + rows, :]
    prev = buf[HALO - 1:HALO - 1 + rows, :]
    buf[0:HALO, :] = buf[rows:rows + HALO, :]
    return cur + (prev - cur) * mu_ref[...]


RWKV_GROUP = 8
RWKV_PASSES = dict(scores=1, solve=1, values=1, state=1)
RWKV_TIME_BLOCK = 256
HEAD_SUM_PASSES = 1


def _rwkv_body(r_ref, k_ref, v_ref, lo_ref, mur_ref, muk_ref, muv_ref, mulo_ref, w0_ref, w2_ref,
               a0_ref, a2_ref, g2_ref, kk_ref, ka_ref, rk_ref, lnw_ref, lnb_ref, tri_ref,
               ones_ref, o_ref, rbuf, kbuf, vbuf, lobuf, z_ref, y_buf):
    rows = r_ref.shape[0]
    n_pairs = r_ref.shape[1] // LANES
    ch = RWKV_CHUNK
    hd = RWKV_HEAD_DIM

    @pl.when(pl.program_id(2) == 0)
    def _():
        for buf in (rbuf, kbuf, vbuf, lobuf):
            buf[0:HALO, :] = jnp.zeros((HALO, buf.shape[1]), F32)
        z_ref[...] = jnp.zeros(z_ref.shape, F32)

    r = _shift_lerp(rbuf, r_ref, mur_ref, rows)
    k = _shift_lerp(kbuf, k_ref, muk_ref, rows)
    v = _shift_lerp(vbuf, v_ref, muv_ref, rows)
    lo = _shift_lerp(lobuf, lo_ref, mulo_ref, rows)
    w_lo, a_lo, g_lo = lo[:, 0:LANES], lo[:, LANES:2 * LANES], lo[:, 2 * LANES:]

    log_w = -jax.nn.softplus(-(w0_ref[...] + _mm(jnp.tanh(w_lo), w2_ref[...], 3))) - 0.5
    lw = -jnp.exp(log_w)
    a = jax.nn.sigmoid(a0_ref[...] + _mm(a_lo, a2_ref[...], 3))
    gate = _mm(jax.nn.sigmoid(g_lo), g2_ref[...], 1)

    head_sum = ones_ref[...]

    def head_total(x):
        return jnp.concatenate([_dotx_l(x[:, p * LANES:(p + 1) * LANES], head_sum, HEAD_SUM_PASSES)
                                for p in range(n_pairs)], axis=1)

    kk = k * kk_ref[...]
    kk = kk * lax.rsqrt(jnp.maximum(head_total(kk * kk), 1e-24))
    kh = k * (1.0 + (a - 1.0) * ka_ref[...])
    kb = kk * a
    cum = _dotx_r(tri_ref[...], lw, 3)

    c_row = lax.broadcasted_iota(jnp.int32, (ch, LANES), 0)
    c_col = lax.broadcasted_iota(jnp.int32, (ch, LANES), 1) % hd
    strict = c_col < c_row
    incl = c_col <= c_row
    same_sub = (c_col // RWKV_SUB) == (c_row // RWKV_SUB)
    eye_pair = (c_col == c_row).astype(F32)
    s_row = lax.broadcasted_iota(jnp.int32, (LANES, LANES), 0)
    s_col = lax.broadcasted_iota(jnp.int32, (LANES, LANES), 1)
    same_head = (s_row // hd) == (s_col // hd)
    eye_full = (s_row == s_col).astype(F32)

    n_chunks = rows // ch
    cat = jnp.concatenate
    mm_scores = functools.partial(_mm, passes=RWKV_PASSES["scores"])
    mm_solve = functools.partial(_mm, passes=RWKV_PASSES["solve"])
    mm_values = functools.partial(_mm, passes=RWKV_PASSES["values"])
    mm_state = functools.partial(_mm, passes=RWKV_PASSES["state"])
    a_hat, r_hat, k_til, b_til, k_end, b_end, vp, decay_end = [], [], [], [], [], [], [], []
    for j in range(n_chunks):
        sl = slice(j * ch, (j + 1) * ch)
        cm, lwj, kkj, khj, kbj, rj, vj = cum[sl], lw[sl], kk[sl], kh[sl], kb[sl], r[sl], v[sl]
        cm_end = cm[ch - 1:ch, :]
        e_neg = jnp.exp(-cm)
        e_end = jnp.exp(cm_end - cm)
        wide = (-kkj * jnp.exp(cm - lwj), rj * jnp.exp(cm), khj * e_neg, kbj * e_neg,
                khj * e_end, kbj * e_end, vj, jnp.exp(cm_end))
        for pi in range(n_pairs):
            ls = slice(pi * LANES, (pi + 1) * LANES)
            for dst, src in zip((a_hat, r_hat, k_til, b_til, k_end, b_end, vp, decay_end), wide):
                dst.append(src[:, ls])
    units = range(n_chunks * n_pairs)
    each = lambda fn: [fn(u) for u in units]

    big = each(lambda u: mm_scores(cat([a_hat[u], r_hat[u]], axis=0),
                                   cat([_stack_heads(k_til[u]), _stack_heads(b_til[u])], axis=0),
                                   dot=_dot_nt))
    l_ak = each(lambda u: jnp.where(strict, big[u][0:ch, 0:LANES], 0.0))
    l_ab = each(lambda u: jnp.where(strict, big[u][0:ch, LANES:], 0.0))
    m_rk = each(lambda u: jnp.where(incl, big[u][ch:, 0:LANES], 0.0))
    m_rb = each(lambda u: jnp.where(incl, big[u][ch:, LANES:], 0.0))

    l_d = each(lambda u: jnp.where(same_sub, l_ab[u], 0.0))
    l_o = each(lambda u: l_ab[u] - l_d[u])
    x = each(lambda u: eye_pair + l_d[u])
    p = each(lambda u: mm_solve(l_d[u], _stack_heads(l_d[u])))
    for _ in range(2):
        both = each(lambda u: mm_solve(cat([p[u], x[u]], axis=0), _stack_heads(p[u])))
        x = each(lambda u: x[u] + both[u][ch:])
        p = each(lambda u: both[u][:ch])
    t_d = each(lambda u: x[u] + mm_solve(x[u], _stack_heads(p[u])))
    nn = each(lambda u: mm_solve(t_d[u], _stack_heads(l_o[u])))
    n2 = each(lambda u: mm_solve(nn[u], _stack_heads(nn[u])))
    mx = each(lambda u: eye_pair + nn[u])
    mx = each(lambda u: mx[u] + mm_solve(mx[u], _stack_heads(n2[u])))
    t = each(lambda u: mm_solve(mx[u], _stack_heads(t_d[u])))

    lmv = each(lambda u: mm_values(cat([l_ak[u], m_rk[u]], axis=0), _stack_heads(vp[u])))
    wu = each(lambda u: mm_values(
        t[u], cat([_stack_heads(a_hat[u]), _stack_heads(lmv[u][:ch])], axis=1)))
    w_a = each(lambda u: wu[u][:, 0:LANES])
    u_v = each(lambda u: wu[u][:, LANES:])
    mw = each(lambda u: mm_values(
        m_rb[u], cat([_stack_heads(w_a[u]), _stack_heads(u_v[u])], axis=1)))
    q_eff = each(lambda u: r_hat[u] + mw[u][:, 0:LANES])
    y_loc = each(lambda u: lmv[u][ch:] + mw[u][:, LANES:])
    a_eff = each(lambda u: jnp.where(same_head, mm_state(b_end[u], w_a[u], dot=_dot_tn), 0.0)
                 + eye_full * decay_end[u])
    z_loc = each(lambda u: jnp.where(
        same_head, mm_state(cat([k_end[u], b_end[u]], axis=0), cat([vp[u], u_v[u]], axis=0),
                            dot=_dot_tn), 0.0))

    z = [z_ref[pi] for pi in range(n_pairs)]
    for j in range(n_chunks):
        sl = slice(j * ch, (j + 1) * ch)
        us = [j * n_pairs + pi for pi in range(n_pairs)]
        qz = [mm_state(cat([q_eff[u], a_eff[u]], axis=0), z[pi]) for pi, u in enumerate(us)]
        for pi, u in enumerate(us):
            y_buf[sl, pi * LANES:(pi + 1) * LANES] = qz[pi][:ch] + y_loc[u]
        z = [qz[pi][ch:] + z_loc[u] for pi, u in enumerate(us)]
    for pi in range(n_pairs):
        z_ref[pi] = z[pi]

    y = y_buf[...]
    mean = head_total(y) * (1.0 / hd)
    yc = y - mean
    var = head_total(yc * yc) * (1.0 / hd)
    yn = yc * lax.rsqrt(var + RWKV_LN_EPS) * lnw_ref[...] + lnb_ref[...]
    bonus = head_total(r * kh * rk_ref[...]) * v
    o_ref[...] = ((yn + bonus) * gate).astype(o_ref.dtype)


def _rwkv_mixer(proj, bsz, seq, mu, w0, w2, a0, a2, g2, k_k, k_a, r_k, ln_w, ln_b):
    tb = RWKV_TIME_BLOCK
    nc = seq // tb
    d = D_MODEL
    lora_w = 2 * LANES + RWKV_LORA_G
    mur, muk, muv = _row2(mu[0:d]), _row2(mu[d:2 * d]), _row2(mu[2 * d:3 * d])
    o = 3 * d
    pad_l = lambda x, n: jnp.pad(x, (0, n - x.shape[0]))
    mulo = _row2(jnp.concatenate([pad_l(mu[o:o + RWKV_LORA_W], LANES),
                                  pad_l(mu[o + RWKV_LORA_W:o + RWKV_LORA_W + RWKV_LORA_A], LANES),
                                  mu[o + RWKV_LORA_W + RWKV_LORA_A:]]))
    w2p = jnp.pad(w2, ((0, LANES - RWKV_LORA_W), (0, 0)))
    a2p = jnp.pad(a2, ((0, LANES - RWKV_LORA_A), (0, 0)))
    ch = RWKV_CHUNK
    t_i = jnp.arange(tb)[:, None]
    t_j = jnp.arange(tb)[None, :]
    tri = ((t_i >= t_j) & (t_i // ch == t_j // ch)).astype(BF16)
    l_i = jnp.arange(LANES)[:, None]
    l_j = jnp.arange(LANES)[None, :]
    head_sum = (l_i // RWKV_HEAD_DIM == l_j // RWKV_HEAD_DIM).astype(BF16)

    gw = RWKV_GROUP * LANES
    row = lambda b, p, c: b * nc + c
    col = lambda base: (lambda b, p, c: (row(b, p, c), base // gw + p))
    vec = pl.BlockSpec((1, gw), lambda b, p, c: (0, p))
    in_specs = [
        pl.BlockSpec((tb, gw), col(C_R)),
        pl.BlockSpec((tb, gw), col(C_K)),
        pl.BlockSpec((tb, gw), col(C_V)),
        pl.BlockSpec((tb, lora_w), lambda b, p, c: (row(b, p, c), C_LORA // lora_w)),
        vec, vec, vec,
        pl.BlockSpec((1, lora_w), lambda b, p, c: (0, 0)),
        vec,
        pl.BlockSpec((LANES, gw), lambda b, p, c: (0, p)),
        vec,
        pl.BlockSpec((LANES, gw), lambda b, p, c: (0, p)),
        pl.BlockSpec((RWKV_LORA_G, gw), lambda b, p, c: (0, p)),
        vec, vec, vec, vec, vec,
        pl.BlockSpec((tb, tb), lambda b, p, c: (0, 0)),
        pl.BlockSpec((LANES, LANES), lambda b, p, c: (0, 0)),
    ]
    return pl.pallas_call(
        _rwkv_body,
        grid=(bsz, RWKV_PAIRS // RWKV_GROUP, nc),
        in_specs=in_specs,
        out_specs=pl.BlockSpec((tb, gw), lambda b, p, c: (row(b, p, c), p)),
        out_shape=jax.ShapeDtypeStruct((bsz * seq, D_MODEL), BF16),
        scratch_shapes=[pltpu.VMEM((tb + HALO, gw), F32),
                        pltpu.VMEM((tb + HALO, gw), F32),
                        pltpu.VMEM((tb + HALO, gw), F32),
                        pltpu.VMEM((tb + HALO, lora_w), F32),
                        pltpu.VMEM((RWKV_GROUP, LANES, LANES), F32),
                        pltpu.VMEM((tb, gw), F32)],
        compiler_params=_cparams(3),
        name="rwkv_mixer",
    )(proj, proj, proj, proj, mur, muk, muv, mulo, _row2(w0), w2p, _row2(a0), a2p, g2,
      _row2(k_k), _row2(k_a), _row2(r_k.reshape(-1)), _row2(ln_w), _row2(ln_b), tri, head_sum)


def _gated_merge_body(ys_ref, yr_ref, ym_ref, ps_ref, pr_ref, pm_ref, gs_ref, gr_ref, gm_ref, o_ref):
    acc = jax.nn.sigmoid(gs_ref[...]) * _dot(ys_ref[...], ps_ref[...])
    acc = acc + jax.nn.sigmoid(gr_ref[...]) * _dot(yr_ref[...], pr_ref[...])
    acc = acc + jax.nn.sigmoid(gm_ref[...]) * _dot(ym_ref[...], pm_ref[...])
    o_ref[...] = acc.astype(o_ref.dtype)


def _gated_merge(ys, ps, proj, tm=TM_GATED, tn=TN_GATED):
    m, k = ys[0].shape
    tm = min(tm, m)
    n = ps[0].shape[1]
    y_spec = pl.BlockSpec((tm, k), lambda j, i: (i, 0))
    p_spec = pl.BlockSpec((k, tn), lambda j, i: (0, j), pipeline_mode=pl.Buffered(1))
    gate_spec = lambda col: pl.BlockSpec((tm, tn), lambda j, i: (i, col // tn + j))
    return pl.pallas_call(
        _gated_merge_body,
        grid=(n // tn, m // tm),
        in_specs=[y_spec] * 3 + [p_spec] * 3 + [gate_spec(C_GS), gate_spec(C_GR), gate_spec(C_GM)],
        out_specs=pl.BlockSpec((tm, tn), lambda j, i: (i, j)),
        out_shape=jax.ShapeDtypeStruct((m, n), BF16),
        compiler_params=_cparams(2),
        name="gated_merge",
    )(*ys, *ps, proj, proj, proj)


def _mm_res_ln_body(a_ref, w_ref, h_ref, g_ref, b_ref, o_ref, ob_ref, *, row_split):
    n = a_ref.shape[0] // row_split
    subs = [slice(s * n, (s + 1) * n) for s in range(row_split)]
    acc = [_dot(a_ref[rows, :], w_ref[...]) for rows in subs]
    for rows, mix in zip(subs, acc):
        h = _layer_norm_rows(ALPHA * h_ref[rows, :] + mix, g_ref[...], b_ref[...])
        o_ref[rows, :] = h
        ob_ref[rows, :] = h.astype(BF16)


def _matmul_res_ln(a, w, h, g, b, tm, row_split):
    m, k = a.shape
    n = w.shape[1]
    const = lambda shape: pl.BlockSpec(shape, lambda i: (0, 0), pipeline_mode=pl.Buffered(1))
    return pl.pallas_call(
        functools.partial(_mm_res_ln_body, row_split=row_split),
        grid=(m // tm,),
        in_specs=[pl.BlockSpec((tm, k), lambda i: (i, 0)),
                  const((k, n)),
                  pl.BlockSpec((tm, n), lambda i: (i, 0)),
                  const((1, n)), const((1, n))],
        out_specs=[pl.BlockSpec((tm, n), lambda i: (i, 0)),
                   pl.BlockSpec((tm, n), lambda i: (i, 0))],
        out_shape=[jax.ShapeDtypeStruct((m, n), F32), jax.ShapeDtypeStruct((m, n), BF16)],
        compiler_params=_cparams(1),
        name="matmul_res_ln",
    )(a, w, h, _row2(g), _row2(b))


FFN_HALO = 16


def _ffn_up_body(h_ref, hp_ref, wg_ref, wua_ref, wub_ref, cwg_ref, cbg_ref, cwu_ref, cbu_ref, o_ref,
                 gbuf, ubuf, wg_s, wu_s, *, blocks_per_seq):
    rows = h_ref.shape[0]
    tn = o_ref.shape[1]
    first = (pl.program_id(1) % blocks_per_seq) == 0
    keep = jnp.where(first, 0.0, 1.0)

    @pl.when(pl.program_id(1) == 0)
    def _():
        col = pl.program_id(0) * tn + lax.broadcasted_iota(jnp.int32, wg_s.shape, 1)
        off = FFN_D % tn
        up = jnp.concatenate([wua_ref[:, off:], wub_ref[:, :off]], axis=1)
        wg_s[...] = jnp.where(col < FFN_D, wg_ref[...], 0.0).astype(BF16)
        wu_s[...] = jnp.where(col < FFN_D, up, 0.0).astype(BF16)

    def conv(buf, cw_ref, cb_ref):
        acc = cb_ref[...] + cw_ref[2:3, :] * buf[FFN_HALO:FFN_HALO + rows, :]
        acc = acc + cw_ref[1:2, :] * buf[FFN_HALO - 1:FFN_HALO - 1 + rows, :]
        return acc + cw_ref[0:1, :] * buf[FFN_HALO - 2:FFN_HALO - 2 + rows, :]

    for buf, w_ref in ((gbuf, wg_s), (ubuf, wu_s)):
        buf[0:FFN_HALO, :] = _dot(hp_ref[...], w_ref[...]) * keep
        buf[FFN_HALO:FFN_HALO + rows, :] = _dot(h_ref[...], w_ref[...])
    gate = conv(gbuf, cwg_ref, cbg_ref)
    up = conv(ubuf, cwu_ref, cbu_ref)
    gelu = 0.5 * gate * (1.0 + lax.erf(gate * (2.0 ** -0.5)))
    o_ref[...] = (gelu * up).astype(o_ref.dtype)


def _ffn_up(hb, w_up, layer, cwg, cbg, cwu, cbu, seq, tm):
    m, k = hb.shape
    n = FFN_PAD
    tn = FFN_TN
    halo_blocks = tm // FFN_HALO
    up0 = FFN_D // tn
    body = functools.partial(_ffn_up_body, blocks_per_seq=seq // tm)
    w_spec = lambda first_block: pl.BlockSpec((None, k, tn), lambda j, i: (layer, 0, first_block + j))
    return pl.pallas_call(
        body,
        grid=(n // tn, m // tm),
        in_specs=[pl.BlockSpec((tm, k), lambda j, i: (i, 0)),
                  pl.BlockSpec((FFN_HALO, k), lambda j, i: (jnp.maximum(i * halo_blocks - 1, 0), 0)),
                  w_spec(0), w_spec(up0), w_spec(up0 + 1),
                  pl.BlockSpec((3, tn), lambda j, i: (0, j)),
                  pl.BlockSpec((1, tn), lambda j, i: (0, j)),
                  pl.BlockSpec((3, tn), lambda j, i: (0, j)),
                  pl.BlockSpec((1, tn), lambda j, i: (0, j))],
        out_specs=pl.BlockSpec((tm, tn), lambda j, i: (i, j)),
        out_shape=jax.ShapeDtypeStruct((m, n), BF16),
        scratch_shapes=[pltpu.VMEM((tm + FFN_HALO, tn), F32),
                        pltpu.VMEM((tm + FFN_HALO, tn), F32),
                        pltpu.VMEM((k, tn), BF16),
                        pltpu.VMEM((k, tn), BF16)],
        compiler_params=_cparams(2),
        name="ffn_up",
    )(hb, hb, w_up, w_up, w_up, cwg, cbg, cwu, cbu)


def _pack_segments():
    d = D_MODEL
    order = (("z", d), ("x", d), ("bc", 2 * SSD_GN), ("dt", SSD_HEADS), ("r", d), ("k", d), ("v", d),
             ("w_lo", RWKV_LORA_W), ("a_lo", RWKV_LORA_A), ("g_lo", RWKV_LORA_G), ("qk", d),
             ("mv", d), ("mo", d), ("gi", ML_HEADS), ("gf", ML_HEADS), ("gs", d), ("gr", d), ("gm", d))
    dst = dict(z=C_Z, x=C_X, r=C_R, k=C_K, v=C_V, qk=C_QK, mv=C_MV, mo=C_MO, gs=C_GS, gr=C_GR,
               gm=C_GM, bc=C_BC, w_lo=C_LORA, a_lo=C_LORA + LANES, g_lo=C_LORA + 2 * LANES,
               dt=C_SMALL, gi=C_SMALL + SSD_HEADS, gf=C_SMALL + SSD_HEADS + ML_HEADS)
    segments, src = [], 0
    for name, width in order:
        segments.append((src, width, dst[name]))
        src += width
    return tuple(segments), src


def _pack_body(wt_ref, o_ref, *, segments):
    group, start, pos = [], 0, 0

    def flush():
        nonlocal group, start
        rows = group[0] if len(group) == 1 else jnp.concatenate(group, axis=0)
        o_ref[:, start:pos] = rows.T.astype(o_ref.dtype)
        group, start = [], pos

    for src, width, dst in sorted(segments, key=lambda s: s[2]) + [(None, 0, N_PROJ)]:
        if dst > pos:
            group.append(jnp.zeros((dst - pos, LANES), F32))
            pos = dst
            if pos % LANES == 0:
                flush()
        if width:
            group.append(wt_ref[src:src + width, :])
            pos += width
            if pos % LANES == 0:
                flush()


def _pack_w_in(w_in, layer):
    segments, d_in = _pack_segments()
    d = w_in.shape[1]
    return pl.pallas_call(
        functools.partial(_pack_body, segments=segments),
        grid=(d // LANES,),
        in_specs=[pl.BlockSpec((None, d_in, LANES), lambda i: (layer, 0, i))],
        out_specs=pl.BlockSpec((LANES, N_PROJ), lambda i: (i, 0)),
        out_shape=jax.ShapeDtypeStruct((d, N_PROJ), BF16),
        compiler_params=_cparams(1),
        name="pack_w_in",
    )(jnp.swapaxes(w_in, 1, 2))


def _to_bf16_body(w_ref, o_ref, *, valid_rows):
    x = w_ref[...]
    if valid_rows is not None:
        row = pl.program_id(0) * o_ref.shape[0] + lax.broadcasted_iota(jnp.int32, o_ref.shape, 0)
        x = jnp.where(row < valid_rows, x, 0.0)
    o_ref[...] = x.astype(o_ref.dtype)


def _to_bf16(w_stack, layer, out_rows=None, rows_per_step=512):
    _, rows, cols = w_stack.shape
    out_rows = rows if out_rows is None else out_rows
    tr = rows_per_step
    last = (rows - 1) // tr
    return pl.pallas_call(
        functools.partial(_to_bf16_body, valid_rows=None if out_rows == rows else rows),
        grid=(out_rows // tr,),
        in_specs=[pl.BlockSpec((None, tr, cols), lambda i: (layer, jnp.minimum(i, last), 0))],
        out_specs=pl.BlockSpec((tr, cols), lambda i: (i, 0)),
        out_shape=jax.ShapeDtypeStruct((out_rows, cols), BF16),
        compiler_params=_cparams(1),
        name="to_bf16",
    )(w_stack)


def _pad_cols(x, n):
    return jnp.pad(x, ((0, 0), (0, n - x.shape[1])))


def kernel(x, ln_in_g, ln_in_b, w_in, ssd_conv_w, ssd_conv_b, ssd_dt_bias, ssd_a_log, ssd_d, ssd_norm_w, rwkv_mu, rwkv_w0, rwkv_w2, rwkv_a0, rwkv_a2, rwkv_g2, rwkv_k_k, rwkv_k_a, rwkv_r_k, rwkv_ln_w, rwkv_ln_b, mlstm_conv_w, mlstm_conv_b, mlstm_i_bias, mlstm_f_bias, mlstm_norm_w, proj_ssd, proj_rwkv, proj_mlstm, w_out, ln1_g, ln1_b, ffn_w_up, ffn_conv_w, ffn_conv_b, ffn_w_down, ln2_g, ln2_b):
    bsz, seq, d = x.shape
    n_tok = bsz * seq
    tb = TIME_BLOCK
    t_i = jnp.arange(tb)[:, None]
    t_j = jnp.arange(tb)[None, :]
    tri = (t_i >= t_j).astype(BF16)

    h, hb = _ln_in(x.reshape(n_tok, d), ln_in_g, ln_in_b)
    for l in range(DEPTH):
        proj = _matmul(hb, _pack_w_in(w_in, l), tm=min(TM_IN_PROJ, n_tok), tn=TN_IN_PROJ,
                       out_dtype=F32, name="in_proj")
        y_ssd = _ssd_mixer(proj, bsz, seq, ssd_conv_w[l], ssd_conv_b[l], ssd_dt_bias[l],
                           ssd_a_log[l], ssd_d[l], ssd_norm_w[l], tri)
        y_rwkv = _rwkv_mixer(proj, bsz, seq, rwkv_mu[l], rwkv_w0[l], rwkv_w2[l], rwkv_a0[l],
                             rwkv_a2[l], rwkv_g2[l], rwkv_k_k[l], rwkv_k_a[l], rwkv_r_k[l],
                             rwkv_ln_w[l], rwkv_ln_b[l])
        y_mlstm = _mlstm_mixer(proj, bsz, seq, mlstm_conv_w[l], mlstm_conv_b[l], mlstm_i_bias[l],
                               mlstm_f_bias[l], mlstm_norm_w[l], tri)
        merged = _gated_merge(
            (y_ssd, y_rwkv, y_mlstm),
            (_to_bf16(proj_ssd, l), _to_bf16(proj_rwkv, l), _to_bf16(proj_mlstm, l)), proj)
        h, hb = _matmul_res_ln(merged, _to_bf16(w_out, l), h, ln1_g[l], ln1_b[l],
                               tm=min(TM_W_OUT, n_tok), row_split=4)

        cwg = _pad_cols(ffn_conv_w[l][:, :FFN_D], FFN_PAD)
        cwu = _pad_cols(ffn_conv_w[l][:, FFN_D:], FFN_PAD)
        cbg = _pad_cols(_row2(ffn_conv_b[l][:FFN_D]), FFN_PAD)
        cbu = _pad_cols(_row2(ffn_conv_b[l][FFN_D:]), FFN_PAD)
        act = _ffn_up(hb, ffn_w_up, l, cwg, cbg, cwu, cbu, seq, tm=min(TM_FFN_UP, seq))
        wd = _to_bf16(ffn_w_down, l, out_rows=FFN_PAD)
        h, hb = _matmul_res_ln(act, wd, h, ln2_g[l], ln2_b[l], tm=min(TM_FFN_DOWN, n_tok),
                               row_split=1)
    return h.reshape(bsz, seq, d)
```

```python
import functools

import jax
import jax.numpy as jnp
from jax import lax
from jax.experimental import pallas as pl
from jax.experimental.pallas import tpu as pltpu

F32 = jnp.float32
BF16 = jnp.bfloat16

D_MODEL = 2048
DEPTH = 2
ALPHA = (2.0 * DEPTH) ** 0.25
LN_EPS = 1e-5

SSD_HEADS = 32
SSD_HEAD_DIM = 64
SSD_GROUPS = 4
SSD_STATE = 128
SSD_GN = SSD_GROUPS * SSD_STATE
SSD_EPS = 1e-5
SSD_GROUP_W = D_MODEL // SSD_GROUPS
SSD_HPG = SSD_HEADS // SSD_GROUPS

RWKV_HEAD_DIM = 64
RWKV_PAIRS = D_MODEL // 128
RWKV_LORA_W = 96
RWKV_LORA_A = 96
RWKV_LORA_G = 256
RWKV_LN_EPS = 64e-5
RWKV_CHUNK = 64
RWKV_SUB = 8

ML_HEADS = 8
ML_DK = 128
ML_DV = 256
ML_EPS = 1e-6
ML_GROUP = 8

FFN_D = 5504
FFN_PAD = 5632
FFN_TN = 512

LANES = 128
HALO = 8
TIME_BLOCK = 128

C_Z = 0
C_X = 2048
C_R = 4096
C_K = 6144
C_V = 8192
C_QK = 10240
C_MV = 12288
C_MO = 14336
C_GS = 16384
C_GR = 18432
C_GM = 20480
C_BC = 22528
C_LORA = 23552
C_SMALL = 24064
N_PROJ = 24576

VMEM_LIMIT = 56 * 1024 * 1024

TM_LN_IN = 512
TM_IN_PROJ, TN_IN_PROJ = 1024, 2048
TM_GATED, TN_GATED = 512, 1024
TM_W_OUT = 512
TM_FFN_UP = 1024
TM_FFN_DOWN = 256


def _dot(a, b):
    return jnp.dot(a, b, preferred_element_type=F32)


def _dot_nt(a, b):
    return lax.dot_general(a, b, (((1,), (1,)), ((), ())), preferred_element_type=F32)


def _dot_tn(a, b):
    return lax.dot_general(a, b, (((0,), (0,)), ((), ())), preferred_element_type=F32)


def _split(x, n):
    parts, r = [], x
    for i in range(n):
        p = r.astype(BF16)
        parts.append(p)
        if i + 1 < n:
            r = r - p.astype(F32)
    return parts


def _dotx_l(x, w_exact, n, dot=_dot):
    acc = None
    for p in _split(x, n):
        t = dot(p, w_exact)
        acc = t if acc is None else acc + t
    return acc


def _dotx_r(w_exact, x, n, dot=_dot):
    acc = None
    for p in _split(x, n):
        t = dot(w_exact, p)
        acc = t if acc is None else acc + t
    return acc


def _mm(a, b, passes, dot=_dot):
    if passes == 1:
        return dot(a.astype(BF16), b.astype(BF16))
    ah, al = _split(a, 2)
    bh, bl = _split(b, 2)
    return dot(ah, bh) + (dot(ah, bl) + dot(al, bh))


def _layer_norm_rows(x, g, b):
    mu = jnp.mean(x, axis=-1, keepdims=True)
    xc = x - mu
    var = jnp.mean(xc * xc, axis=-1, keepdims=True)
    return xc * lax.rsqrt(var + LN_EPS) * g + b


def _cparams(n_axes):
    return pltpu.CompilerParams(dimension_semantics=("arbitrary",) * n_axes,
                                vmem_limit_bytes=VMEM_LIMIT)


def _row2(v):
    return v.reshape(1, -1)


def _ln_in_body(x_ref, g_ref, b_ref, h_ref, hb_ref):
    h = _layer_norm_rows(x_ref[...], g_ref[...], b_ref[...])
    h_ref[...] = h
    hb_ref[...] = h.astype(BF16)


def _ln_in(x2d, g, b, tm=TM_LN_IN):
    n, d = x2d.shape
    tm = min(tm, n)
    return pl.pallas_call(
        _ln_in_body,
        grid=(n // tm,),
        in_specs=[pl.BlockSpec((tm, d), lambda i: (i, 0)),
                  pl.BlockSpec((1, d), lambda i: (0, 0)),
                  pl.BlockSpec((1, d), lambda i: (0, 0))],
        out_specs=[pl.BlockSpec((tm, d), lambda i: (i, 0)),
                   pl.BlockSpec((tm, d), lambda i: (i, 0))],
        out_shape=[jax.ShapeDtypeStruct((n, d), F32), jax.ShapeDtypeStruct((n, d), BF16)],
        compiler_params=_cparams(1),
        name="ln_in",
    )(x2d, _row2(g), _row2(b))


def _mm_body(a_ref, w_ref, o_ref):
    o_ref[...] = _dot(a_ref[...], w_ref[...]).astype(o_ref.dtype)


def _matmul(a, w, tm, tn, out_dtype, name):
    m, k = a.shape
    _, n = w.shape
    return pl.pallas_call(
        _mm_body,
        grid=(n // tn, m // tm),
        in_specs=[pl.BlockSpec((tm, k), lambda j, i: (i, 0)),
                  pl.BlockSpec((k, tn), lambda j, i: (0, j))],
        out_specs=pl.BlockSpec((tm, tn), lambda j, i: (i, j)),
        out_shape=jax.ShapeDtypeStruct((m, n), out_dtype),
        compiler_params=_cparams(2),
        name=name,
    )(a, w)


CONV_HALO = 16


def _conv_shift_matrix(rows):
    t = jnp.arange(rows)[:, None]
    c = jnp.arange(3 * (rows + CONV_HALO))[None, :]
    k, j = c // (rows + CONV_HALO), c % (rows + CONV_HALO)
    return (j == CONV_HALO + t - 3 + k).astype(BF16)


def _conv4_silu(buf, src_ref, w_ref, b_ref, shift_ref, rows):
    x = src_ref[...]
    buf[CONV_HALO:CONV_HALO + rows, :] = x
    hist = buf[...]
    taps = jnp.concatenate([(hist * w_ref[k:k + 1, :]).astype(BF16) for k in range(3)], axis=0)
    acc = b_ref[...] + w_ref[3:4, :] * x + _dot(shift_ref[...], taps)
    buf[0:CONV_HALO, :] = buf[rows:rows + CONV_HALO, :]
    return jax.nn.silu(acc)


def _ssd_body(z_ref, x_ref, bc_ref, sm_ref, cwx_ref, cbx_ref, cwbc_ref, cbbc_ref, dtb_ref,
              alog_ref, dsk_ref, nw_ref, e_ref, tri_ref, shift_ref, o_ref, xbuf, bcbuf, st_ref):
    rows = x_ref.shape[0]
    gw = SSD_GROUP_W
    groups = range(SSD_GROUPS)

    @pl.when(pl.program_id(1) == 0)
    def _():
        xbuf[0:CONV_HALO, :] = jnp.zeros((CONV_HALO, xbuf.shape[1]), F32)
        bcbuf[0:CONV_HALO, :] = jnp.zeros((CONV_HALO, bcbuf.shape[1]), F32)
        st_ref[...] = jnp.zeros(st_ref.shape, F32)

    xs_all = _conv4_silu(xbuf, x_ref, cwx_ref, cbx_ref, shift_ref, rows)
    bc = _conv4_silu(bcbuf, bc_ref, cwbc_ref, cbbc_ref, shift_ref, rows)

    dt = jax.nn.softplus(sm_ref[...] + dtb_ref[...])
    adt = dt * (-jnp.exp(alog_ref[...]))
    a_cs = _dotx_r(tri_ref[...], adt, 3)
    a_cs_t = a_cs.T

    t_row = lax.broadcasted_iota(jnp.int32, (rows, rows), 0)
    t_col = lax.broadcasted_iota(jnp.int32, (rows, rows), 1)
    causal = t_row >= t_col
    lane = lax.broadcasted_iota(jnp.int32, (rows, LANES), 1)

    xs = [xs_all[:, g * gw:(g + 1) * gw] for g in groups]
    bm = [bc[:, g * SSD_STATE:(g + 1) * SSD_STATE].astype(BF16) for g in groups]
    cm = [bc[:, SSD_GN + g * SSD_STATE:SSD_GN + (g + 1) * SSD_STATE].astype(BF16) for g in groups]
    a_exp = [_dotx_l(a_cs, e_ref[g], 3) for g in groups]
    dt_exp = [_dotx_l(dt, e_ref[g], 3) for g in groups]
    scores = [_dot_nt(cm[g], bm[g]) for g in groups]
    st = [st_ref[g] for g in groups]
    y_off = [_dot(cm[g], st[g].astype(BF16)) for g in groups]
    xdt = [xs[g] * dt_exp[g] for g in groups]

    y_diag = []
    for g in groups:
        parts = []
        for pr in range(SSD_HPG // 2):
            xp = xdt[g][:, pr * LANES:(pr + 1) * LANES]
            acc = None
            for s in range(2):
                hh = g * SSD_HPG + 2 * pr + s
                seg = a_cs[:, hh:hh + 1] - a_cs_t[hh:hh + 1, :]
                m = jnp.where(causal, jnp.exp(seg), 0.0) * scores[g]
                head_lanes = (lane < SSD_HEAD_DIM) if s == 0 else (lane >= SSD_HEAD_DIM)
                xm = jnp.where(head_lanes, xp, 0.0)
                t = _dot(m.astype(BF16), xm.astype(BF16))
                acc = t if acc is None else acc + t
            parts.append(acc)
        y_diag.append(jnp.concatenate(parts, axis=1))

    for g in groups:
        cols = slice(g * gw, (g + 1) * gw)
        a_last = a_exp[g][rows - 1:rows, :]
        st_ref[g] = jnp.exp(a_last) * st[g] + _dot_tn(
            bm[g], (xdt[g] * jnp.exp(a_last - a_exp[g])).astype(BF16))
        y = y_diag[g] + y_off[g] * jnp.exp(a_exp[g]) + xs[g] * dsk_ref[:, cols]
        y = y * jax.nn.silu(z_ref[:, cols])
        ms = jnp.mean(y * y, axis=-1, keepdims=True)
        o_ref[:, cols] = (y * lax.rsqrt(ms + SSD_EPS) * nw_ref[:, cols]).astype(o_ref.dtype)


def _ssd_mixer(proj, bsz, seq, conv_w, conv_b, dt_bias, a_log, d_skip, norm_w, tri):
    tb = TIME_BLOCK
    nc = seq // tb
    gw = SSD_GROUP_W
    d = D_MODEL
    bcw = 2 * SSD_GN
    cwx, cwbc = conv_w[:, :d], conv_w[:, d:]
    cbx, cbbc = _row2(conv_b[:d]), _row2(conv_b[d:])
    head_lanes = lambda v: _row2(jnp.pad(v, (0, LANES - SSD_HEADS)))
    grp = jnp.arange(SSD_GROUPS)[:, None, None]
    src = jnp.arange(LANES)[None, :, None]
    dst = jnp.arange(gw)[None, None, :]
    expand = (src == grp * SSD_HPG + dst // SSD_HEAD_DIM).astype(BF16)

    row = lambda b, c: b * nc + c
    full = lambda shape: pl.BlockSpec(shape, lambda b, c: (0,) * len(shape))
    in_specs = [
        pl.BlockSpec((tb, d), lambda b, c: (row(b, c), C_Z // d)),
        pl.BlockSpec((tb, d), lambda b, c: (row(b, c), C_X // d)),
        pl.BlockSpec((tb, bcw), lambda b, c: (row(b, c), C_BC // bcw)),
        pl.BlockSpec((tb, LANES), lambda b, c: (row(b, c), C_SMALL // LANES)),
        full((4, d)), full((1, d)), full((4, bcw)), full((1, bcw)),
        full((1, LANES)), full((1, LANES)), full((1, d)), full((1, d)),
        full((SSD_GROUPS, LANES, gw)), full((tb, tb)), full((tb, 3 * (tb + CONV_HALO))),
    ]
    return pl.pallas_call(
        _ssd_body,
        grid=(bsz, nc),
        in_specs=in_specs,
        out_specs=pl.BlockSpec((tb, d), lambda b, c: (row(b, c), 0)),
        out_shape=jax.ShapeDtypeStruct((bsz * seq, d), BF16),
        scratch_shapes=[pltpu.VMEM((tb + CONV_HALO, d), F32),
                        pltpu.VMEM((tb + CONV_HALO, bcw), F32),
                        pltpu.VMEM((SSD_GROUPS, SSD_STATE, gw), F32)],
        compiler_params=_cparams(2),
        name="ssd_mixer",
    )(proj, proj, proj, proj, cwx, cbx, cwbc, cbbc, head_lanes(dt_bias), head_lanes(a_log),
      _row2(jnp.repeat(d_skip, SSD_HEAD_DIM)), _row2(norm_w), expand, tri, _conv_shift_matrix(tb))


def _mlstm_body(q_ref, k_ref, v_ref, og_ref, sm_ref, sel_ref, cwq_ref, cbq_ref, cwk_ref, cbk_ref,
                gb_ref, nw_ref, tri_ref, shift_ref, o_ref, qbuf, kbuf, ct_ref, n_ref, m_ref):
    rows = q_ref.shape[0]

    @pl.when(pl.program_id(2) == 0)
    def _():
        qbuf[0:CONV_HALO, :] = jnp.zeros((CONV_HALO, qbuf.shape[1]), F32)
        kbuf[0:CONV_HALO, :] = jnp.zeros((CONV_HALO, kbuf.shape[1]), F32)
        ct_ref[...] = jnp.zeros(ct_ref.shape, F32)
        n_ref[...] = jnp.zeros(n_ref.shape, F32)
        m_ref[...] = jnp.zeros(m_ref.shape, F32)

    heads = range(q_ref.shape[1] // ML_DK)
    q_all = _conv4_silu(qbuf, q_ref, cwq_ref, cbq_ref, shift_ref, rows) * (ML_DK ** -0.5)
    k_all = _conv4_silu(kbuf, k_ref, cwk_ref, cbk_ref, shift_ref, rows)
    qf = [q_all[:, i * ML_DK:(i + 1) * ML_DK] for i in heads]
    q = [qf[i].astype(BF16) for i in heads]
    k = [k_all[:, i * ML_DK:(i + 1) * ML_DK] for i in heads]
    kb = [k[i].astype(BF16) for i in heads]
    v = [v_ref[:, i * ML_DV:(i + 1) * ML_DV] for i in heads]

    pre = _dotx_l(sm_ref[...], sel_ref[...], 3) + gb_ref[...]
    log_f = jax.nn.log_sigmoid(pre)
    bcum_all = _dotx_r(tri_ref[...], log_f, 3)
    lane = lax.broadcasted_iota(jnp.int32, (rows, LANES), 1)
    gates_t = jnp.where(lane % 2 == 0, pre, bcum_all).T
    t_row = lax.broadcasted_iota(jnp.int32, (rows, rows), 0)
    t_col = lax.broadcasted_iota(jnp.int32, (rows, rows), 1)
    causal = t_row >= t_col

    li = [pre[:, 2 * i:2 * i + 1] for i in heads]
    bcum = [bcum_all[:, 2 * i + 1:2 * i + 2] for i in heads]
    log_d = [jnp.where(causal, bcum[i] - gates_t[2 * i + 1:2 * i + 2, :] + gates_t[2 * i:2 * i + 1, :],
                       -jnp.inf) for i in heads]
    qk = [_dot_nt(q[i], kb[i]) for i in heads]
    ct = [ct_ref[i] for i in heads]
    inter = [_dot(q[i], ct[i].astype(BF16)) for i in heads]
    m_prev = [m_ref[i, 0:1, 0:1] for i in heads]
    m_inter = [bcum[i] + m_prev[i] for i in heads]
    m_t = [jnp.maximum(jnp.max(log_d[i], axis=-1, keepdims=True), m_inter[i]) for i in heads]
    w = [jnp.exp(log_d[i] - m_t[i]) * qk[i] for i in heads]
    s_inter = [jnp.exp(m_inter[i] - m_t[i]) for i in heads]
    nvec = [n_ref[i, 0:1, :] for i in heads]
    num = [_dot(w[i].astype(BF16), v[i].astype(BF16)) + s_inter[i] * inter[i] for i in heads]
    den = [jnp.sum(w[i], axis=-1, keepdims=True)
           + s_inter[i] * jnp.sum(qf[i] * nvec[i], axis=-1, keepdims=True)
           for i in heads]
    h = [num[i] / jnp.maximum(jnp.abs(den[i]), jnp.exp(-m_t[i])) for i in heads]

    b_tot = [bcum[i][rows - 1:rows, :] for i in heads]
    log_w = [b_tot[i] - bcum[i] + li[i] for i in heads]
    m_new = [jnp.maximum(b_tot[i] + m_prev[i], jnp.max(log_w[i], axis=0, keepdims=True))
             for i in heads]
    w_end = [jnp.exp(log_w[i] - m_new[i]) for i in heads]
    s_old = [jnp.exp(b_tot[i] + m_prev[i] - m_new[i]) for i in heads]
    ct_new = [s_old[i] * ct[i] + _dot_tn(kb[i], (w_end[i] * v[i]).astype(BF16)) for i in heads]
    for i in heads:
        ct_ref[i] = ct_new[i]
        n_ref[i] = jnp.broadcast_to(
            s_old[i] * nvec[i] + jnp.sum(w_end[i] * k[i], axis=0, keepdims=True), n_ref.shape[1:])
        m_ref[i] = jnp.broadcast_to(m_new[i], m_ref.shape[1:])
        hn = h[i] * lax.rsqrt(jnp.mean(h[i] * h[i], axis=-1, keepdims=True) + ML_EPS)
        cols = slice(i * ML_DV, (i + 1) * ML_DV)
        o_ref[:, cols] = (jax.nn.sigmoid(og_ref[:, cols]) * hn * nw_ref[:, cols]).astype(o_ref.dtype)


def _mlstm_mixer(proj, bsz, seq, conv_w, conv_b, i_bias, f_bias, norm_w, tri):
    tb = TIME_BLOCK
    nc = seq // tb
    hg = ML_GROUP
    n_groups = ML_HEADS // hg
    qw, vw = hg * ML_DK, hg * ML_DV
    small_i = SSD_HEADS
    small_f = SSD_HEADS + ML_HEADS
    src = jnp.arange(LANES)[None, :, None]
    dst = jnp.arange(LANES)[None, None, :]
    grp = jnp.arange(n_groups)[:, None, None]
    head = grp * hg + dst // 2
    sel = ((dst < 2 * hg) & (((dst % 2 == 0) & (src == small_i + head))
                             | ((dst % 2 == 1) & (src == small_f + head)))).astype(BF16)
    gate_bias = jnp.stack([i_bias.reshape(n_groups, hg), f_bias.reshape(n_groups, hg)], axis=-1)
    gate_bias = jnp.pad(gate_bias.reshape(n_groups, 1, 2 * hg), ((0, 0), (0, 0), (0, LANES - 2 * hg)))
    conv_b2 = _row2(conv_b)
    kq = n_groups
    row = lambda b, h, c: b * nc + c
    in_specs = [
        pl.BlockSpec((tb, qw), lambda b, h, c: (row(b, h, c), C_QK // qw + h)),
        pl.BlockSpec((tb, qw), lambda b, h, c: (row(b, h, c), C_QK // qw + kq + h)),
        pl.BlockSpec((tb, vw), lambda b, h, c: (row(b, h, c), C_MV // vw + h)),
        pl.BlockSpec((tb, vw), lambda b, h, c: (row(b, h, c), C_MO // vw + h)),
        pl.BlockSpec((tb, LANES), lambda b, h, c: (row(b, h, c), C_SMALL // LANES)),
        pl.BlockSpec((None, LANES, LANES), lambda b, h, c: (h, 0, 0)),
        pl.BlockSpec((4, qw), lambda b, h, c: (0, h)),
        pl.BlockSpec((1, qw), lambda b, h, c: (0, h)),
        pl.BlockSpec((4, qw), lambda b, h, c: (0, kq + h)),
        pl.BlockSpec((1, qw), lambda b, h, c: (0, kq + h)),
        pl.BlockSpec((None, 1, LANES), lambda b, h, c: (h, 0, 0)),
        pl.BlockSpec((1, vw), lambda b, h, c: (0, h)),
        pl.BlockSpec((tb, tb), lambda b, h, c: (0, 0)),
        pl.BlockSpec((tb, 3 * (tb + CONV_HALO)), lambda b, h, c: (0, 0)),
    ]
    return pl.pallas_call(
        _mlstm_body,
        grid=(bsz, n_groups, nc),
        in_specs=in_specs,
        out_specs=pl.BlockSpec((tb, vw), lambda b, h, c: (row(b, h, c), h)),
        out_shape=jax.ShapeDtypeStruct((bsz * seq, D_MODEL), BF16),
        scratch_shapes=[pltpu.VMEM((tb + CONV_HALO, qw), F32),
                        pltpu.VMEM((tb + CONV_HALO, qw), F32),
                        pltpu.VMEM((hg, ML_DK, ML_DV), F32),
                        pltpu.VMEM((hg, HALO, LANES), F32),
                        pltpu.VMEM((hg, HALO, LANES), F32)],
        compiler_params=_cparams(3),
        name="mlstm_mixer",
    )(proj, proj, proj, proj, proj, sel, conv_w, conv_b2, conv_w, conv_b2, gate_bias,
      _row2(norm_w), tri, _conv_shift_matrix(tb))


def _stack_heads(x):
    lane = lax.broadcasted_iota(jnp.int32, x.shape, 1)
    return jnp.concatenate([jnp.where(lane < RWKV_HEAD_DIM, x, 0.0),
                            jnp.where(lane >= RWKV_HEAD_DIM, x, 0.0)], axis=0)


def _shift_lerp(buf, src_ref, mu_ref, rows):
    buf[HALO:HALO + rows, :] = src_ref[...]
    cur = buf[HALO:HALO + rows, :]
    prev = buf[HALO - 1:HALO - 1 + rows, :]
    buf[0:HALO, :] = buf[rows:rows + HALO, :]
    return cur + (prev - cur) * mu_ref[...]


RWKV_GROUP = 8
RWKV_PASSES = dict(scores=1, solve=1, values=1, state=1)
RWKV_TIME_BLOCK = 256
HEAD_SUM_PASSES = 1


def _rwkv_body(r_ref, k_ref, v_ref, lo_ref, mur_ref, muk_ref, muv_ref, mulo_ref, w0_ref, w2_ref,
               a0_ref, a2_ref, g2_ref, kk_ref, ka_ref, rk_ref, lnw_ref, lnb_ref, tri_ref,
               ones_ref, o_ref, rbuf, kbuf, vbuf, lobuf, z_ref, y_buf):
    rows = r_ref.shape[0]
    n_pairs = r_ref.shape[1] // LANES
    ch = RWKV_CHUNK
    hd = RWKV_HEAD_DIM

    @pl.when(pl.program_id(2) == 0)
    def _():
        for buf in (rbuf, kbuf, vbuf, lobuf):
            buf[0:HALO, :] = jnp.zeros((HALO, buf.shape[1]), F32)
        z_ref[...] = jnp.zeros(z_ref.shape, F32)

    r = _shift_lerp(rbuf, r_ref, mur_ref, rows)
    k = _shift_lerp(kbuf, k_ref, muk_ref, rows)
    v = _shift_lerp(vbuf, v_ref, muv_ref, rows)
    lo = _shift_lerp(lobuf, lo_ref, mulo_ref, rows)
    w_lo, a_lo, g_lo = lo[:, 0:LANES], lo[:, LANES:2 * LANES], lo[:, 2 * LANES:]

    log_w = -jax.nn.softplus(-(w0_ref[...] + _mm(jnp.tanh(w_lo), w2_ref[...], 3))) - 0.5
    lw = -jnp.exp(log_w)
    a = jax.nn.sigmoid(a0_ref[...] + _mm(a_lo, a2_ref[...], 3))
    gate = _mm(jax.nn.sigmoid(g_lo), g2_ref[...], 1)

    head_sum = ones_ref[...]

    def head_total(x):
        return jnp.concatenate([_dotx_l(x[:, p * LANES:(p + 1) * LANES], head_sum, HEAD_SUM_PASSES)
                                for p in range(n_pairs)], axis=1)

    kk = k * kk_ref[...]
    kk = kk * lax.rsqrt(jnp.maximum(head_total(kk * kk), 1e-24))
    kh = k * (1.0 + (a - 1.0) * ka_ref[...])
    kb = kk * a
    cum = _dotx_r(tri_ref[...], lw, 3)

    c_row = lax.broadcasted_iota(jnp.int32, (ch, LANES), 0)
    c_col = lax.broadcasted_iota(jnp.int32, (ch, LANES), 1) % hd
    strict = c_col < c_row
    incl = c_col <= c_row
    same_sub = (c_col // RWKV_SUB) == (c_row // RWKV_SUB)
    eye_pair = (c_col == c_row).astype(F32)
    s_row = lax.broadcasted_iota(jnp.int32, (LANES, LANES), 0)
    s_col = lax.broadcasted_iota(jnp.int32, (LANES, LANES), 1)
    same_head = (s_row // hd) == (s_col // hd)
    eye_full = (s_row == s_col).astype(F32)

    n_chunks = rows // ch
    cat = jnp.concatenate
    mm_scores = functools.partial(_mm, passes=RWKV_PASSES["scores"])
    mm_solve = functools.partial(_mm, passes=RWKV_PASSES["solve"])
    mm_values = functools.partial(_mm, passes=RWKV_PASSES["values"])
    mm_state = functools.partial(_mm, passes=RWKV_PASSES["state"])
    a_hat, r_hat, k_til, b_til, k_end, b_end, vp, decay_end = [], [], [], [], [], [], [], []
    for j in range(n_chunks):
        sl = slice(j * ch, (j + 1) * ch)
        cm, lwj, kkj, khj, kbj, rj, vj = cum[sl], lw[sl], kk[sl], kh[sl], kb[sl], r[sl], v[sl]
        cm_end = cm[ch - 1:ch, :]
        e_neg = jnp.exp(-cm)
        e_end = jnp.exp(cm_end - cm)
        wide = (-kkj * jnp.exp(cm - lwj), rj * jnp.exp(cm), khj * e_neg, kbj * e_neg,
                khj * e_end, kbj * e_end, vj, jnp.exp(cm_end))
        for pi in range(n_pairs):
            ls = slice(pi * LANES, (pi + 1) * LANES)
            for dst, src in zip((a_hat, r_hat, k_til, b_til, k_end, b_end, vp, decay_end), wide):
                dst.append(src[:, ls])
    units = range(n_chunks * n_pairs)
    each = lambda fn: [fn(u) for u in units]

    big = each(lambda u: mm_scores(cat([a_hat[u], r_hat[u]], axis=0),
                                   cat([_stack_heads(k_til[u]), _stack_heads(b_til[u])], axis=0),
                                   dot=_dot_nt))
    l_ak = each(lambda u: jnp.where(strict, big[u][0:ch, 0:LANES], 0.0))
    l_ab = each(lambda u: jnp.where(strict, big[u][0:ch, LANES:], 0.0))
    m_rk = each(lambda u: jnp.where(incl, big[u][ch:, 0:LANES], 0.0))
    m_rb = each(lambda u: jnp.where(incl, big[u][ch:, LANES:], 0.0))

    l_d = each(lambda u: jnp.where(same_sub, l_ab[u], 0.0))
    l_o = each(lambda u: l_ab[u] - l_d[u])
    sub_levels = RWKV_SUB.bit_length() - 1
    blk_levels = (ch // RWKV_SUB).bit_length() - 1
    x = each(lambda u: eye_pair + l_d[u])
    p = each(lambda u: mm_solve(l_d[u], _stack_heads(l_d[u])))
    for _ in range(sub_levels - 2):
        both = each(lambda u: mm_solve(cat([p[u], x[u]], axis=0), _stack_heads(p[u])))
        x = each(lambda u: x[u] + both[u][ch:])
        p = each(lambda u: both[u][:ch])
    t_d = each(lambda u: x[u] + mm_solve(x[u], _stack_heads(p[u])))
    nn = each(lambda u: mm_solve(t_d[u], _stack_heads(l_o[u])))
    mx = each(lambda u: eye_pair + nn[u])
    pw = nn
    for _ in range(blk_levels - 1):
        pw = each(lambda u: mm_solve(pw[u], _stack_heads(pw[u])))
        mx = each(lambda u: mx[u] + mm_solve(mx[u], _stack_heads(pw[u])))
    t = each(lambda u: mm_solve(mx[u], _stack_heads(t_d[u])))

    lmv = each(lambda u: mm_values(cat([l_ak[u], m_rk[u]], axis=0), _stack_heads(vp[u])))
    wu = each(lambda u: mm_values(
        t[u], cat([_stack_heads(a_hat[u]), _stack_heads(lmv[u][:ch])], axis=1)))
    w_a = each(lambda u: wu[u][:, 0:LANES])
    u_v = each(lambda u: wu[u][:, LANES:])
    mw = each(lambda u: mm_values(
        m_rb[u], cat([_stack_heads(w_a[u]), _stack_heads(u_v[u])], axis=1)))
    q_eff = each(lambda u: r_hat[u] + mw[u][:, 0:LANES])
    y_loc = each(lambda u: lmv[u][ch:] + mw[u][:, LANES:])
    a_eff = each(lambda u: jnp.where(same_head, mm_state(b_end[u], w_a[u], dot=_dot_tn), 0.0)
                 + eye_full * decay_end[u])
    z_loc = each(lambda u: jnp.where(
        same_head, mm_state(cat([k_end[u], b_end[u]], axis=0), cat([vp[u], u_v[u]], axis=0),
                            dot=_dot_tn), 0.0))

    z = [z_ref[pi] for pi in range(n_pairs)]
    for j in range(n_chunks):
        sl = slice(j * ch, (j + 1) * ch)
        us = [j * n_pairs + pi for pi in range(n_pairs)]
        qz = [mm_state(cat([q_eff[u], a_eff[u]], axis=0), z[pi]) for pi, u in enumerate(us)]
        for pi, u in enumerate(us):
            y_buf[sl, pi * LANES:(pi + 1) * LANES] = qz[pi][:ch] + y_loc[u]
        z = [qz[pi][ch:] + z_loc[u] for pi, u in enumerate(us)]
    for pi in range(n_pairs):
        z_ref[pi] = z[pi]

    y = y_buf[...]
    mean = head_total(y) * (1.0 / hd)
    yc = y - mean
    var = head_total(yc * yc) * (1.0 / hd)
    yn = yc * lax.rsqrt(var + RWKV_LN_EPS) * lnw_ref[...] + lnb_ref[...]
    bonus = head_total(r * kh * rk_ref[...]) * v
    o_ref[...] = ((yn + bonus) * gate).astype(o_ref.dtype)


def _rwkv_mixer(proj, bsz, seq, mu, w0, w2, a0, a2, g2, k_k, k_a, r_k, ln_w, ln_b):
    tb = RWKV_TIME_BLOCK
    nc = seq // tb
    d = D_MODEL
    lora_w = 2 * LANES + RWKV_LORA_G
    mur, muk, muv = _row2(mu[0:d]), _row2(mu[d:2 * d]), _row2(mu[2 * d:3 * d])
    o = 3 * d
    pad_l = lambda x, n: jnp.pad(x, (0, n - x.shape[0]))
    mulo = _row2(jnp.concatenate([pad_l(mu[o:o + RWKV_LORA_W], LANES),
                                  pad_l(mu[o + RWKV_LORA_W:o + RWKV_LORA_W + RWKV_LORA_A], LANES),
                                  mu[o + RWKV_LORA_W + RWKV_LORA_A:]]))
    w2p = jnp.pad(w2, ((0, LANES - RWKV_LORA_W), (0, 0)))
    a2p = jnp.pad(a2, ((0, LANES - RWKV_LORA_A), (0, 0)))
    ch = RWKV_CHUNK
    t_i = jnp.arange(tb)[:, None]
    t_j = jnp.arange(tb)[None, :]
    tri = ((t_i >= t_j) & (t_i // ch == t_j // ch)).astype(BF16)
    l_i = jnp.arange(LANES)[:, None]
    l_j = jnp.arange(LANES)[None, :]
    head_sum = (l_i // RWKV_HEAD_DIM == l_j // RWKV_HEAD_DIM).astype(BF16)

    gw = RWKV_GROUP * LANES
    row = lambda b, p, c: b * nc + c
    col = lambda base: (lambda b, p, c: (row(b, p, c), base // gw + p))
    vec = pl.BlockSpec((1, gw), lambda b, p, c: (0, p))
    in_specs = [
        pl.BlockSpec((tb, gw), col(C_R)),
        pl.BlockSpec((tb, gw), col(C_K)),
        pl.BlockSpec((tb, gw), col(C_V)),
        pl.BlockSpec((tb, lora_w), lambda b, p, c: (row(b, p, c), C_LORA // lora_w)),
        vec, vec, vec,
        pl.BlockSpec((1, lora_w), lambda b, p, c: (0, 0)),
        vec,
        pl.BlockSpec((LANES, gw), lambda b, p, c: (0, p)),
        vec,
        pl.BlockSpec((LANES, gw), lambda b, p, c: (0, p)),
        pl.BlockSpec((RWKV_LORA_G, gw), lambda b, p, c: (0, p)),
        vec, vec, vec, vec, vec,
        pl.BlockSpec((tb, tb), lambda b, p, c: (0, 0)),
        pl.BlockSpec((LANES, LANES), lambda b, p, c: (0, 0)),
    ]
    return pl.pallas_call(
        _rwkv_body,
        grid=(bsz, RWKV_PAIRS // RWKV_GROUP, nc),
        in_specs=in_specs,
        out_specs=pl.BlockSpec((tb, gw), lambda b, p, c: (row(b, p, c), p)),
        out_shape=jax.ShapeDtypeStruct((bsz * seq, D_MODEL), BF16),
        scratch_shapes=[pltpu.VMEM((tb + HALO, gw), F32),
                        pltpu.VMEM((tb + HALO, gw), F32),
                        pltpu.VMEM((tb + HALO, gw), F32),
                        pltpu.VMEM((tb + HALO, lora_w), F32),
                        pltpu.VMEM((RWKV_GROUP, LANES, LANES), F32),
                        pltpu.VMEM((tb, gw), F32)],
        compiler_params=_cparams(3),
        name="rwkv_mixer",
    )(proj, proj, proj, proj, mur, muk, muv, mulo, _row2(w0), w2p, _row2(a0), a2p, g2,
      _row2(k_k), _row2(k_a), _row2(r_k.reshape(-1)), _row2(ln_w), _row2(ln_b), tri, head_sum)


def _gated_merge_body(ys_ref, yr_ref, ym_ref, ps_ref, pr_ref, pm_ref, gs_ref, gr_ref, gm_ref, o_ref):
    acc = jax.nn.sigmoid(gs_ref[...]) * _dot(ys_ref[...], ps_ref[...])
    acc = acc + jax.nn.sigmoid(gr_ref[...]) * _dot(yr_ref[...], pr_ref[...])
    acc = acc + jax.nn.sigmoid(gm_ref[...]) * _dot(ym_ref[...], pm_ref[...])
    o_ref[...] = acc.astype(o_ref.dtype)


def _gated_merge(ys, ps, proj, tm=TM_GATED, tn=TN_GATED):
    m, k = ys[0].shape
    tm = min(tm, m)
    n = ps[0].shape[1]
    y_spec = pl.BlockSpec((tm, k), lambda j, i: (i, 0))
    p_spec = pl.BlockSpec((k, tn), lambda j, i: (0, j), pipeline_mode=pl.Buffered(1))
    gate_spec = lambda col: pl.BlockSpec((tm, tn), lambda j, i: (i, col // tn + j))
    return pl.pallas_call(
        _gated_merge_body,
        grid=(n // tn, m // tm),
        in_specs=[y_spec] * 3 + [p_spec] * 3 + [gate_spec(C_GS), gate_spec(C_GR), gate_spec(C_GM)],
        out_specs=pl.BlockSpec((tm, tn), lambda j, i: (i, j)),
        out_shape=jax.ShapeDtypeStruct((m, n), BF16),
        compiler_params=_cparams(2),
        name="gated_merge",
    )(*ys, *ps, proj, proj, proj)


def _mm_res_ln_body(a_ref, w_ref, h_ref, g_ref, b_ref, o_ref, ob_ref, *, row_split):
    n = a_ref.shape[0] // row_split
    subs = [slice(s * n, (s + 1) * n) for s in range(row_split)]
    acc = [_dot(a_ref[rows, :], w_ref[...]) for rows in subs]
    for rows, mix in zip(subs, acc):
        h = _layer_norm_rows(ALPHA * h_ref[rows, :] + mix, g_ref[...], b_ref[...])
        o_ref[rows, :] = h
        ob_ref[rows, :] = h.astype(BF16)


def _matmul_res_ln(a, w, h, g, b, tm, row_split):
    m, k = a.shape
    n = w.shape[1]
    const = lambda shape: pl.BlockSpec(shape, lambda i: (0, 0), pipeline_mode=pl.Buffered(1))
    return pl.pallas_call(
        functools.partial(_mm_res_ln_body, row_split=row_split),
        grid=(m // tm,),
        in_specs=[pl.BlockSpec((tm, k), lambda i: (i, 0)),
                  const((k, n)),
                  pl.BlockSpec((tm, n), lambda i: (i, 0)),
                  const((1, n)), const((1, n))],
        out_specs=[pl.BlockSpec((tm, n), lambda i: (i, 0)),
                   pl.BlockSpec((tm, n), lambda i: (i, 0))],
        out_shape=[jax.ShapeDtypeStruct((m, n), F32), jax.ShapeDtypeStruct((m, n), BF16)],
        compiler_params=_cparams(1),
        name="matmul_res_ln",
    )(a, w, h, _row2(g), _row2(b))


FFN_HALO = 16


def _ffn_up_body(h_ref, hp_ref, wg_ref, wua_ref, wub_ref, cwg_ref, cbg_ref, cwu_ref, cbu_ref, o_ref,
                 gbuf, ubuf, wg_s, wu_s, *, blocks_per_seq):
    rows = h_ref.shape[0]
    tn = o_ref.shape[1]
    first = (pl.program_id(1) % blocks_per_seq) == 0
    keep = jnp.where(first, 0.0, 1.0)

    @pl.when(pl.program_id(1) == 0)
    def _():
        col = pl.program_id(0) * tn + lax.broadcasted_iota(jnp.int32, wg_s.shape, 1)
        off = FFN_D % tn
        up = jnp.concatenate([wua_ref[:, off:], wub_ref[:, :off]], axis=1)
        wg_s[...] = jnp.where(col < FFN_D, wg_ref[...], 0.0).astype(BF16)
        wu_s[...] = jnp.where(col < FFN_D, up, 0.0).astype(BF16)

    def conv(buf, cw_ref, cb_ref):
        acc = cb_ref[...] + cw_ref[2:3, :] * buf[FFN_HALO:FFN_HALO + rows, :]
        acc = acc + cw_ref[1:2, :] * buf[FFN_HALO - 1:FFN_HALO - 1 + rows, :]
        return acc + cw_ref[0:1, :] * buf[FFN_HALO - 2:FFN_HALO - 2 + rows, :]

    for buf, w_ref in ((gbuf, wg_s), (ubuf, wu_s)):
        buf[0:FFN_HALO, :] = _dot(hp_ref[...], w_ref[...]) * keep
        buf[FFN_HALO:FFN_HALO + rows, :] = _dot(h_ref[...], w_ref[...])
    gate = conv(gbuf, cwg_ref, cbg_ref)
    up = conv(ubuf, cwu_ref, cbu_ref)
    gelu = 0.5 * gate * (1.0 + lax.erf(gate * (2.0 ** -0.5)))
    o_ref[...] = (gelu * up).astype(o_ref.dtype)


def _ffn_up(hb, w_up, layer, cwg, cbg, cwu, cbu, seq, tm):
    m, k = hb.shape
    n = FFN_PAD
    tn = FFN_TN
    halo_blocks = tm // FFN_HALO
    up0 = FFN_D // tn
    body = functools.partial(_ffn_up_body, blocks_per_seq=seq // tm)
    w_spec = lambda first_block: pl.BlockSpec((None, k, tn), lambda j, i: (layer, 0, first_block + j))
    return pl.pallas_call(
        body,
        grid=(n // tn, m // tm),
        in_specs=[pl.BlockSpec((tm, k), lambda j, i: (i, 0)),
                  pl.BlockSpec((FFN_HALO, k), lambda j, i: (jnp.maximum(i * halo_blocks - 1, 0), 0)),
                  w_spec(0), w_spec(up0), w_spec(up0 + 1),
                  pl.BlockSpec((3, tn), lambda j, i: (0, j)),
                  pl.BlockSpec((1, tn), lambda j, i: (0, j)),
                  pl.BlockSpec((3, tn), lambda j, i: (0, j)),
                  pl.BlockSpec((1, tn), lambda j, i: (0, j))],
        out_specs=pl.BlockSpec((tm, tn), lambda j, i: (i, j)),
        out_shape=jax.ShapeDtypeStruct((m, n), BF16),
        scratch_shapes=[pltpu.VMEM((tm + FFN_HALO, tn), F32),
                        pltpu.VMEM((tm + FFN_HALO, tn), F32),
                        pltpu.VMEM((k, tn), BF16),
                        pltpu.VMEM((k, tn), BF16)],
        compiler_params=_cparams(2),
        name="ffn_up",
    )(hb, hb, w_up, w_up, w_up, cwg, cbg, cwu, cbu)


def _pack_segments():
    d = D_MODEL
    order = (("z", d), ("x", d), ("bc", 2 * SSD_GN), ("dt", SSD_HEADS), ("r", d), ("k", d), ("v", d),
             ("w_lo", RWKV_LORA_W), ("a_lo", RWKV_LORA_A), ("g_lo", RWKV_LORA_G), ("qk", d),
             ("mv", d), ("mo", d), ("gi", ML_HEADS), ("gf", ML_HEADS), ("gs", d), ("gr", d), ("gm", d))
    dst = dict(z=C_Z, x=C_X, r=C_R, k=C_K, v=C_V, qk=C_QK, mv=C_MV, mo=C_MO, gs=C_GS, gr=C_GR,
               gm=C_GM, bc=C_BC, w_lo=C_LORA, a_lo=C_LORA + LANES, g_lo=C_LORA + 2 * LANES,
               dt=C_SMALL, gi=C_SMALL + SSD_HEADS, gf=C_SMALL + SSD_HEADS + ML_HEADS)
    segments, src = [], 0
    for name, width in order:
        segments.append((src, width, dst[name]))
        src += width
    return tuple(segments), src


def _pack_body(wt_ref, o_ref, *, segments):
    group, start, pos = [], 0, 0

    def flush():
        nonlocal group, start
        rows = group[0] if len(group) == 1 else jnp.concatenate(group, axis=0)
        o_ref[:, start:pos] = rows.T.astype(o_ref.dtype)
        group, start = [], pos

    for src, width, dst in sorted(segments, key=lambda s: s[2]) + [(None, 0, N_PROJ)]:
        if dst > pos:
            group.append(jnp.zeros((dst - pos, LANES), F32))
            pos = dst
            if pos % LANES == 0:
                flush()
        if width:
            group.append(wt_ref[src:src + width, :])
            pos += width
            if pos % LANES == 0:
                flush()


def _pack_w_in(w_in, layer):
    segments, d_in = _pack_segments()
    d = w_in.shape[1]
    return pl.pallas_call(
        functools.partial(_pack_body, segments=segments),
        grid=(d // LANES,),
        in_specs=[pl.BlockSpec((None, d_in, LANES), lambda i: (layer, 0, i))],
        out_specs=pl.BlockSpec((LANES, N_PROJ), lambda i: (i, 0)),
        out_shape=jax.ShapeDtypeStruct((d, N_PROJ), BF16),
        compiler_params=_cparams(1),
        name="pack_w_in",
    )(jnp.swapaxes(w_in, 1, 2))


def _to_bf16_body(w_ref, o_ref, *, valid_rows):
    x = w_ref[...]
    if valid_rows is not None:
        row = pl.program_id(0) * o_ref.shape[0] + lax.broadcasted_iota(jnp.int32, o_ref.shape, 0)
        x = jnp.where(row < valid_rows, x, 0.0)
    o_ref[...] = x.astype(o_ref.dtype)


def _to_bf16(w_stack, layer, out_rows=None, rows_per_step=512):
    _, rows, cols = w_stack.shape
    out_rows = rows if out_rows is None else out_rows
    tr = rows_per_step
    last = (rows - 1) // tr
    return pl.pallas_call(
        functools.partial(_to_bf16_body, valid_rows=None if out_rows == rows else rows),
        grid=(out_rows // tr,),
        in_specs=[pl.BlockSpec((None, tr, cols), lambda i: (layer, jnp.minimum(i, last), 0))],
        out_specs=pl.BlockSpec((tr, cols), lambda i: (i, 0)),
        out_shape=jax.ShapeDtypeStruct((out_rows, cols), BF16),
        compiler_params=_cparams(1),
        name="to_bf16",
    )(w_stack)


def _pad_cols(x, n):
    return jnp.pad(x, ((0, 0), (0, n - x.shape[1])))


def kernel(x, ln_in_g, ln_in_b, w_in, ssd_conv_w, ssd_conv_b, ssd_dt_bias, ssd_a_log, ssd_d, ssd_norm_w, rwkv_mu, rwkv_w0, rwkv_w2, rwkv_a0, rwkv_a2, rwkv_g2, rwkv_k_k, rwkv_k_a, rwkv_r_k, rwkv_ln_w, rwkv_ln_b, mlstm_conv_w, mlstm_conv_b, mlstm_i_bias, mlstm_f_bias, mlstm_norm_w, proj_ssd, proj_rwkv, proj_mlstm, w_out, ln1_g, ln1_b, ffn_w_up, ffn_conv_w, ffn_conv_b, ffn_w_down, ln2_g, ln2_b):
    bsz, seq, d = x.shape
    n_tok = bsz * seq
    tb = TIME_BLOCK
    t_i = jnp.arange(tb)[:, None]
    t_j = jnp.arange(tb)[None, :]
    tri = (t_i >= t_j).astype(BF16)

    h, hb = _ln_in(x.reshape(n_tok, d), ln_in_g, ln_in_b)
    for l in range(DEPTH):
        proj = _matmul(hb, _pack_w_in(w_in, l), tm=min(TM_IN_PROJ, n_tok), tn=TN_IN_PROJ,
                       out_dtype=F32, name="in_proj")
        y_ssd = _ssd_mixer(proj, bsz, seq, ssd_conv_w[l], ssd_conv_b[l], ssd_dt_bias[l],
                           ssd_a_log[l], ssd_d[l], ssd_norm_w[l], tri)
        y_rwkv = _rwkv_mixer(proj, bsz, seq, rwkv_mu[l], rwkv_w0[l], rwkv_w2[l], rwkv_a0[l],
                             rwkv_a2[l], rwkv_g2[l], rwkv_k_k[l], rwkv_k_a[l], rwkv_r_k[l],
                             rwkv_ln_w[l], rwkv_ln_b[l])
        y_mlstm = _mlstm_mixer(proj, bsz, seq, mlstm_conv_w[l], mlstm_conv_b[l], mlstm_i_bias[l],
                               mlstm_f_bias[l], mlstm_norm_w[l], tri)
        merged = _gated_merge(
            (y_ssd, y_rwkv, y_mlstm),
            (_to_bf16(proj_ssd, l), _to_bf16(proj_rwkv, l), _to_bf16(proj_mlstm, l)), proj)
        h, hb = _matmul_res_ln(merged, _to_bf16(w_out, l), h, ln1_g[l], ln1_b[l],
                               tm=min(TM_W_OUT, n_tok), row_split=4)

        cwg = _pad_cols(ffn_conv_w[l][:, :FFN_D], FFN_PAD)
        cwu = _pad_cols(ffn_conv_w[l][:, FFN_D:], FFN_PAD)
        cbg = _pad_cols(_row2(ffn_conv_b[l][:FFN_D]), FFN_PAD)
        cbu = _pad_cols(_row2(ffn_conv_b[l][FFN_D:]), FFN_PAD)
        act = _ffn_up(hb, ffn_w_up, l, cwg, cbg, cwu, cbu, seq, tm=min(TM_FFN_UP, seq))
        wd = _to_bf16(ffn_w_down, l, out_rows=FFN_PAD)
        h, hb = _matmul_res_ln(act, wd, h, ln2_g[l], ln2_b[l], tm=min(TM_FFN_DOWN, n_tok),
                               row_split=1)
    return h.reshape(bsz, seq, d)
```

```python
import functools

import jax
import jax.numpy as jnp
from jax import lax
from jax.experimental import pallas as pl
from jax.experimental.pallas import tpu as pltpu

F32 = jnp.float32
BF16 = jnp.bfloat16

D_MODEL = 2048
DEPTH = 2
ALPHA = (2.0 * DEPTH) ** 0.25
LN_EPS = 1e-5

SSD_HEADS = 32
SSD_HEAD_DIM = 64
SSD_GROUPS = 4
SSD_STATE = 128
SSD_GN = SSD_GROUPS * SSD_STATE
SSD_EPS = 1e-5
SSD_GROUP_W = D_MODEL // SSD_GROUPS
SSD_HPG = SSD_HEADS // SSD_GROUPS

RWKV_HEAD_DIM = 64
RWKV_PAIRS = D_MODEL // 128
RWKV_LORA_W = 96
RWKV_LORA_A = 96
RWKV_LORA_G = 256
RWKV_LN_EPS = 64e-5
RWKV_CHUNK = 64
RWKV_SUB = 8

ML_HEADS = 8
ML_DK = 128
ML_DV = 256
ML_EPS = 1e-6
ML_GROUP = 8

FFN_D = 5504
FFN_PAD = 5632
FFN_TN = 512

LANES = 128
HALO = 8
TIME_BLOCK = 128

C_Z = 0
C_X = 2048
C_R = 4096
C_K = 6144
C_V = 8192
C_QK = 10240
C_MV = 12288
C_MO = 14336
C_GS = 16384
C_GR = 18432
C_GM = 20480
C_BC = 22528
C_LORA = 23552
C_SMALL = 24064
N_PROJ = 24576

VMEM_LIMIT = 56 * 1024 * 1024

TM_LN_IN = 512
TM_IN_PROJ, TN_IN_PROJ = 1024, 2048
TM_GATED, TN_GATED = 512, 1024
TM_W_OUT = 512
TM_FFN_UP = 1024
TM_FFN_DOWN = 256


def _dot(a, b):
    return jnp.dot(a, b, preferred_element_type=F32)


def _dot_nt(a, b):
    return lax.dot_general(a, b, (((1,), (1,)), ((), ())), preferred_element_type=F32)


def _dot_tn(a, b):
    return lax.dot_general(a, b, (((0,), (0,)), ((), ())), preferred_element_type=F32)


def _split(x, n):
    parts, r = [], x
    for i in range(n):
        p = r.astype(BF16)
        parts.append(p)
        if i + 1 < n:
            r = r - p.astype(F32)
    return parts


def _dotx_l(x, w_exact, n, dot=_dot):
    acc = None
    for p in _split(x, n):
        t = dot(p, w_exact)
        acc = t if acc is None else acc + t
    return acc


def _dotx_r(w_exact, x, n, dot=_dot):
    acc = None
    for p in _split(x, n):
        t = dot(w_exact, p)
        acc = t if acc is None else acc + t
    return acc


def _mm(a, b, passes, dot=_dot):
    if passes == 1:
        return dot(a.astype(BF16), b.astype(BF16))
    ah, al = _split(a, 2)
    bh, bl = _split(b, 2)
    return dot(ah, bh) + (dot(ah, bl) + dot(al, bh))


def _layer_norm_rows(x, g, b):
    mu = jnp.mean(x, axis=-1, keepdims=True)
    xc = x - mu
    var = jnp.mean(xc * xc, axis=-1, keepdims=True)
    return xc * lax.rsqrt(var + LN_EPS) * g + b


def _cparams(n_axes):
    return pltpu.CompilerParams(dimension_semantics=("arbitrary",) * n_axes,
                                vmem_limit_bytes=VMEM_LIMIT)


def _row2(v):
    return v.reshape(1, -1)


def _ln_in_body(x_ref, g_ref, b_ref, h_ref, hb_ref):
    h = _layer_norm_rows(x_ref[...], g_ref[...], b_ref[...])
    h_ref[...] = h
    hb_ref[...] = h.astype(BF16)


def _ln_in(x2d, g, b, tm=TM_LN_IN):
    n, d = x2d.shape
    tm = min(tm, n)
    return pl.pallas_call(
        _ln_in_body,
        grid=(n // tm,),
        in_specs=[pl.BlockSpec((tm, d), lambda i: (i, 0)),
                  pl.BlockSpec((1, d), lambda i: (0, 0)),
                  pl.BlockSpec((1, d), lambda i: (0, 0))],
        out_specs=[pl.BlockSpec((tm, d), lambda i: (i, 0)),
                   pl.BlockSpec((tm, d), lambda i: (i, 0))],
        out_shape=[jax.ShapeDtypeStruct((n, d), F32), jax.ShapeDtypeStruct((n, d), BF16)],
        compiler_params=_cparams(1),
        name="ln_in",
    )(x2d, _row2(g), _row2(b))


def _mm_body(a_ref, w_ref, o_ref):
    o_ref[...] = _dot(a_ref[...], w_ref[...]).astype(o_ref.dtype)


def _matmul(a, w, tm, tn, out_dtype, name):
    m, k = a.shape
    _, n = w.shape
    return pl.pallas_call(
        _mm_body,
        grid=(n // tn, m // tm),
        in_specs=[pl.BlockSpec((tm, k), lambda j, i: (i, 0)),
                  pl.BlockSpec((k, tn), lambda j, i: (0, j))],
        out_specs=pl.BlockSpec((tm, tn), lambda j, i: (i, j)),
        out_shape=jax.ShapeDtypeStruct((m, n), out_dtype),
        compiler_params=_cparams(2),
        name=name,
    )(a, w)


CONV_HALO = 16


def _conv_shift_matrix(rows):
    t = jnp.arange(rows)[:, None]
    c = jnp.arange(3 * (rows + CONV_HALO))[None, :]
    k, j = c // (rows + CONV_HALO), c % (rows + CONV_HALO)
    return (j == CONV_HALO + t - 3 + k).astype(BF16)


def _conv4_silu(buf, src_ref, w_ref, b_ref, shift_ref, rows):
    x = src_ref[...]
    buf[CONV_HALO:CONV_HALO + rows, :] = x
    hist = buf[...]
    taps = jnp.concatenate([(hist * w_ref[k:k + 1, :]).astype(BF16) for k in range(3)], axis=0)
    acc = b_ref[...] + w_ref[3:4, :] * x + _dot(shift_ref[...], taps)
    buf[0:CONV_HALO, :] = buf[rows:rows + CONV_HALO, :]
    return jax.nn.silu(acc)


def _ssd_body(z_ref, x_ref, bc_ref, sm_ref, cwx_ref, cbx_ref, cwbc_ref, cbbc_ref, dtb_ref,
              alog_ref, dsk_ref, nw_ref, e_ref, tri_ref, shift_ref, o_ref, xbuf, bcbuf, st_ref):
    rows = x_ref.shape[0]
    gw = SSD_GROUP_W
    groups = range(SSD_GROUPS)

    @pl.when(pl.program_id(1) == 0)
    def _():
        xbuf[0:CONV_HALO, :] = jnp.zeros((CONV_HALO, xbuf.shape[1]), F32)
        bcbuf[0:CONV_HALO, :] = jnp.zeros((CONV_HALO, bcbuf.shape[1]), F32)
        st_ref[...] = jnp.zeros(st_ref.shape, F32)

    xs_all = _conv4_silu(xbuf, x_ref, cwx_ref, cbx_ref, shift_ref, rows)
    bc = _conv4_silu(bcbuf, bc_ref, cwbc_ref, cbbc_ref, shift_ref, rows)

    dt = jax.nn.softplus(sm_ref[...] + dtb_ref[...])
    adt = dt * (-jnp.exp(alog_ref[...]))
    a_cs = _dotx_r(tri_ref[...], adt, 3)
    a_cs_t = a_cs.T

    t_row = lax.broadcasted_iota(jnp.int32, (rows, rows), 0)
    t_col = lax.broadcasted_iota(jnp.int32, (rows, rows), 1)
    causal = t_row >= t_col
    lane = lax.broadcasted_iota(jnp.int32, (rows, LANES), 1)

    xs = [xs_all[:, g * gw:(g + 1) * gw] for g in groups]
    bm = [bc[:, g * SSD_STATE:(g + 1) * SSD_STATE].astype(BF16) for g in groups]
    cm = [bc[:, SSD_GN + g * SSD_STATE:SSD_GN + (g + 1) * SSD_STATE].astype(BF16) for g in groups]
    a_exp = [_dotx_l(a_cs, e_ref[g], 3) for g in groups]
    dt_exp = [_dotx_l(dt, e_ref[g], 3) for g in groups]
    scores = [_dot_nt(cm[g], bm[g]) for g in groups]
    st = [st_ref[g] for g in groups]
    y_off = [_dot(cm[g], st[g].astype(BF16)) for g in groups]
    xdt = [xs[g] * dt_exp[g] for g in groups]

    y_diag = []
    for g in groups:
        parts = []
        for pr in range(SSD_HPG // 2):
            xp = xdt[g][:, pr * LANES:(pr + 1) * LANES]
            acc = None
            for s in range(2):
                hh = g * SSD_HPG + 2 * pr + s
                seg = a_cs[:, hh:hh + 1] - a_cs_t[hh:hh + 1, :]
                m = jnp.where(causal, jnp.exp(seg), 0.0) * scores[g]
                head_lanes = (lane < SSD_HEAD_DIM) if s == 0 else (lane >= SSD_HEAD_DIM)
                xm = jnp.where(head_lanes, xp, 0.0)
                t = _dot(m.astype(BF16), xm.astype(BF16))
                acc = t if acc is None else acc + t
            parts.append(acc)
        y_diag.append(jnp.concatenate(parts, axis=1))

    for g in groups:
        cols = slice(g * gw, (g + 1) * gw)
        a_last = a_exp[g][rows - 1:rows, :]
        st_ref[g] = jnp.exp(a_last) * st[g] + _dot_tn(
            bm[g], (xdt[g] * jnp.exp(a_last - a_exp[g])).astype(BF16))
        y = y_diag[g] + y_off[g] * jnp.exp(a_exp[g]) + xs[g] * dsk_ref[:, cols]
        y = y * jax.nn.silu(z_ref[:, cols])
        ms = jnp.mean(y * y, axis=-1, keepdims=True)
        o_ref[:, cols] = (y * lax.rsqrt(ms + SSD_EPS) * nw_ref[:, cols]).astype(o_ref.dtype)


def _ssd_mixer(proj, bsz, seq, conv_w, conv_b, dt_bias, a_log, d_skip, norm_w, tri):
    tb = TIME_BLOCK
    nc = seq // tb
    gw = SSD_GROUP_W
    d = D_MODEL
    bcw = 2 * SSD_GN
    cwx, cwbc = conv_w[:, :d], conv_w[:, d:]
    cbx, cbbc = _row2(conv_b[:d]), _row2(conv_b[d:])
    head_lanes = lambda v: _row2(jnp.pad(v, (0, LANES - SSD_HEADS)))
    grp = jnp.arange(SSD_GROUPS)[:, None, None]
    src = jnp.arange(LANES)[None, :, None]
    dst = jnp.arange(gw)[None, None, :]
    expand = (src == grp * SSD_HPG + dst // SSD_HEAD_DIM).astype(BF16)

    row = lambda b, c: b * nc + c
    full = lambda shape: pl.BlockSpec(shape, lambda b, c: (0,) * len(shape))
    in_specs = [
        pl.BlockSpec((tb, d), lambda b, c: (row(b, c), C_Z // d)),
        pl.BlockSpec((tb, d), lambda b, c: (row(b, c), C_X // d)),
        pl.BlockSpec((tb, bcw), lambda b, c: (row(b, c), C_BC // bcw)),
        pl.BlockSpec((tb, LANES), lambda b, c: (row(b, c), C_SMALL // LANES)),
        full((4, d)), full((1, d)), full((4, bcw)), full((1, bcw)),
        full((1, LANES)), full((1, LANES)), full((1, d)), full((1, d)),
        full((SSD_GROUPS, LANES, gw)), full((tb, tb)), full((tb, 3 * (tb + CONV_HALO))),
    ]
    return pl.pallas_call(
        _ssd_body,
        grid=(bsz, nc),
        in_specs=in_specs,
        out_specs=pl.BlockSpec((tb, d), lambda b, c: (row(b, c), 0)),
        out_shape=jax.ShapeDtypeStruct((bsz * seq, d), BF16),
        scratch_shapes=[pltpu.VMEM((tb + CONV_HALO, d), F32),
                        pltpu.VMEM((tb + CONV_HALO, bcw), F32),
                        pltpu.VMEM((SSD_GROUPS, SSD_STATE, gw), F32)],
        compiler_params=_cparams(2),
        name="ssd_mixer",
    )(proj, proj, proj, proj, cwx, cbx, cwbc, cbbc, head_lanes(dt_bias), head_lanes(a_log),
      _row2(jnp.repeat(d_skip, SSD_HEAD_DIM)), _row2(norm_w), expand, tri, _conv_shift_matrix(tb))


def _mlstm_body(q_ref, k_ref, v_ref, og_ref, sm_ref, sel_ref, cwq_ref, cbq_ref, cwk_ref, cbk_ref,
                gb_ref, nw_ref, tri_ref, shift_ref, o_ref, qbuf, kbuf, ct_ref, n_ref, m_ref):
    rows = q_ref.shape[0]

    @pl.when(pl.program_id(2) == 0)
    def _():
        qbuf[0:CONV_HALO, :] = jnp.zeros((CONV_HALO, qbuf.shape[1]), F32)
        kbuf[0:CONV_HALO, :] = jnp.zeros((CONV_HALO, kbuf.shape[1]), F32)
        ct_ref[...] = jnp.zeros(ct_ref.shape, F32)
        n_ref[...] = jnp.zeros(n_ref.shape, F32)
        m_ref[...] = jnp.zeros(m_ref.shape, F32)

    heads = range(q_ref.shape[1] // ML_DK)
    q_all = _conv4_silu(qbuf, q_ref, cwq_ref, cbq_ref, shift_ref, rows) * (ML_DK ** -0.5)
    k_all = _conv4_silu(kbuf, k_ref, cwk_ref, cbk_ref, shift_ref, rows)
    qf = [q_all[:, i * ML_DK:(i + 1) * ML_DK] for i in heads]
    q = [qf[i].astype(BF16) for i in heads]
    k = [k_all[:, i * ML_DK:(i + 1) * ML_DK] for i in heads]
    kb = [k[i].astype(BF16) for i in heads]
    v = [v_ref[:, i * ML_DV:(i + 1) * ML_DV] for i in heads]

    pre = _dotx_l(sm_ref[...], sel_ref[...], 3) + gb_ref[...]
    log_f = jax.nn.log_sigmoid(pre)
    bcum_all = _dotx_r(tri_ref[...], log_f, 3)
    lane = lax.broadcasted_iota(jnp.int32, (rows, LANES), 1)
    gates_t = jnp.where(lane % 2 == 0, pre, bcum_all).T
    t_row = lax.broadcasted_iota(jnp.int32, (rows, rows), 0)
    t_col = lax.broadcasted_iota(jnp.int32, (rows, rows), 1)
    causal = t_row >= t_col

    li = [pre[:, 2 * i:2 * i + 1] for i in heads]
    bcum = [bcum_all[:, 2 * i + 1:2 * i + 2] for i in heads]
    log_d = [jnp.where(causal, bcum[i] - gates_t[2 * i + 1:2 * i + 2, :] + gates_t[2 * i:2 * i + 1, :],
                       -jnp.inf) for i in heads]
    qk = [_dot_nt(q[i], kb[i]) for i in heads]
    ct = [ct_ref[i] for i in heads]
    inter = [_dot(q[i], ct[i].astype(BF16)) for i in heads]
    m_prev = [m_ref[i, 0:1, 0:1] for i in heads]
    m_inter = [bcum[i] + m_prev[i] for i in heads]
    m_t = [jnp.maximum(jnp.max(log_d[i], axis=-1, keepdims=True), m_inter[i]) for i in heads]
    w = [jnp.exp(log_d[i] - m_t[i]) * qk[i] for i in heads]
    s_inter = [jnp.exp(m_inter[i] - m_t[i]) for i in heads]
    nvec = [n_ref[i, 0:1, :] for i in heads]
    num = [_dot(w[i].astype(BF16), v[i].astype(BF16)) + s_inter[i] * inter[i] for i in heads]
    den = [jnp.sum(w[i], axis=-1, keepdims=True)
           + s_inter[i] * jnp.sum(qf[i] * nvec[i], axis=-1, keepdims=True)
           for i in heads]
    h = [num[i] / jnp.maximum(jnp.abs(den[i]), jnp.exp(-m_t[i])) for i in heads]

    b_tot = [bcum[i][rows - 1:rows, :] for i in heads]
    log_w = [b_tot[i] - bcum[i] + li[i] for i in heads]
    m_new = [jnp.maximum(b_tot[i] + m_prev[i], jnp.max(log_w[i], axis=0, keepdims=True))
             for i in heads]
    w_end = [jnp.exp(log_w[i] - m_new[i]) for i in heads]
    s_old = [jnp.exp(b_tot[i] + m_prev[i] - m_new[i]) for i in heads]
    ct_new = [s_old[i] * ct[i] + _dot_tn(kb[i], (w_end[i] * v[i]).astype(BF16)) for i in heads]
    for i in heads:
        ct_ref[i] = ct_new[i]
        n_ref[i] = jnp.broadcast_to(
            s_old[i] * nvec[i] + jnp.sum(w_end[i] * k[i], axis=0, keepdims=True), n_ref.shape[1:])
        m_ref[i] = jnp.broadcast_to(m_new[i], m_ref.shape[1:])
        hn = h[i] * lax.rsqrt(jnp.mean(h[i] * h[i], axis=-1, keepdims=True) + ML_EPS)
        cols = slice(i * ML_DV, (i + 1) * ML_DV)
        o_ref[:, cols] = (jax.nn.sigmoid(og_ref[:, cols]) * hn * nw_ref[:, cols]).astype(o_ref.dtype)


def _mlstm_mixer(proj, bsz, seq, conv_w, conv_b, i_bias, f_bias, norm_w, tri):
    tb = TIME_BLOCK
    nc = seq // tb
    hg = ML_GROUP
    n_groups = ML_HEADS // hg
    qw, vw = hg * ML_DK, hg * ML_DV
    small_i = SSD_HEADS
    small_f = SSD_HEADS + ML_HEADS
    src = jnp.arange(LANES)[None, :, None]
    dst = jnp.arange(LANES)[None, None, :]
    grp = jnp.arange(n_groups)[:, None, None]
    head = grp * hg + dst // 2
    sel = ((dst < 2 * hg) & (((dst % 2 == 0) & (src == small_i + head))
                             | ((dst % 2 == 1) & (src == small_f + head)))).astype(BF16)
    gate_bias = jnp.stack([i_bias.reshape(n_groups, hg), f_bias.reshape(n_groups, hg)], axis=-1)
    gate_bias = jnp.pad(gate_bias.reshape(n_groups, 1, 2 * hg), ((0, 0), (0, 0), (0, LANES - 2 * hg)))
    conv_b2 = _row2(conv_b)
    kq = n_groups
    row = lambda b, h, c: b * nc + c
    in_specs = [
        pl.BlockSpec((tb, qw), lambda b, h, c: (row(b, h, c), C_QK // qw + h)),
        pl.BlockSpec((tb, qw), lambda b, h, c: (row(b, h, c), C_QK // qw + kq + h)),
        pl.BlockSpec((tb, vw), lambda b, h, c: (row(b, h, c), C_MV // vw + h)),
        pl.BlockSpec((tb, vw), lambda b, h, c: (row(b, h, c), C_MO // vw + h)),
        pl.BlockSpec((tb, LANES), lambda b, h, c: (row(b, h, c), C_SMALL // LANES)),
        pl.BlockSpec((None, LANES, LANES), lambda b, h, c: (h, 0, 0)),
        pl.BlockSpec((4, qw), lambda b, h, c: (0, h)),
        pl.BlockSpec((1, qw), lambda b, h, c: (0, h)),
        pl.BlockSpec((4, qw), lambda b, h, c: (0, kq + h)),
        pl.BlockSpec((1, qw), lambda b, h, c: (0, kq + h)),
        pl.BlockSpec((None, 1, LANES), lambda b, h, c: (h, 0, 0)),
        pl.BlockSpec((1, vw), lambda b, h, c: (0, h)),
        pl.BlockSpec((tb, tb), lambda b, h, c: (0, 0)),
        pl.BlockSpec((tb, 3 * (tb + CONV_HALO)), lambda b, h, c: (0, 0)),
    ]
    return pl.pallas_call(
        _mlstm_body,
        grid=(bsz, n_groups, nc),
        in_specs=in_specs,
        out_specs=pl.BlockSpec((tb, vw), lambda b, h, c: (row(b, h, c), h)),
        out_shape=jax.ShapeDtypeStruct((bsz * seq, D_MODEL), BF16),
        scratch_shapes=[pltpu.VMEM((tb + CONV_HALO, qw), F32),
                        pltpu.VMEM((tb + CONV_HALO, qw), F32),
                        pltpu.VMEM((hg, ML_DK, ML_DV), F32),
                        pltpu.VMEM((hg, HALO, LANES), F32),
                        pltpu.VMEM((hg, HALO, LANES), F32)],
        compiler_params=_cparams(3),
        name="mlstm_mixer",
    )(proj, proj, proj, proj, proj, sel, conv_w, conv_b2, conv_w, conv_b2, gate_bias,
      _row2(norm_w), tri, _conv_shift_matrix(tb))


def _stack_heads(x):
    lane = lax.broadcasted_iota(jnp.int32, x.shape, 1)
    return jnp.concatenate([jnp.where(lane < RWKV_HEAD_DIM, x, 0.0),
                            jnp.where(lane >= RWKV_HEAD_DIM, x, 0.0)], axis=0)


def _shift_lerp(buf, src_ref, mu_ref, rows):
    buf[HALO:HALO + rows, :] = src_ref[...]
    cur = buf[HALO:HALO + rows, :]
    prev = buf[HALO - 1:HALO - 1 + rows, :]
    buf[0:HALO, :] = buf[rows:rows + HALO, :]
    return cur + (prev - cur) * mu_ref[...]


RWKV_GROUP = 8
RWKV_PASSES = dict(scores=1, solve=1, values=1, state=1)
RWKV_TIME_BLOCK = 256
HEAD_SUM_PASSES = 1


def _rwkv_body(r_ref, k_ref, v_ref, lo_ref, mur_ref, muk_ref, muv_ref, mulo_ref, w0_ref, w2_ref,
               a0_ref, a2_ref, g2_ref, kk_ref, ka_ref, rk_ref, lnw_ref, lnb_ref, tri_ref,
               ones_ref, o_ref, rbuf, kbuf, vbuf, lobuf, z_ref, y_buf):
    rows = r_ref.shape[0]
    n_pairs = r_ref.shape[1] // LANES
    ch = RWKV_CHUNK
    hd = RWKV_HEAD_DIM

    @pl.when(pl.program_id(2) == 0)
    def _():
        for buf in (rbuf, kbuf, vbuf, lobuf):
            buf[0:HALO, :] = jnp.zeros((HALO, buf.shape[1]), F32)
        z_ref[...] = jnp.zeros(z_ref.shape, F32)

    r = _shift_lerp(rbuf, r_ref, mur_ref, rows)
    k = _shift_lerp(kbuf, k_ref, muk_ref, rows)
    v = _shift_lerp(vbuf, v_ref, muv_ref, rows)
    lo = _shift_lerp(lobuf, lo_ref, mulo_ref, rows)
    w_lo, a_lo, g_lo = lo[:, 0:LANES], lo[:, LANES:2 * LANES], lo[:, 2 * LANES:]

    log_w = -jax.nn.softplus(-(w0_ref[...] + _mm(jnp.tanh(w_lo), w2_ref[...], 3))) - 0.5
    lw = -jnp.exp(log_w)
    a = jax.nn.sigmoid(a0_ref[...] + _mm(a_lo, a2_ref[...], 3))
    gate = _mm(jax.nn.sigmoid(g_lo), g2_ref[...], 1)

    head_sum = ones_ref[...]

    def head_total(x):
        return jnp.concatenate([_dotx_l(x[:, p * LANES:(p + 1) * LANES], head_sum, HEAD_SUM_PASSES)
                                for p in range(n_pairs)], axis=1)

    kk = k * kk_ref[...]
    kk = kk * lax.rsqrt(jnp.maximum(head_total(kk * kk), 1e-24))
    kh = k * (1.0 + (a - 1.0) * ka_ref[...])
    kb = kk * a
    cum = _dotx_r(tri_ref[...], lw, 2)

    c_row = lax.broadcasted_iota(jnp.int32, (ch, LANES), 0)
    c_col = lax.broadcasted_iota(jnp.int32, (ch, LANES), 1) % hd
    strict = c_col < c_row
    incl = c_col <= c_row
    same_sub = (c_col // RWKV_SUB) == (c_row // RWKV_SUB)
    eye_pair = (c_col == c_row).astype(F32)
    s_row = lax.broadcasted_iota(jnp.int32, (LANES, LANES), 0)
    s_col = lax.broadcasted_iota(jnp.int32, (LANES, LANES), 1)
    same_head = (s_row // hd) == (s_col // hd)
    eye_full = (s_row == s_col).astype(F32)

    n_chunks = rows // ch
    cat = jnp.concatenate
    mm_scores = functools.partial(_mm, passes=RWKV_PASSES["scores"])
    mm_solve = functools.partial(_mm, passes=RWKV_PASSES["solve"])
    mm_values = functools.partial(_mm, passes=RWKV_PASSES["values"])
    mm_state = functools.partial(_mm, passes=RWKV_PASSES["state"])
    a_hat, r_hat, k_til, b_til, k_end, b_end, vp, decay_end = [], [], [], [], [], [], [], []
    for j in range(n_chunks):
        sl = slice(j * ch, (j + 1) * ch)
        cm, lwj, kkj, khj, kbj, rj, vj = cum[sl], lw[sl], kk[sl], kh[sl], kb[sl], r[sl], v[sl]
        cm_end = cm[ch - 1:ch, :]
        e_neg = jnp.exp(-cm)
        e_end = jnp.exp(cm_end - cm)
        wide = (-kkj * jnp.exp(cm - lwj), rj * jnp.exp(cm), khj * e_neg, kbj * e_neg,
                khj * e_end, kbj * e_end, vj, jnp.exp(cm_end))
        for pi in range(n_pairs):
            ls = slice(pi * LANES, (pi + 1) * LANES)
            for dst, src in zip((a_hat, r_hat, k_til, b_til, k_end, b_end, vp, decay_end), wide):
                dst.append(src[:, ls])
    units = range(n_chunks * n_pairs)
    each = lambda fn: [fn(u) for u in units]

    big = each(lambda u: mm_scores(cat([a_hat[u], r_hat[u]], axis=0),
                                   cat([_stack_heads(k_til[u]), _stack_heads(b_til[u])], axis=0),
                                   dot=_dot_nt))
    l_ak = each(lambda u: jnp.where(strict, big[u][0:ch, 0:LANES], 0.0))
    l_ab = each(lambda u: jnp.where(strict, big[u][0:ch, LANES:], 0.0))
    m_rk = each(lambda u: jnp.where(incl, big[u][ch:, 0:LANES], 0.0))
    m_rb = each(lambda u: jnp.where(incl, big[u][ch:, LANES:], 0.0))

    l_d = each(lambda u: jnp.where(same_sub, l_ab[u], 0.0))
    l_o = each(lambda u: l_ab[u] - l_d[u])
    sub_levels = RWKV_SUB.bit_length() - 1
    blk_levels = (ch // RWKV_SUB).bit_length() - 1
    x = each(lambda u: eye_pair + l_d[u])
    p = each(lambda u: mm_solve(l_d[u], _stack_heads(l_d[u])))
    for _ in range(sub_levels - 2):
        both = each(lambda u: mm_solve(cat([p[u], x[u]], axis=0), _stack_heads(p[u])))
        x = each(lambda u: x[u] + both[u][ch:])
        p = each(lambda u: both[u][:ch])
    t_d = each(lambda u: x[u] + mm_solve(x[u], _stack_heads(p[u])))
    nn = each(lambda u: mm_solve(t_d[u], _stack_heads(l_o[u])))
    mx = each(lambda u: eye_pair + nn[u])
    pw = nn
    for _ in range(blk_levels - 1):
        pw = each(lambda u: mm_solve(pw[u], _stack_heads(pw[u])))
        mx = each(lambda u: mx[u] + mm_solve(mx[u], _stack_heads(pw[u])))
    t = each(lambda u: mm_solve(mx[u], _stack_heads(t_d[u])))

    lmv = each(lambda u: mm_values(cat([l_ak[u], m_rk[u]], axis=0), _stack_heads(vp[u])))
    wu = each(lambda u: mm_values(
        t[u], cat([_stack_heads(a_hat[u]), _stack_heads(lmv[u][:ch])], axis=1)))
    w_a = each(lambda u: wu[u][:, 0:LANES])
    u_v = each(lambda u: wu[u][:, LANES:])
    mw = each(lambda u: mm_values(
        m_rb[u], cat([_stack_heads(w_a[u]), _stack_heads(u_v[u])], axis=1)))
    q_eff = each(lambda u: r_hat[u] + mw[u][:, 0:LANES])
    y_loc = each(lambda u: lmv[u][ch:] + mw[u][:, LANES:])
    a_eff = each(lambda u: jnp.where(same_head, mm_state(b_end[u], w_a[u], dot=_dot_tn), 0.0)
                 + eye_full * decay_end[u])
    z_loc = each(lambda u: jnp.where(
        same_head, mm_state(cat([k_end[u], b_end[u]], axis=0), cat([vp[u], u_v[u]], axis=0),
                            dot=_dot_tn), 0.0))

    z = [z_ref[pi] for pi in range(n_pairs)]
    for j in range(n_chunks):
        sl = slice(j * ch, (j + 1) * ch)
        us = [j * n_pairs + pi for pi in range(n_pairs)]
        qz = [mm_state(cat([q_eff[u], a_eff[u]], axis=0), z[pi]) for pi, u in enumerate(us)]
        for pi, u in enumerate(us):
            y_buf[sl, pi * LANES:(pi + 1) * LANES] = qz[pi][:ch] + y_loc[u]
        z = [qz[pi][ch:] + z_loc[u] for pi, u in enumerate(us)]
    for pi in range(n_pairs):
        z_ref[pi] = z[pi]

    y = y_buf[...]
    mean = head_total(y) * (1.0 / hd)
    yc = y - mean
    var = head_total(yc * yc) * (1.0 / hd)
    yn = yc * lax.rsqrt(var + RWKV_LN_EPS) * lnw_ref[...] + lnb_ref[...]
    bonus = head_total(r * kh * rk_ref[...]) * v
    o_ref[...] = ((yn + bonus) * gate).astype(o_ref.dtype)


def _rwkv_mixer(proj, bsz, seq, mu, w0, w2, a0, a2, g2, k_k, k_a, r_k, ln_w, ln_b):
    tb = RWKV_TIME_BLOCK
    nc = seq // tb
    d = D_MODEL
    lora_w = 2 * LANES + RWKV_LORA_G
    mur, muk, muv = _row2(mu[0:d]), _row2(mu[d:2 * d]), _row2(mu[2 * d:3 * d])
    o = 3 * d
    pad_l = lambda x, n: jnp.pad(x, (0, n - x.shape[0]))
    mulo = _row2(jnp.concatenate([pad_l(mu[o:o + RWKV_LORA_W], LANES),
                                  pad_l(mu[o + RWKV_LORA_W:o + RWKV_LORA_W + RWKV_LORA_A], LANES),
                                  mu[o + RWKV_LORA_W + RWKV_LORA_A:]]))
    w2p = jnp.pad(w2, ((0, LANES - RWKV_LORA_W), (0, 0)))
    a2p = jnp.pad(a2, ((0, LANES - RWKV_LORA_A), (0, 0)))
    ch = RWKV_CHUNK
    t_i = jnp.arange(tb)[:, None]
    t_j = jnp.arange(tb)[None, :]
    tri = ((t_i >= t_j) & (t_i // ch == t_j // ch)).astype(BF16)
    l_i = jnp.arange(LANES)[:, None]
    l_j = jnp.arange(LANES)[None, :]
    head_sum = (l_i // RWKV_HEAD_DIM == l_j // RWKV_HEAD_DIM).astype(BF16)

    gw = RWKV_GROUP * LANES
    row = lambda b, p, c: b * nc + c
    col = lambda base: (lambda b, p, c: (row(b, p, c), base // gw + p))
    vec = pl.BlockSpec((1, gw), lambda b, p, c: (0, p))
    in_specs = [
        pl.BlockSpec((tb, gw), col(C_R)),
        pl.BlockSpec((tb, gw), col(C_K)),
        pl.BlockSpec((tb, gw), col(C_V)),
        pl.BlockSpec((tb, lora_w), lambda b, p, c: (row(b, p, c), C_LORA // lora_w)),
        vec, vec, vec,
        pl.BlockSpec((1, lora_w), lambda b, p, c: (0, 0)),
        vec,
        pl.BlockSpec((LANES, gw), lambda b, p, c: (0, p)),
        vec,
        pl.BlockSpec((LANES, gw), lambda b, p, c: (0, p)),
        pl.BlockSpec((RWKV_LORA_G, gw), lambda b, p, c: (0, p)),
        vec, vec, vec, vec, vec,
        pl.BlockSpec((tb, tb), lambda b, p, c: (0, 0)),
        pl.BlockSpec((LANES, LANES), lambda b, p, c: (0, 0)),
    ]
    return pl.pallas_call(
        _rwkv_body,
        grid=(bsz, RWKV_PAIRS // RWKV_GROUP, nc),
        in_specs=in_specs,
        out_specs=pl.BlockSpec((tb, gw), lambda b, p, c: (row(b, p, c), p)),
        out_shape=jax.ShapeDtypeStruct((bsz * seq, D_MODEL), BF16),
        scratch_shapes=[pltpu.VMEM((tb + HALO, gw), F32),
                        pltpu.VMEM((tb + HALO, gw), F32),
                        pltpu.VMEM((tb + HALO, gw), F32),
                        pltpu.VMEM((tb + HALO, lora_w), F32),
                        pltpu.VMEM((RWKV_GROUP, LANES, LANES), F32),
                        pltpu.VMEM((tb, gw), F32)],
        compiler_params=_cparams(3),
        name="rwkv_mixer",
    )(proj, proj, proj, proj, mur, muk, muv, mulo, _row2(w0), w2p, _row2(a0), a2p, g2,
      _row2(k_k), _row2(k_a), _row2(r_k.reshape(-1)), _row2(ln_w), _row2(ln_b), tri, head_sum)


def _gated_merge_body(ys_ref, yr_ref, ym_ref, ps_ref, pr_ref, pm_ref, gs_ref, gr_ref, gm_ref, o_ref):
    acc = jax.nn.sigmoid(gs_ref[...]) * _dot(ys_ref[...], ps_ref[...])
    acc = acc + jax.nn.sigmoid(gr_ref[...]) * _dot(yr_ref[...], pr_ref[...])
    acc = acc + jax.nn.sigmoid(gm_ref[...]) * _dot(ym_ref[...], pm_ref[...])
    o_ref[...] = acc.astype(o_ref.dtype)


def _gated_merge(ys, ps, proj, tm=TM_GATED, tn=TN_GATED):
    m, k = ys[0].shape
    tm = min(tm, m)
    n = ps[0].shape[1]
    y_spec = pl.BlockSpec((tm, k), lambda j, i: (i, 0))
    p_spec = pl.BlockSpec((k, tn), lambda j, i: (0, j), pipeline_mode=pl.Buffered(1))
    gate_spec = lambda col: pl.BlockSpec((tm, tn), lambda j, i: (i, col // tn + j))
    return pl.pallas_call(
        _gated_merge_body,
        grid=(n // tn, m // tm),
        in_specs=[y_spec] * 3 + [p_spec] * 3 + [gate_spec(C_GS), gate_spec(C_GR), gate_spec(C_GM)],
        out_specs=pl.BlockSpec((tm, tn), lambda j, i: (i, j)),
        out_shape=jax.ShapeDtypeStruct((m, n), BF16),
        compiler_params=_cparams(2),
        name="gated_merge",
    )(*ys, *ps, proj, proj, proj)


def _mm_res_ln_body(a_ref, w_ref, h_ref, g_ref, b_ref, o_ref, ob_ref, *, row_split):
    n = a_ref.shape[0] // row_split
    subs = [slice(s * n, (s + 1) * n) for s in range(row_split)]
    acc = [_dot(a_ref[rows, :], w_ref[...]) for rows in subs]
    for rows, mix in zip(subs, acc):
        h = _layer_norm_rows(ALPHA * h_ref[rows, :] + mix, g_ref[...], b_ref[...])
        o_ref[rows, :] = h
        ob_ref[rows, :] = h.astype(BF16)


def _matmul_res_ln(a, w, h, g, b, tm, row_split):
    m, k = a.shape
    n = w.shape[1]
    const = lambda shape: pl.BlockSpec(shape, lambda i: (0, 0), pipeline_mode=pl.Buffered(1))
    return pl.pallas_call(
        functools.partial(_mm_res_ln_body, row_split=row_split),
        grid=(m // tm,),
        in_specs=[pl.BlockSpec((tm, k), lambda i: (i, 0)),
                  const((k, n)),
                  pl.BlockSpec((tm, n), lambda i: (i, 0)),
                  const((1, n)), const((1, n))],
        out_specs=[pl.BlockSpec((tm, n), lambda i: (i, 0)),
                   pl.BlockSpec((tm, n), lambda i: (i, 0))],
        out_shape=[jax.ShapeDtypeStruct((m, n), F32), jax.ShapeDtypeStruct((m, n), BF16)],
        compiler_params=_cparams(1),
        name="matmul_res_ln",
    )(a, w, h, _row2(g), _row2(b))


FFN_HALO = 16


def _ffn_up_body(h_ref, hp_ref, wg_ref, wua_ref, wub_ref, cwg_ref, cbg_ref, cwu_ref, cbu_ref, o_ref,
                 gbuf, ubuf, wg_s, wu_s, *, blocks_per_seq):
    rows = h_ref.shape[0]
    tn = o_ref.shape[1]
    first = (pl.program_id(1) % blocks_per_seq) == 0
    keep = jnp.where(first, 0.0, 1.0)

    @pl.when(pl.program_id(1) == 0)
    def _():
        col = pl.program_id(0) * tn + lax.broadcasted_iota(jnp.int32, wg_s.shape, 1)
        off = FFN_D % tn
        up = jnp.concatenate([wua_ref[:, off:], wub_ref[:, :off]], axis=1)
        wg_s[...] = jnp.where(col < FFN_D, wg_ref[...], 0.0).astype(BF16)
        wu_s[...] = jnp.where(col < FFN_D, up, 0.0).astype(BF16)

    def conv(buf, cw_ref, cb_ref):
        acc = cb_ref[...] + cw_ref[2:3, :] * buf[FFN_HALO:FFN_HALO + rows, :]
        acc = acc + cw_ref[1:2, :] * buf[FFN_HALO - 1:FFN_HALO - 1 + rows, :]
        return acc + cw_ref[0:1, :] * buf[FFN_HALO - 2:FFN_HALO - 2 + rows, :]

    for buf, w_ref in ((gbuf, wg_s), (ubuf, wu_s)):
        buf[0:FFN_HALO, :] = _dot(hp_ref[...], w_ref[...]) * keep
        buf[FFN_HALO:FFN_HALO + rows, :] = _dot(h_ref[...], w_ref[...])
    gate = conv(gbuf, cwg_ref, cbg_ref)
    up = conv(ubuf, cwu_ref, cbu_ref)
    gelu = 0.5 * gate * (1.0 + lax.erf(gate * (2.0 ** -0.5)))
    o_ref[...] = (gelu * up).astype(o_ref.dtype)


def _ffn_up(hb, w_up, layer, cwg, cbg, cwu, cbu, seq, tm):
    m, k = hb.shape
    n = FFN_PAD
    tn = FFN_TN
    halo_blocks = tm // FFN_HALO
    up0 = FFN_D // tn
    body = functools.partial(_ffn_up_body, blocks_per_seq=seq // tm)
    w_spec = lambda first_block: pl.BlockSpec((None, k, tn), lambda j, i: (layer, 0, first_block + j))
    return pl.pallas_call(
        body,
        grid=(n // tn, m // tm),
        in_specs=[pl.BlockSpec((tm, k), lambda j, i: (i, 0)),
                  pl.BlockSpec((FFN_HALO, k), lambda j, i: (jnp.maximum(i * halo_blocks - 1, 0), 0)),
                  w_spec(0), w_spec(up0), w_spec(up0 + 1),
                  pl.BlockSpec((3, tn), lambda j, i: (0, j)),
                  pl.BlockSpec((1, tn), lambda j, i: (0, j)),
                  pl.BlockSpec((3, tn), lambda j, i: (0, j)),
                  pl.BlockSpec((1, tn), lambda j, i: (0, j))],
        out_specs=pl.BlockSpec((tm, tn), lambda j, i: (i, j)),
        out_shape=jax.ShapeDtypeStruct((m, n), BF16),
        scratch_shapes=[pltpu.VMEM((tm + FFN_HALO, tn), F32),
                        pltpu.VMEM((tm + FFN_HALO, tn), F32),
                        pltpu.VMEM((k, tn), BF16),
                        pltpu.VMEM((k, tn), BF16)],
        compiler_params=_cparams(2),
        name="ffn_up",
    )(hb, hb, w_up, w_up, w_up, cwg, cbg, cwu, cbu)


def _pack_segments():
    d = D_MODEL
    order = (("z", d), ("x", d), ("bc", 2 * SSD_GN), ("dt", SSD_HEADS), ("r", d), ("k", d), ("v", d),
             ("w_lo", RWKV_LORA_W), ("a_lo", RWKV_LORA_A), ("g_lo", RWKV_LORA_G), ("qk", d),
             ("mv", d), ("mo", d), ("gi", ML_HEADS), ("gf", ML_HEADS), ("gs", d), ("gr", d), ("gm", d))
    dst = dict(z=C_Z, x=C_X, r=C_R, k=C_K, v=C_V, qk=C_QK, mv=C_MV, mo=C_MO, gs=C_GS, gr=C_GR,
               gm=C_GM, bc=C_BC, w_lo=C_LORA, a_lo=C_LORA + LANES, g_lo=C_LORA + 2 * LANES,
               dt=C_SMALL, gi=C_SMALL + SSD_HEADS, gf=C_SMALL + SSD_HEADS + ML_HEADS)
    segments, src = [], 0
    for name, width in order:
        segments.append((src, width, dst[name]))
        src += width
    return tuple(segments), src


def _pack_body(wt_ref, o_ref, *, segments):
    group, start, pos = [], 0, 0

    def flush():
        nonlocal group, start
        rows = group[0] if len(group) == 1 else jnp.concatenate(group, axis=0)
        o_ref[:, start:pos] = rows.T.astype(o_ref.dtype)
        group, start = [], pos

    for src, width, dst in sorted(segments, key=lambda s: s[2]) + [(None, 0, N_PROJ)]:
        if dst > pos:
            group.append(jnp.zeros((dst - pos, LANES), F32))
            pos = dst
            if pos % LANES == 0:
                flush()
        if width:
            group.append(wt_ref[src:src + width, :])
            pos += width
            if pos % LANES == 0:
                flush()


def _pack_w_in(w_in, layer):
    segments, d_in = _pack_segments()
    d = w_in.shape[1]
    return pl.pallas_call(
        functools.partial(_pack_body, segments=segments),
        grid=(d // LANES,),
        in_specs=[pl.BlockSpec((None, d_in, LANES), lambda i: (layer, 0, i))],
        out_specs=pl.BlockSpec((LANES, N_PROJ), lambda i: (i, 0)),
        out_shape=jax.ShapeDtypeStruct((d, N_PROJ), BF16),
        compiler_params=_cparams(1),
        name="pack_w_in",
    )(jnp.swapaxes(w_in, 1, 2))


def _to_bf16_body(w_ref, o_ref, *, valid_rows):
    x = w_ref[...]
    if valid_rows is not None:
        row = pl.program_id(0) * o_ref.shape[0] + lax.broadcasted_iota(jnp.int32, o_ref.shape, 0)
        x = jnp.where(row < valid_rows, x, 0.0)
    o_ref[...] = x.astype(o_ref.dtype)


def _to_bf16(w_stack, layer, out_rows=None, rows_per_step=512):
    _, rows, cols = w_stack.shape
    out_rows = rows if out_rows is None else out_rows
    tr = rows_per_step
    last = (rows - 1) // tr
    return pl.pallas_call(
        functools.partial(_to_bf16_body, valid_rows=None if out_rows == rows else rows),
        grid=(out_rows // tr,),
        in_specs=[pl.BlockSpec((None, tr, cols), lambda i: (layer, jnp.minimum(i, last), 0))],
        out_specs=pl.BlockSpec((tr, cols), lambda i: (i, 0)),
        out_shape=jax.ShapeDtypeStruct((out_rows, cols), BF16),
        compiler_params=_cparams(1),
        name="to_bf16",
    )(w_stack)


def _pad_cols(x, n):
    return jnp.pad(x, ((0, 0), (0, n - x.shape[1])))


def kernel(x, ln_in_g, ln_in_b, w_in, ssd_conv_w, ssd_conv_b, ssd_dt_bias, ssd_a_log, ssd_d, ssd_norm_w, rwkv_mu, rwkv_w0, rwkv_w2, rwkv_a0, rwkv_a2, rwkv_g2, rwkv_k_k, rwkv_k_a, rwkv_r_k, rwkv_ln_w, rwkv_ln_b, mlstm_conv_w, mlstm_conv_b, mlstm_i_bias, mlstm_f_bias, mlstm_norm_w, proj_ssd, proj_rwkv, proj_mlstm, w_out, ln1_g, ln1_b, ffn_w_up, ffn_conv_w, ffn_conv_b, ffn_w_down, ln2_g, ln2_b):
    bsz, seq, d = x.shape
    n_tok = bsz * seq
    tb = TIME_BLOCK
    t_i = jnp.arange(tb)[:, None]
    t_j = jnp.arange(tb)[None, :]
    tri = (t_i >= t_j).astype(BF16)

    h, hb = _ln_in(x.reshape(n_tok, d), ln_in_g, ln_in_b)
    for l in range(DEPTH):
        proj = _matmul(hb, _pack_w_in(w_in, l), tm=min(TM_IN_PROJ, n_tok), tn=TN_IN_PROJ,
                       out_dtype=F32, name="in_proj")
        y_ssd = _ssd_mixer(proj, bsz, seq, ssd_conv_w[l], ssd_conv_b[l], ssd_dt_bias[l],
                           ssd_a_log[l], ssd_d[l], ssd_norm_w[l], tri)
        y_rwkv = _rwkv_mixer(proj, bsz, seq, rwkv_mu[l], rwkv_w0[l], rwkv_w2[l], rwkv_a0[l],
                             rwkv_a2[l], rwkv_g2[l], rwkv_k_k[l], rwkv_k_a[l], rwkv_r_k[l],
                             rwkv_ln_w[l], rwkv_ln_b[l])
        y_mlstm = _mlstm_mixer(proj, bsz, seq, mlstm_conv_w[l], mlstm_conv_b[l], mlstm_i_bias[l],
                               mlstm_f_bias[l], mlstm_norm_w[l], tri)
        merged = _gated_merge(
            (y_ssd, y_rwkv, y_mlstm),
            (_to_bf16(proj_ssd, l), _to_bf16(proj_rwkv, l), _to_bf16(proj_mlstm, l)), proj)
        h, hb = _matmul_res_ln(merged, _to_bf16(w_out, l), h, ln1_g[l], ln1_b[l],
                               tm=min(TM_W_OUT, n_tok), row_split=4)

        cwg = _pad_cols(ffn_conv_w[l][:, :FFN_D], FFN_PAD)
        cwu = _pad_cols(ffn_conv_w[l][:, FFN_D:], FFN_PAD)
        cbg = _pad_cols(_row2(ffn_conv_b[l][:FFN_D]), FFN_PAD)
        cbu = _pad_cols(_row2(ffn_conv_b[l][FFN_D:]), FFN_PAD)
        act = _ffn_up(hb, ffn_w_up, l, cwg, cbg, cwu, cbu, seq, tm=min(TM_FFN_UP, seq))
        wd = _to_bf16(ffn_w_down, l, out_rows=FFN_PAD)
        h, hb = _matmul_res_ln(act, wd, h, ln2_g[l], ln2_b[l], tm=min(TM_FFN_DOWN, n_tok),
                               row_split=1)
    return h.reshape(bsz, seq, d)
```
